```python
import math
import jax, jax.numpy as jnp
from jax import lax
import numpy as np

D_MODEL = 2048
BATCH = 2
SEQ = 4096
DEPTH = 2
DEC_BATCH = 32
DEC_SEQ = 8
PAST_LEN = 8192
PAGE_SIZE = 128

H_A = 8
DK_A = 128
DV_A = 128
QKV_A = H_A * (2 * DK_A + DV_A)
W_A = H_A * DV_A
CONV_W = 4
CHUNK_A = 64
H_B = 8
HD_B = 128
W_B = H_B * HD_B
MOBA_BLOCK = 256
MOBA_TOPK = 3
MOBA_QB = 64
PAGES_PER_BLOCK = MOBA_BLOCK // PAGE_SIZE
H_C = 4
DK_C = 256
DV_C = 256
QKV_C = H_C * (2 * DK_C + DV_C)
W_C = H_C * DV_C
CHUNK_C = 64
EPS = 1e-6
SPLIT_SIZES = (QKV_A, W_A, H_A, H_A, 3 * W_B, W_B, QKV_C, W_C, W_C, 2 * H_C, 3 * D_MODEL)
D_IN = sum(SPLIT_SIZES)

kernel_name = "hybrid_deltanet_moba_mlstm_gated_merge_step"


def rms_norm(x, g):
    xf = x.astype(jnp.float32)
    y = xf * lax.rsqrt(jnp.mean(xf * xf, axis=-1, keepdims=True) + EPS)
    return (y * g.astype(jnp.float32)).astype(x.dtype)


def l2_norm(x):
    xf = x.astype(jnp.float32)
    return xf * lax.rsqrt(jnp.sum(xf * xf, axis=-1, keepdims=True) + EPS)


def split_cols(p):
    idx, acc = [], 0
    for s in SPLIT_SIZES[:-1]:
        acc += s
        idx.append(acc)
    return jnp.split(p, idx, axis=-1)


def to_chunks(t, c):
    b, l, h = t.shape[:3]
    t = t.reshape(b, l // c, c, h, *t.shape[3:])
    return jnp.moveaxis(jnp.moveaxis(t, 1, 0), 3, 2)


def from_chunks(t):
    n, b, h, c = t.shape[:4]
    t = jnp.moveaxis(jnp.moveaxis(t, 2, 3), 0, 1)
    return t.reshape(b, n * c, h, *t.shape[4:])


def causal_conv(x, buf, w):
    l = x.shape[1]
    xp = jnp.concatenate([buf.astype(x.dtype), x], axis=1)
    y = xp[:, 0:l] * w[0]
    for j in range(1, CONV_W):
        y = y + xp[:, j:j + l] * w[j]
    return jax.nn.silu(y), xp[:, l:]


def gated_delta(q, k, v, beta, g, s0):
    l = q.shape[1]
    c = math.gcd(l, CHUNK_A)
    q, k, v = (to_chunks(t, c) for t in (q * DK_A ** -0.5, k, v))
    beta, g = (to_chunks(t, c) for t in (beta, g))
    causal = jnp.tril(jnp.ones((c, c), bool))
    strict = jnp.tril(jnp.ones((c, c), bool), -1)
    G = jnp.cumsum(g, axis=-1)
    decay = jnp.exp(jnp.where(causal, G[..., :, None] - G[..., None, :], -jnp.inf))
    a_mat = jnp.where(strict, beta[..., :, None] * decay * jnp.einsum('nbhik,nbhjk->nbhij', k, k), 0.0)
    t_mat = a_mat + jnp.eye(c, dtype=a_mat.dtype)

    def solve(rhs):
        return lax.linalg.triangular_solve(t_mat, rhs, left_side=True, lower=True, unit_diagonal=True)

    u = solve(beta[..., None] * v)
    kt = solve((beta * jnp.exp(G))[..., None] * k)
    p = decay * jnp.einsum('nbhik,nbhjk->nbhij', q, k)
    qg = q * jnp.exp(G)[..., None]
    kend = k * jnp.exp(G[..., -1:] - G)[..., None]
    gc = jnp.exp(G[..., -1])

    def step(s, inp):
        u_n, kt_n, p_n, qg_n, kend_n, gc_n = inp
        w = u_n - jnp.einsum('bhck,bhkv->bhcv', kt_n, s)
        o = jnp.einsum('bhck,bhkv->bhcv', qg_n, s) + jnp.einsum('bhij,bhjv->bhiv', p_n, w)
        s = gc_n[..., None, None] * s + jnp.einsum('bhck,bhcv->bhkv', kend_n, w)
        return s, o

    s, o = lax.scan(step, s0, (u, kt, p, qg, kend, gc))
    return from_chunks(o), s


def mlstm(q, k, v, i_pre, log_f, c0, n0, m0):
    l = q.shape[1]
    c = math.gcd(l, CHUNK_C)
    q, k, v = (to_chunks(t, c) for t in (q, k * DK_C ** -0.5, v))
    i_pre, log_f = (to_chunks(t, c) for t in (i_pre, log_f))
    causal = jnp.tril(jnp.ones((c, c), bool))

    def step(carry, inp):
        cm, n, m = carry
        q_n, k_n, v_n, i_n, f_n = inp
        b = jnp.cumsum(f_n, axis=-1)
        inter = b + m[..., None]
        intra = jnp.where(causal, b[..., :, None] - b[..., None, :] + i_n[..., None, :], -jnp.inf)
        m_t = jnp.maximum(inter, jnp.max(intra, axis=-1))
        w_inter = jnp.exp(inter - m_t)
        s = jnp.einsum('bhik,bhjk->bhij', q_n, k_n) * jnp.exp(intra - m_t[..., None])
        num = w_inter[..., None] * jnp.einsum('bhik,bhkv->bhiv', q_n, cm) + jnp.einsum('bhij,bhjv->bhiv', s, v_n)
        den = w_inter * jnp.einsum('bhik,bhk->bhi', q_n, n) + jnp.sum(s, axis=-1)
        h = num / jnp.maximum(jnp.abs(den), jnp.exp(-m_t))[..., None]
        m_new = m_t[..., -1]
        w_c = jnp.exp(b[..., -1] + m - m_new)
        w_j = jnp.exp(b[..., -1:] - b + i_n - m_new[..., None])
        cm = w_c[..., None, None] * cm + jnp.einsum('bhc,bhck,bhcv->bhkv', w_j, k_n, v_n)
        n = w_c[..., None] * n + jnp.einsum('bhc,bhck->bhk', w_j, k_n)
        return (cm, n, m_new), h

    (cm, n, m), h = lax.scan(step, (c0, n0, m0), (q, k, v, i_pre, log_f))
    return from_chunks(h), cm, n, m


def gather_rows(pool, ptab, start, stop):
    p0 = start // PAGE_SIZE
    p1 = -(-stop // PAGE_SIZE)
    b = ptab.shape[0]
    rows = pool[ptab[:, p0:p1]].reshape(b, (p1 - p0) * PAGE_SIZE, *pool.shape[2:])
    return rows[:, start - p0 * PAGE_SIZE: stop - p0 * PAGE_SIZE]


def moba_core(q, qpos, own_k, own_v, own_kpos, kbar, k_pool, v_pool, ptab, n_sel):
    bsz, nq, nh, _ = q.shape
    own_blk = qpos // MOBA_BLOCK
    own_mask = (own_kpos[None, :] <= qpos[:, None]) & (own_kpos[None, :] >= own_blk[:, None] * MOBA_BLOCK)
    s_own = jnp.where(own_mask, jnp.einsum('bqhd,bkhd->bhqk', q, own_k).astype(jnp.float32), -jnp.inf)
    if n_sel == 0:
        p = jax.nn.softmax(s_own, axis=-1).astype(own_v.dtype)
        return jnp.einsum('bhqk,bkhd->bqhd', p, own_v)
    nbk = kbar.shape[1]
    sc = jnp.einsum('bqhd,bnhd->bhqn', q.astype(jnp.float32), kbar)
    sc = jnp.where(jnp.arange(nbk)[None, :] < own_blk[:, None], sc, -jnp.inf)
    top_s, top_i = lax.top_k(sc, n_sel)
    lp = top_i[..., None] * PAGES_PER_BLOCK + jnp.arange(PAGES_PER_BLOCK)
    pid = ptab[jnp.arange(bsz)[:, None, None, None, None], lp]
    hid = jnp.arange(nh)[None, :, None, None, None, None]
    rows = jnp.arange(PAGE_SIZE)
    kg = k_pool[pid[..., None], rows, hid]
    vg = v_pool[pid[..., None], rows, hid]
    s_sel = jnp.einsum('bqhd,bhqnprd->bhqnpr', q, kg).astype(jnp.float32)
    s_sel = s_sel.reshape(bsz, nh, nq, n_sel * MOBA_BLOCK)
    valid = jnp.repeat(top_s > -jnp.inf, MOBA_BLOCK, axis=-1)
    s_sel = jnp.where(valid, s_sel, -jnp.inf)
    k0 = s_own.shape[-1]
    p = jax.nn.softmax(jnp.concatenate([s_own, s_sel], axis=-1), axis=-1).astype(own_v.dtype)
    p_sel = p[..., k0:].reshape(bsz, nh, nq, n_sel, PAGES_PER_BLOCK, PAGE_SIZE)
    return (jnp.einsum('bhqk,bkhd->bqhd', p[..., :k0], own_v)
            + jnp.einsum('bhqnpr,bhqnprd->bqhd', p_sel, vg))


def moba_prompt(q, k, v):
    bsz, l, nh, hd = q.shape
    nb = -(-l // MOBA_BLOCK)
    sp = nb * MOBA_BLOCK
    pad = ((0, 0), (0, sp - l), (0, 0), (0, 0))
    qp, kp, vp = jnp.pad(q, pad), jnp.pad(k, pad), jnp.pad(v, pad)
    kbar = kp.reshape(bsz, nb, MOBA_BLOCK, nh, hd).astype(jnp.float32).mean(axis=2)
    npg = sp // PAGE_SIZE
    k_pool = kp.reshape(bsz * npg, PAGE_SIZE, nh, hd)
    v_pool = vp.reshape(bsz * npg, PAGE_SIZE, nh, hd)
    ptab = jnp.arange(bsz * npg, dtype=jnp.int32).reshape(bsz, npg)
    n_sel = min(MOBA_TOPK, nb - 1)
    nqb = sp // MOBA_QB

    def one(args):
        i, qi = args
        start = ((i * MOBA_QB) // MOBA_BLOCK) * MOBA_BLOCK
        own_k = lax.dynamic_slice_in_dim(kp, start, MOBA_BLOCK, axis=1)
        own_v = lax.dynamic_slice_in_dim(vp, start, MOBA_BLOCK, axis=1)
        qpos = i * MOBA_QB + jnp.arange(MOBA_QB)
        own_kpos = start + jnp.arange(MOBA_BLOCK)
        return moba_core(qi, qpos, own_k, own_v, own_kpos, kbar, k_pool, v_pool, ptab, n_sel)

    qs = jnp.moveaxis(qp.reshape(bsz, nqb, MOBA_QB, nh, hd), 1, 0)
    out = lax.map(one, (jnp.arange(nqb), qs))
    return jnp.moveaxis(out, 0, 1).reshape(bsz, sp, nh, hd)[:, :l]


def moba_sample(q, k, v, k_pool, v_pool, page_table):
    bsz, l, nh, hd = q.shape
    past_len = page_table.shape[1] * PAGE_SIZE
    nbp = past_len // MOBA_BLOCK
    kbar = k_pool[page_table[:, :nbp * PAGES_PER_BLOCK]].reshape(
        bsz, nbp, MOBA_BLOCK, nh, hd).astype(jnp.float32).mean(axis=2)
    r = past_len - nbp * MOBA_BLOCK
    own_k = jnp.concatenate([gather_rows(k_pool, page_table, past_len - r, past_len).astype(k.dtype), k], axis=1)
    own_v = jnp.concatenate([gather_rows(v_pool, page_table, past_len - r, past_len).astype(v.dtype), v], axis=1)
    own_kpos = past_len - r + jnp.arange(r + l)
    qpos = past_len + jnp.arange(l)
    n_sel = min(MOBA_TOPK, nbp)

    def one(args):
        qi, pi = args
        return moba_core(qi, pi, own_k, own_v, own_kpos, kbar, k_pool, v_pool, page_table, n_sel)

    out = lax.map(one, (jnp.moveaxis(q, 1, 0)[:, :, None], qpos[:, None]))
    return jnp.moveaxis(out[:, :, 0], 0, 1)


def mixer_block(x, wl, conv_buf, s_a, c_c, n_c, m_c, attend):
    (ln_g, w_in, conv_a, a_log, dt_bias, norm_a, qnorm_b, knorm_b, b_if, norm_c,
     w_br_a, w_br_b, w_br_c, w_out) = wl
    bsz, l, _ = x.shape
    f32 = jnp.float32
    h = rms_norm(x, ln_g)
    (qkv_a, z_a, beta_a, alpha_a, qkv_b, z_b, qkv_c, z_c, o_c, if_c, gates) = split_cols(h @ w_in)

    qkv_a, conv_new = causal_conv(qkv_a, conv_buf, conv_a)
    qa, ka, va = jnp.split(qkv_a, [H_A * DK_A, 2 * H_A * DK_A], axis=-1)
    qa = l2_norm(qa.reshape(bsz, l, H_A, DK_A))
    ka = l2_norm(ka.reshape(bsz, l, H_A, DK_A))
    va = va.reshape(bsz, l, H_A, DV_A).astype(f32)
    beta = jax.nn.sigmoid(beta_a.astype(f32))
    g = -jnp.exp(a_log.astype(f32)) * jax.nn.softplus(alpha_a.astype(f32) + dt_bias.astype(f32))
    oa, s_new = gated_delta(qa, ka, va, beta, g, s_a.astype(f32))
    oa = rms_norm(oa, norm_a).reshape(bsz, l, W_A).astype(x.dtype) * jax.nn.silu(z_a)

    qb, kb, vb = jnp.split(qkv_b, 3, axis=-1)
    qb = rms_norm(qb.reshape(bsz, l, H_B, HD_B), qnorm_b) * HD_B ** -0.5
    kb = rms_norm(kb.reshape(bsz, l, H_B, HD_B), knorm_b)
    vb = vb.reshape(bsz, l, H_B, HD_B)
    ob = attend(qb, kb, vb).reshape(bsz, l, W_B) * jax.nn.silu(z_b)

    qc, kc, vc = jnp.split(qkv_c, [H_C * DK_C, 2 * H_C * DK_C], axis=-1)
    gate_pre = if_c.astype(f32) + b_if.astype(f32)
    hc, c_new, n_new, m_new = mlstm(
        qc.reshape(bsz, l, H_C, DK_C).astype(f32), kc.reshape(bsz, l, H_C, DK_C).astype(f32),
        vc.reshape(bsz, l, H_C, DV_C).astype(f32), gate_pre[..., :H_C],
        jax.nn.log_sigmoid(gate_pre[..., H_C:]), c_c.astype(f32), n_c.astype(f32), m_c.astype(f32))
    hc = jax.nn.sigmoid(o_c.astype(f32)).reshape(bsz, l, H_C, DV_C) * hc
    oc = rms_norm(hc, norm_c).reshape(bsz, l, W_C).astype(x.dtype) * jax.nn.silu(z_c)

    g_a, g_b, g_c = jnp.split(jax.nn.sigmoid(gates), 3, axis=-1)
    merged = g_a * (oa @ w_br_a) + g_b * (ob @ w_br_b) + g_c * (oc @ w_br_c)
    y = x + merged @ w_out
    dt = x.dtype
    return y, (kb, vb, conv_new.astype(dt), s_new.astype(dt), c_new.astype(dt),
               n_new.astype(dt), m_new.astype(dt))


def setup_inputs(seed: int = 0) -> dict:
    key = jax.random.key(seed)
    ks = jax.random.split(key, 32)
    f32 = jnp.float32

    def nrm(k, shape, scale):
        return jax.random.normal(k, shape, f32) * scale

    n_pages = PAST_LEN // PAGE_SIZE
    n_phys = (5 * DEC_BATCH * n_pages) // 4
    page_table = jax.random.permutation(ks[4], n_phys)[: DEC_BATCH * n_pages].reshape(
        DEC_BATCH, n_pages).astype(jnp.int32)
    dt = jnp.exp(jax.random.uniform(ks[10], (DEPTH, H_A), f32, math.log(1e-3), math.log(1e-1)))
    return {
        "x_prompt": nrm(ks[0], (BATCH, SEQ, D_MODEL), 1.0),
        "x_sample": nrm(ks[1], (DEC_BATCH, DEC_SEQ, D_MODEL), 1.0),
        "cache_k": nrm(ks[2], (DEPTH, n_phys, PAGE_SIZE, H_B, HD_B), 1.0),
        "cache_v": nrm(ks[3], (DEPTH, n_phys, PAGE_SIZE, H_B, HD_B), 1.0),
        "page_table": page_table,
        "state_conv_a": nrm(ks[5], (DEPTH, DEC_BATCH, CONV_W - 1, QKV_A), 1.0),
        "state_delta_a": nrm(ks[6], (DEPTH, DEC_BATCH, H_A, DK_A, DV_A), 0.1),
        "state_mlstm_c": nrm(ks[7], (DEPTH, DEC_BATCH, H_C, DK_C, DV_C), 0.1),
        "state_mlstm_n": nrm(ks[8], (DEPTH, DEC_BATCH, H_C, DK_C), 0.1),
        "state_mlstm_m": nrm(ks[9], (DEPTH, DEC_BATCH, H_C), 1.0),
        "ln_g": 1.0 + nrm(ks[11], (DEPTH, D_MODEL), 0.02),
        "w_in": nrm(ks[12], (DEPTH, D_MODEL, D_IN), D_MODEL ** -0.5),
        "conv_a": nrm(ks[13], (DEPTH, CONV_W, QKV_A), CONV_W ** -0.5),
        "a_log": jnp.log(jax.random.uniform(ks[14], (DEPTH, H_A), f32, 1.0, 16.0)),
        "dt_bias": dt + jnp.log(-jnp.expm1(-dt)),
        "norm_a": 1.0 + nrm(ks[15], (DEPTH, DV_A), 0.02),
        "qnorm_b": 1.0 + nrm(ks[16], (DEPTH, HD_B), 0.02),
        "knorm_b": 1.0 + nrm(ks[17], (DEPTH, HD_B), 0.02),
        "b_if": jnp.concatenate([nrm(ks[18], (DEPTH, H_C), 0.1),
                                 3.0 + 3.0 * jax.random.uniform(ks[19], (DEPTH, H_C), f32)], axis=-1),
        "norm_c": 1.0 + nrm(ks[20], (DEPTH, DV_C), 0.02),
        "w_br_a": nrm(ks[21], (DEPTH, W_A, D_MODEL), W_A ** -0.5),
        "w_br_b": nrm(ks[22], (DEPTH, W_B, D_MODEL), W_B ** -0.5),
        "w_br_c": nrm(ks[23], (DEPTH, W_C, D_MODEL), W_C ** -0.5),
        "w_out": nrm(ks[24], (DEPTH, D_MODEL, D_MODEL), D_MODEL ** -0.5),
    }


def reference(x_prompt, x_sample, cache_k, cache_v, page_table, state_conv_a, state_delta_a,
              state_mlstm_c, state_mlstm_n, state_mlstm_m, ln_g, w_in, conv_a, a_log, dt_bias,
              norm_a, qnorm_b, knorm_b, b_if, norm_c, w_br_a, w_br_b, w_br_c, w_out):
    weights = (ln_g, w_in, conv_a, a_log, dt_bias, norm_a, qnorm_b, knorm_b, b_if, norm_c,
               w_br_a, w_br_b, w_br_c, w_out)
    f32 = jnp.float32
    bp = x_prompt.shape[0]
    yp, ys = x_prompt, x_sample
    new_p = [[] for _ in range(7)]
    new_s = [[] for _ in range(7)]
    for l in range(DEPTH):
        wl = tuple(w[l] for w in weights)
        yp, st_p = mixer_block(
            yp, wl, jnp.zeros((bp, CONV_W - 1, QKV_A), x_prompt.dtype),
            jnp.zeros((bp, H_A, DK_A, DV_A), f32), jnp.zeros((bp, H_C, DK_C, DV_C), f32),
            jnp.zeros((bp, H_C, DK_C), f32), jnp.zeros((bp, H_C), f32), moba_prompt)
        ys, st_s = mixer_block(
            ys, wl, state_conv_a[l], state_delta_a[l], state_mlstm_c[l], state_mlstm_n[l],
            state_mlstm_m[l],
            lambda q, k, v, l=l: moba_sample(q, k, v, cache_k[l], cache_v[l], page_table))
        for i in range(7):
            new_p[i].append(st_p[i])
            new_s[i].append(st_s[i])
    k_p, v_p, conv_p, delta_p, c_p, n_p, m_p = [jnp.stack(t) for t in new_p]
    k_s, v_s, conv_s, delta_s, c_s, n_s, m_s = [jnp.stack(t) for t in new_s]
    return (yp, ys, k_p, v_p, conv_p, delta_p, c_p, n_p, m_p,
            k_s, v_s, conv_s, delta_s, c_s, n_s, m_s)
```

```python
import functools

import jax
import jax.numpy as jnp
from jax import lax
from jax.experimental import pallas as pl
from jax.experimental.pallas import tpu as pltpu

F32 = jnp.float32
BF16 = jnp.bfloat16
HI = lax.Precision.HIGHEST
EPS = 1e-6
NEG_INF = float("-inf")

H_A, DK_A = 8, 128
H_B, HD_B = 8, 128
H_C, DK_C = 4, 256
CONV_W = 4
MOBA_BLOCK = 256
MOBA_TOPK = 3
PAGE_SIZE = 128
CHUNK = 64
W_BR = 1024
D_MODEL = 2048

LANES = 128
SUBLANES = 8
VMEM_LIMIT = 48 * 1024 * 1024

COL_QA, COL_KA, COL_VA, COL_ZA = 0, 1, 2, 3
COL_QB, COL_KB, COL_VB, COL_ZB = 4, 5, 6, 7
COL_QC, COL_KC, COL_VC, COL_ZC, COL_OC = 8, 9, 10, 11, 12
COL_GATE = 13
N_BIG = 19 * W_BR
N_SMALL = 256
COL_SMALL = N_BIG // LANES
N_PROJ = N_BIG + N_SMALL
LANE_BETA, LANE_ALPHA, LANE_I, LANE_F = 0, 8, 16, 20


def _params(*sem):
    return pltpu.CompilerParams(dimension_semantics=sem, vmem_limit_bytes=VMEM_LIMIT)


def _sigmoid(x):
    return 1.0 / (1.0 + jnp.exp(-x))


def _silu(x):
    return x * _sigmoid(x)


def _softplus(x):
    return jnp.maximum(x, 0.0) + jnp.log1p(jnp.exp(-jnp.abs(x)))


def _dot(a, b, precision=None):
    return jnp.dot(a, b, precision=precision, preferred_element_type=F32)


def _dot_nt(a, b, precision=None):
    return lax.dot_general(a, b, (((1,), (1,)), ((), ())), precision=precision,
                           preferred_element_type=F32)


def _dot_tn(a, b, precision=None):
    return lax.dot_general(a, b, (((0,), (0,)), ((), ())), precision=precision,
                           preferred_element_type=F32)


def _pick_tile(n, candidates):
    for c in candidates:
        if n % c == 0:
            return c
    return n


def _rms_kernel(x_ref, g_ref, o_ref):
    x = x_ref[...]
    y = x * lax.rsqrt(jnp.mean(x * x, axis=-1, keepdims=True) + EPS)
    o_ref[...] = (y * g_ref[...]).astype(o_ref.dtype)


def rms_cast(x, g):
    t, d = x.shape
    tm = _pick_tile(t, (512, 256, 128))
    return pl.pallas_call(
        _rms_kernel, grid=(t // tm,),
        in_specs=[pl.BlockSpec((tm, d), lambda i: (i, 0)), pl.BlockSpec((1, d), lambda i: (0, 0))],
        out_specs=pl.BlockSpec((tm, d), lambda i: (i, 0)),
        out_shape=jax.ShapeDtypeStruct((t, d), BF16),
        compiler_params=_params("arbitrary"), name="rms_cast")(x, g.reshape(1, d))


def _mm_kernel(a_ref, w_ref, o_ref):
    o_ref[...] = _dot(a_ref[...], w_ref[...])


def in_proj(h, w):
    t, k = h.shape
    n = w.shape[1]
    tm = _pick_tile(t, (1024, 768, 512, 256, 128))
    tn = _pick_tile(n, (1792, 1024, 512, 256))
    return pl.pallas_call(
        _mm_kernel, grid=(n // tn, t // tm),
        in_specs=[pl.BlockSpec((tm, k), lambda j, i: (i, 0)), pl.BlockSpec((k, tn), lambda j, i: (0, j))],
        out_specs=pl.BlockSpec((tm, tn), lambda j, i: (i, j)),
        out_shape=jax.ShapeDtypeStruct((t, n), F32),
        compiler_params=_params("arbitrary", "arbitrary"), name="in_proj")(h, w)


def _merge_kernel(oa_ref, ob_ref, oc_ref, zb_ref, ga_ref, gb_ref, gc_ref, wa_ref, wb_ref, wc_ref, o_ref):
    ob = ob_ref[...] * _silu(zb_ref[...])
    m = _sigmoid(ga_ref[...]) * _dot(oa_ref[...].astype(BF16), wa_ref[...])
    m += _sigmoid(gb_ref[...]) * _dot(ob.astype(BF16), wb_ref[...])
    m += _sigmoid(gc_ref[...]) * _dot(oc_ref[...].astype(BF16), wc_ref[...])
    o_ref[...] = m.astype(o_ref.dtype)


def merge(oa, ob, oc, p, wa, wb, wc):
    t = oa.shape[0]
    tm = _pick_tile(t, (512, 256, 128))
    tn = W_BR
    row = lambda j, i: (i, 0)
    return pl.pallas_call(
        _merge_kernel, grid=(D_MODEL // tn, t // tm),
        in_specs=[pl.BlockSpec((tm, W_BR), row), pl.BlockSpec((tm, W_BR), row), pl.BlockSpec((tm, W_BR), row),
                  pl.BlockSpec((tm, W_BR), lambda j, i: (i, COL_ZB)),
                  pl.BlockSpec((tm, tn), lambda j, i: (i, COL_GATE + j)),
                  pl.BlockSpec((tm, tn), lambda j, i: (i, COL_GATE + 2 + j)),
                  pl.BlockSpec((tm, tn), lambda j, i: (i, COL_GATE + 4 + j)),
                  pl.BlockSpec((W_BR, tn), lambda j, i: (0, j)),
                  pl.BlockSpec((W_BR, tn), lambda j, i: (0, j)),
                  pl.BlockSpec((W_BR, tn), lambda j, i: (0, j))],
        out_specs=pl.BlockSpec((tm, tn), lambda j, i: (i, j)),
        out_shape=jax.ShapeDtypeStruct((t, D_MODEL), BF16),
        compiler_params=_params("arbitrary", "arbitrary"), name="merge")(oa, ob, oc, p, p, p, p, wa, wb, wc)


def _outproj_kernel(x_ref, m_ref, w_ref, o_ref):
    o_ref[...] = x_ref[...] + _dot(m_ref[...], w_ref[...])


def out_proj(x, m, w):
    t, d = x.shape
    tm = _pick_tile(t, (512, 256, 128))
    tn = 1024
    return pl.pallas_call(
        _outproj_kernel, grid=(d // tn, t // tm),
        in_specs=[pl.BlockSpec((tm, tn), lambda j, i: (i, j)), pl.BlockSpec((tm, d), lambda j, i: (i, 0)),
                  pl.BlockSpec((d, tn), lambda j, i: (0, j))],
        out_specs=pl.BlockSpec((tm, tn), lambda j, i: (i, j)),
        out_shape=jax.ShapeDtypeStruct((t, d), F32),
        compiler_params=_params("arbitrary", "arbitrary"), name="out_proj")(x, m, w)


def _unit_lower_inverse(a, eye):
    inv = eye - a
    pw = a
    n = 2
    while n < a.shape[0]:
        pw = _dot(pw, pw, HI)
        inv = inv + _dot(inv, pw, HI)
        n *= 2
    return inv


def _delta_kernel(q_ref, k_ref, v_ref, z_ref, sm_ref, cw_ref, cb_ref, al_ref, dtb_ref, s0_ref, ng_ref,
                  o_ref, cn_ref, sn_ref, xp_ref, s_ref, *, c_in):
    c = pl.program_id(1)
    C = CHUNK
    hd = DK_A

    @pl.when(c == 0)
    def _init():
        xp_ref[...] = jnp.zeros(xp_ref.shape, F32)
        xp_ref[0:SUBLANES, :] = cb_ref[0]
        s_ref[...] = s0_ref[0]

    xp_ref[SUBLANES:SUBLANES + c_in, 0:W_BR] = q_ref[...]
    xp_ref[SUBLANES:SUBLANES + c_in, W_BR:2 * W_BR] = k_ref[...]
    xp_ref[SUBLANES:SUBLANES + c_in, 2 * W_BR:3 * W_BR] = v_ref[...]

    valid = lax.broadcasted_iota(jnp.int32, (C, 1), 0) < c_in
    sm = sm_ref[...]
    if c_in < C:
        sm = jnp.concatenate([sm, jnp.zeros((C - c_in, sm.shape[1]), F32)], axis=0)
    beta = _sigmoid(sm)
    g = -jnp.exp(al_ref[...]) * _softplus(sm + dtb_ref[...])
    g = jnp.where(valid, g, 0.0)

    ri = lax.broadcasted_iota(jnp.int32, (C, C), 0)
    ci = lax.broadcasted_iota(jnp.int32, (C, C), 1)
    causal = ci <= ri
    strict = ci < ri
    eye = (ri == ci).astype(F32)
    gcum = _dot(causal.astype(F32), g, HI)
    gcum_t = gcum.T
    e_g = jnp.exp(gcum)
    g_last = gcum[C - 1:C, :]
    e_end = jnp.exp(g_last - gcum)
    e_last = jnp.exp(g_last)

    def conv(col):
        sl = slice(col, col + hd)
        y = xp_ref[5:5 + C, sl] * cw_ref[0:1, sl]
        for j in range(1, CONV_W):
            y = y + xp_ref[5 + j:5 + j + C, sl] * cw_ref[j:j + 1, sl]
        return jnp.where(valid, _silu(y), 0.0)

    for h in range(H_A):
        a = LANE_ALPHA + h
        qh = conv(h * hd)
        kh = conv(W_BR + h * hd)
        vh = conv(2 * W_BR + h * hd)
        qh = qh * lax.rsqrt(jnp.sum(qh * qh, axis=-1, keepdims=True) + EPS) * (DK_A ** -0.5)
        kh = kh * lax.rsqrt(jnp.sum(kh * kh, axis=-1, keepdims=True) + EPS)
        g_col = gcum[:, a:a + 1]
        g_row = gcum_t[a:a + 1, :]
        decay = jnp.exp(jnp.where(causal, g_col - g_row, NEG_INF))
        b_col = beta[:, LANE_BETA + h:LANE_BETA + h + 1]
        a_mat = jnp.where(strict, b_col * decay * _dot_nt(kh, kh, HI), 0.0)
        t_inv = _unit_lower_inverse(a_mat, eye)
        u = _dot(t_inv, b_col * vh, HI)
        kt = _dot(t_inv, (b_col * e_g[:, a:a + 1]) * kh, HI)
        p = decay * _dot_nt(qh, kh, HI)
        qg = qh * e_g[:, a:a + 1]
        kend = kh * e_end[:, a:a + 1]
        s = s_ref[h]
        w = u - _dot(kt, s, HI)
        o = _dot(qg, s, HI) + _dot(p, w, HI)
        s_ref[h] = e_last[:, a:a + 1] * s + _dot_tn(kend, w, HI)
        o = o * lax.rsqrt(jnp.mean(o * o, axis=-1, keepdims=True) + EPS) * ng_ref[...]
        o_ref[:, h * hd:(h + 1) * hd] = o[0:c_in] * _silu(z_ref[:, h * hd:(h + 1) * hd])

    tail = xp_ref[c_in:c_in + SUBLANES, :]
    xp_ref[0:SUBLANES, :] = tail

    @pl.when(c == pl.num_programs(1) - 1)
    def _fin():
        cn_ref[0] = tail
        sn_ref[0] = s_ref[...]


def delta_branch(p, conv_w, conv_buf, a_log, dt_bias, s0, norm_g, bsz, seq):
    c_in = min(seq, CHUNK)
    n_chunks = seq // c_in
    cb = jnp.pad(conv_buf, ((0, 0), (SUBLANES - (CONV_W - 1), 0), (0, 0)))
    lane_vec = lambda v: jnp.zeros((1, LANES), F32).at[0, LANE_ALPHA:LANE_ALPHA + H_A].set(v)
    rb = lambda col: pl.BlockSpec((c_in, W_BR), lambda b, c, col=col: (b * n_chunks + c, col))
    const2 = lambda b, c: (0, 0)
    o, cn, sn = pl.pallas_call(
        functools.partial(_delta_kernel, c_in=c_in), grid=(bsz, n_chunks),
        in_specs=[rb(COL_QA), rb(COL_KA), rb(COL_VA), rb(COL_ZA),
                  pl.BlockSpec((c_in, LANES), lambda b, c: (b * n_chunks + c, COL_SMALL)),
                  pl.BlockSpec((CONV_W, 3 * W_BR), const2),
                  pl.BlockSpec((1, SUBLANES, 3 * W_BR), lambda b, c: (b, 0, 0)),
                  pl.BlockSpec((1, LANES), const2), pl.BlockSpec((1, LANES), const2),
                  pl.BlockSpec((1, H_A, DK_A, DK_A), lambda b, c: (b, 0, 0, 0)),
                  pl.BlockSpec((1, DK_A), const2)],
        out_specs=[pl.BlockSpec((c_in, W_BR), lambda b, c: (b * n_chunks + c, 0)),
                   pl.BlockSpec((1, SUBLANES, 3 * W_BR), lambda b, c: (b, 0, 0)),
                   pl.BlockSpec((1, H_A, DK_A, DK_A), lambda b, c: (b, 0, 0, 0))],
        out_shape=[jax.ShapeDtypeStruct((bsz * seq, W_BR), F32),
                   jax.ShapeDtypeStruct((bsz, SUBLANES, 3 * W_BR), F32),
                   jax.ShapeDtypeStruct((bsz, H_A, DK_A, DK_A), F32)],
        scratch_shapes=[pltpu.VMEM((SUBLANES + CHUNK, 3 * W_BR), F32), pltpu.VMEM((H_A, DK_A, DK_A), F32)],
        compiler_params=_params("arbitrary", "arbitrary"), name="delta")(
            p, p, p, p, p, conv_w, cb, lane_vec(a_log), lane_vec(dt_bias), s0, norm_g.reshape(1, DK_A))
    return o, cn[:, SUBLANES - (CONV_W - 1):], sn


def _mlstm_kernel(q_ref, k_ref, v_ref, z_ref, og_ref, sm_ref, bif_ref, c0_ref, n0_ref, m0_ref, ng_ref,
                  o_ref, cn_ref, nn_ref, mn_ref, c_s, n_s, m_s, *, c_in):
    c = pl.program_id(1)
    C = CHUNK
    hd = DK_C

    @pl.when(c == 0)
    def _init():
        c_s[...] = c0_ref[0]
        n_s[...] = n0_ref[0]
        m_s[...] = m0_ref[0]

    def pad_rows(x):
        if c_in == C:
            return x
        return jnp.concatenate([x, jnp.zeros((C - c_in, x.shape[1]), F32)], axis=0)

    valid = lax.broadcasted_iota(jnp.int32, (C, 1), 0) < c_in
    pre = pad_rows(sm_ref[...]) + bif_ref[...]
    i_pre = jnp.where(valid, pre, NEG_INF)
    log_f = jnp.where(valid, -_softplus(-pre), 0.0)

    ri = lax.broadcasted_iota(jnp.int32, (C, C), 0)
    ci = lax.broadcasted_iota(jnp.int32, (C, C), 1)
    causal = ci <= ri
    bcum = _dot(causal.astype(F32), log_f, HI)
    bcum_t = bcum.T
    i_t = i_pre.T
    m_old = m_s[...]
    m_new_vec = m_old
    lane = lax.broadcasted_iota(jnp.int32, m_old.shape, 1)

    for h in range(H_C):
        sl = slice(h * hd, (h + 1) * hd)
        q = pad_rows(q_ref[:, sl])
        ks = pad_rows(k_ref[:, sl]) * (DK_C ** -0.5)
        v = pad_rows(v_ref[:, sl])
        b_col = bcum[:, LANE_F + h:LANE_F + h + 1]
        b_row = bcum_t[LANE_F + h:LANE_F + h + 1, :]
        i_col = i_pre[:, LANE_I + h:LANE_I + h + 1]
        i_row = i_t[LANE_I + h:LANE_I + h + 1, :]
        m_prev = m_old[:, h:h + 1]
        inter = b_col + m_prev
        intra = jnp.where(causal, b_col - b_row + i_row, NEG_INF)
        m_t = jnp.maximum(inter, jnp.max(intra, axis=-1, keepdims=True))
        w_inter = jnp.exp(inter - m_t)
        s = _dot_nt(q, ks, HI) * jnp.exp(intra - m_t)
        cm = c_s[h]
        n = n_s[h:h + 1, :]
        num = w_inter * _dot(q, cm, HI) + _dot(s, v, HI)
        den = w_inter * jnp.sum(q * n, axis=-1, keepdims=True) + jnp.sum(s, axis=-1, keepdims=True)
        hh = num / jnp.maximum(jnp.abs(den), jnp.exp(-m_t))
        m_new = m_t[C - 1:C, :]
        b_last = b_col[C - 1:C, :]
        w_c = jnp.exp(b_last + m_prev - m_new)
        w_j = jnp.exp(b_last - b_col + i_col - m_new)
        kw = w_j * ks
        c_s[h] = w_c * cm + _dot_tn(kw, v, HI)
        n_s[h:h + 1, :] = w_c * n + jnp.sum(kw, axis=0, keepdims=True)
        m_new_vec = jnp.where(lane == h, m_new, m_new_vec)
        hc = _sigmoid(og_ref[:, sl]) * hh[0:c_in]
        hc = hc * lax.rsqrt(jnp.mean(hc * hc, axis=-1, keepdims=True) + EPS) * ng_ref[...]
        o_ref[:, sl] = hc * _silu(z_ref[:, sl])

    m_s[...] = m_new_vec

    @pl.when(c == pl.num_programs(1) - 1)
    def _fin():
        cn_ref[0] = c_s[...]
        nn_ref[0] = n_s[...]
        mn_ref[0] = m_new_vec


def mlstm_branch(p, b_if, c0, n0, m0, norm_g, bsz, seq):
    c_in = min(seq, CHUNK)
    n_chunks = seq // c_in
    bif = jnp.zeros((1, LANES), F32).at[0, LANE_I:LANE_I + 2 * H_C].set(b_if)
    m0p = jnp.pad(m0, ((0, 0), (0, LANES - H_C))).reshape(bsz, 1, LANES)
    rb = lambda col: pl.BlockSpec((c_in, W_BR), lambda b, c, col=col: (b * n_chunks + c, col))
    const2 = lambda b, c: (0, 0)
    st4 = pl.BlockSpec((1, H_C, DK_C, DK_C), lambda b, c: (b, 0, 0, 0))
    st3 = pl.BlockSpec((1, H_C, DK_C), lambda b, c: (b, 0, 0))
    stm = pl.BlockSpec((1, 1, LANES), lambda b, c: (b, 0, 0))
    o, cn, nn, mn = pl.pallas_call(
        functools.partial(_mlstm_kernel, c_in=c_in), grid=(bsz, n_chunks),
        in_specs=[rb(COL_QC), rb(COL_KC), rb(COL_VC), rb(COL_ZC), rb(COL_OC),
                  pl.BlockSpec((c_in, LANES), lambda b, c: (b * n_chunks + c, COL_SMALL)),
                  pl.BlockSpec((1, LANES), const2), st4, st3, stm, pl.BlockSpec((1, DK_C), const2)],
        out_specs=[pl.BlockSpec((c_in, W_BR), lambda b, c: (b * n_chunks + c, 0)), st4, st3, stm],
        out_shape=[jax.ShapeDtypeStruct((bsz * seq, W_BR), F32),
                   jax.ShapeDtypeStruct((bsz, H_C, DK_C, DK_C), F32),
                   jax.ShapeDtypeStruct((bsz, H_C, DK_C), F32),
                   jax.ShapeDtypeStruct((bsz, 1, LANES), F32)],
        scratch_shapes=[pltpu.VMEM((H_C, DK_C, DK_C), F32), pltpu.VMEM((H_C, DK_C), F32),
                        pltpu.VMEM((1, LANES), F32)],
        compiler_params=_params("arbitrary", "arbitrary"), name="mlstm")(
            p, p, p, p, p, p, bif, c0, n0, m0p, norm_g.reshape(1, DK_C))
    return o, cn, nn, mn[:, 0, :H_C]


def _head_rms(x, g):
    return x * lax.rsqrt(jnp.mean(x * x, axis=-1, keepdims=True) + EPS) * g


def _qknorm_kernel(q_ref, k_ref, qg_ref, kg_ref, qn_ref, kn_ref):
    for h in range(H_B):
        sl = slice(h * HD_B, (h + 1) * HD_B)
        qn_ref[:, sl] = _head_rms(q_ref[:, sl], qg_ref[...]) * (HD_B ** -0.5)
        kn_ref[:, sl] = _head_rms(k_ref[:, sl], kg_ref[...])


def _qknorm_kv_kernel(q_ref, k_ref, v_ref, qg_ref, kg_ref, qn_ref, kn_ref, knb_ref, vt_ref, kbar_ref):
    for h in range(H_B):
        sl = slice(h * HD_B, (h + 1) * HD_B)
        qn_ref[:, sl] = _head_rms(q_ref[:, sl], qg_ref[...]) * (HD_B ** -0.5)
        kn = _head_rms(k_ref[:, sl], kg_ref[...])
        kn_ref[:, sl] = kn
        knb_ref[0, :, sl] = kn.astype(BF16)
        kbar_ref[0, :, sl] = jnp.mean(kn, axis=0, keepdims=True)
    vt_ref[0] = v_ref[...].T.astype(BF16)


def qk_norm(p, qg, kg, with_kv):
    t = p.shape[0]
    qg = qg.reshape(1, HD_B)
    kg = kg.reshape(1, HD_B)
    gs = pl.BlockSpec((1, HD_B), lambda i: (0, 0))
    if not with_kv:
        tm = _pick_tile(t, (256, 128))
        return pl.pallas_call(
            _qknorm_kernel, grid=(t // tm,),
            in_specs=[pl.BlockSpec((tm, W_BR), lambda i: (i, COL_QB)),
                      pl.BlockSpec((tm, W_BR), lambda i: (i, COL_KB)), gs, gs],
            out_specs=[pl.BlockSpec((tm, W_BR), lambda i: (i, 0))] * 2,
            out_shape=[jax.ShapeDtypeStruct((t, W_BR), F32)] * 2,
            compiler_params=_params("arbitrary"), name="qk_norm")(p, p, qg, kg)
    tm = MOBA_BLOCK
    nb = t // tm
    return pl.pallas_call(
        _qknorm_kv_kernel, grid=(nb,),
        in_specs=[pl.BlockSpec((tm, W_BR), lambda i: (i, COL_QB)),
                  pl.BlockSpec((tm, W_BR), lambda i: (i, COL_KB)),
                  pl.BlockSpec((tm, W_BR), lambda i: (i, COL_VB)), gs, gs],
        out_specs=[pl.BlockSpec((tm, W_BR), lambda i: (i, 0)), pl.BlockSpec((tm, W_BR), lambda i: (i, 0)),
                   pl.BlockSpec((1, tm, W_BR), lambda i: (i, 0, 0)),
                   pl.BlockSpec((1, W_BR, tm), lambda i: (i, 0, 0)),
                   pl.BlockSpec((1, 1, W_BR), lambda i: (i, 0, 0))],
        out_shape=[jax.ShapeDtypeStruct((t, W_BR), F32), jax.ShapeDtypeStruct((t, W_BR), F32),
                   jax.ShapeDtypeStruct((nb, tm, W_BR), BF16), jax.ShapeDtypeStruct((nb, W_BR, tm), BF16),
                   jax.ShapeDtypeStruct((nb, 1, W_BR), F32)],
        compiler_params=_params("arbitrary"), name="qk_norm_kv")(p, p, p, qg, kg)


def _select_topk(sc, n_valid_mask):
    nb = sc.shape[0]
    idx = lax.broadcasted_iota(jnp.int32, sc.shape, 0).astype(F32)
    sc = jnp.where(n_valid_mask, sc, NEG_INF)
    sel = jnp.zeros(sc.shape, F32)
    for _ in range(MOBA_TOPK):
        mx = jnp.max(sc, axis=0, keepdims=True)
        first = jnp.min(jnp.where(sc == mx, idx, float(nb)), axis=0, keepdims=True)
        hit = idx == first
        sel = jnp.where(hit & (mx > NEG_INF), 1.0, sel)
        sc = jnp.where(hit, NEG_INF, sc)
    return sel


def _moba_prompt_kernel(q_ref, k_ref, vt_ref, kbar_ref, o_ref, sel_ref):
    i = pl.program_id(2)
    blk = MOBA_BLOCK
    q = q_ref[...]
    qb = q.astype(BF16)
    nb = kbar_ref.shape[1]
    sc = _dot_nt(kbar_ref[0], q, HI)
    past = lax.broadcasted_iota(jnp.int32, (nb, blk), 0) < i
    sel_ref[...] = _select_topk(sc, past)

    kpos = lax.broadcasted_iota(jnp.int32, (blk, blk), 0)
    qpos = lax.broadcasted_iota(jnp.int32, (blk, blk), 1)
    s = jnp.where(kpos <= qpos, _dot_nt(k_ref[i], qb), NEG_INF)
    m = jnp.max(s, axis=0, keepdims=True)
    pr = jnp.exp(s - m)
    l = jnp.sum(pr, axis=0, keepdims=True)
    acc = _dot(vt_ref[i], pr.astype(BF16))

    def body(n, carry):
        m, l, acc = carry
        s = jnp.where(sel_ref[pl.ds(n, 1), :] > 0.0, _dot_nt(k_ref[n], qb), NEG_INF)
        m_new = jnp.maximum(m, jnp.max(s, axis=0, keepdims=True))
        alpha = jnp.exp(m - m_new)
        pr = jnp.exp(s - m_new)
        l = alpha * l + jnp.sum(pr, axis=0, keepdims=True)
        acc = alpha * acc + _dot(vt_ref[n], pr.astype(BF16))
        return m_new, l, acc

    m, l, acc = lax.fori_loop(0, i, body, (m, l, acc))
    o_ref[...] = (acc / l).T


def moba_prompt(qn, knb, vt, kbar, bsz, seq):
    nb = seq // MOBA_BLOCK
    blk = MOBA_BLOCK
    return pl.pallas_call(
        _moba_prompt_kernel, grid=(bsz, H_B, nb),
        in_specs=[pl.BlockSpec((blk, HD_B), lambda b, h, i: (b * nb + i, h)),
                  pl.BlockSpec((nb, blk, HD_B), lambda b, h, i: (b, 0, h)),
                  pl.BlockSpec((nb, HD_B, blk), lambda b, h, i: (b, h, 0)),
                  pl.BlockSpec((1, nb, HD_B), lambda b, h, i: (b, 0, h))],
        out_specs=pl.BlockSpec((blk, HD_B), lambda b, h, i: (b * nb + i, h)),
        out_shape=jax.ShapeDtypeStruct((bsz * seq, W_BR), F32),
        scratch_shapes=[pltpu.VMEM((nb, blk), F32)],
        compiler_params=_params("arbitrary", "arbitrary", "arbitrary"), name="moba_prompt")(qn, knb, vt, kbar)


def _moba_decode_kernel(pt_ref, qt_ref, qtf_ref, bias_ref, obias_ref, kown_ref, vown_ref, *rest, ppb, seq):
    del pt_ref
    k_refs = rest[:ppb]
    v_refs = rest[ppb:2 * ppb]
    o_ref = rest[2 * ppb]
    m_s, l_s, o_s, ks_s = rest[2 * ppb + 1:]
    n = pl.program_id(1)
    nblk = pl.num_programs(1)
    qt = qt_ref[0]
    floor = -1e30

    def partial_softmax(s):
        m = jnp.maximum(jnp.max(s, axis=0, keepdims=True), floor)
        pr = jnp.exp(s - m)
        return m, jnp.sum(pr, axis=0, keepdims=True), pr

    @pl.when(n == 0)
    def _init():
        m_s[...] = jnp.zeros(m_s.shape, F32)
        l_s[...] = jnp.zeros(l_s.shape, F32)

    for j in range(ppb):
        x = k_refs[j][0, 0]
        m, l, pr = partial_softmax(_dot(x.astype(BF16), qt) + bias_ref[...])
        pg = n * ppb + j
        m_s[pl.ds(pg, 1), :] = m
        l_s[pl.ds(pg, 1), :] = l
        o_s[pg] = _dot_tn(pr.astype(BF16), v_refs[j][0, 0].astype(BF16))
        ks_s[pl.ds(pg * H_B, H_B), :] = jnp.sum(x.reshape(PAGE_SIZE, H_B, HD_B), axis=0)

    @pl.when(n == nblk - 1)
    def _combine():
        npg = ks_s.shape[0] // H_B
        nb = npg // ppb
        nslot = m_s.shape[0]
        m_o, l_o, pr_o = partial_softmax(_dot(kown_ref[0].astype(BF16), qt) + obias_ref[...])
        m_s[npg:npg + 1, :] = m_o
        l_s[npg:npg + 1, :] = l_o
        o_s[npg] = _dot_tn(pr_o.astype(BF16), vown_ref[0].astype(BF16))
        scp = _dot(ks_s[...], qtf_ref[0], HI)
        rh = lax.broadcasted_iota(jnp.int32, scp.shape, 0) % H_B
        chd = lax.broadcasted_iota(jnp.int32, scp.shape, 1) // seq
        scp = jnp.where(rh == chd, scp, 0.0)
        rows_per_blk = ppb * H_B
        gather = (lax.broadcasted_iota(jnp.int32, (nb, npg * H_B), 1) // rows_per_blk
                  == lax.broadcasted_iota(jnp.int32, (nb, npg * H_B), 0)).astype(F32)
        scb = _dot(gather, scp, HI) * (1.0 / MOBA_BLOCK)
        sel = _select_topk(scb, jnp.full(scb.shape, True))
        slot = lax.broadcasted_iota(jnp.int32, (nslot, nb), 0)
        expand = (slot // ppb == lax.broadcasted_iota(jnp.int32, (nslot, nb), 1)) & (slot < npg)
        sel_slot = _dot(expand.astype(F32), sel, HI)
        sel_slot = jnp.where(lax.broadcasted_iota(jnp.int32, sel_slot.shape, 0) == npg, 1.0, sel_slot)
        m_pg = jnp.where(sel_slot > 0.0, m_s[...], floor)
        m_all = jnp.max(m_pg, axis=0, keepdims=True)
        w_pg = sel_slot * jnp.exp(m_pg - m_all)
        l_all = jnp.sum(w_pg * l_s[...], axis=0, keepdims=True)
        w_t = (w_pg / jnp.where(l_all == 0.0, 1.0, l_all)).T
        acc = w_t[:, 0:1] * o_s[0]
        for pg in range(1, npg + 1):
            acc = acc + w_t[:, pg:pg + 1] * o_s[pg]
        o_ref[0] = acc


def moba_decode(qn, kn, v, cache_k, cache_v, page_table, layer, bsz, seq):
    ppb = MOBA_BLOCK // PAGE_SIZE
    n_pages = page_table.shape[1]
    nblk = n_pages // ppb
    assert H_B * seq <= LANES and nblk >= MOBA_TOPK and n_pages % ppb == 0 and n_pages < LANES
    ncol = H_B * seq
    rows = PAGE_SIZE * H_B
    depth, n_phys = cache_k.shape[:2]
    ck = cache_k.reshape(depth, n_phys, rows, HD_B)
    cv = cache_v.reshape(depth, n_phys, rows, HD_B)
    qt = jnp.transpose(qn.reshape(bsz, seq, H_B, HD_B), (0, 3, 2, 1)).reshape(bsz, HD_B, ncol)
    qt = jnp.pad(qt, ((0, 0), (0, 0), (0, LANES - ncol)))
    col = jnp.arange(LANES)
    col_h, col_q = col // seq, col % seq
    row_h = jnp.arange(rows) % H_B
    bias = jnp.where(row_h[:, None] == col_h[None, :], 0.0, NEG_INF).astype(F32)
    orow = jnp.arange(seq * H_B)
    obias = jnp.where((orow[:, None] % H_B == col_h[None, :]) & (orow[:, None] // H_B <= col_q[None, :]),
                      0.0, NEG_INF).astype(F32)
    kown = kn.reshape(bsz, seq * H_B, HD_B)
    vown = v.reshape(bsz, seq * H_B, HD_B)

    page = lambda j: pl.BlockSpec((1, 1, rows, HD_B), lambda b, n, pt, j=j: (layer, pt[b, n * ppb + j], 0, 0))
    seq3 = lambda shape: pl.BlockSpec((1,) + shape, lambda b, n, pt: (b, 0, 0))
    const2 = lambda shape: pl.BlockSpec(shape, lambda b, n, pt: (0, 0))
    grid_spec = pltpu.PrefetchScalarGridSpec(
        num_scalar_prefetch=1, grid=(bsz, nblk),
        in_specs=[seq3((HD_B, LANES)), seq3((HD_B, LANES)), const2((rows, LANES)), const2((seq * H_B, LANES)),
                  seq3((seq * H_B, HD_B)), seq3((seq * H_B, HD_B))]
                 + [page(j) for j in range(ppb)] + [page(j) for j in range(ppb)],
        out_specs=seq3((LANES, HD_B)),
        scratch_shapes=[pltpu.VMEM((LANES, LANES), F32), pltpu.VMEM((LANES, LANES), F32),
                        pltpu.VMEM((n_pages + 1, LANES, HD_B), F32), pltpu.VMEM((n_pages * H_B, HD_B), F32)])
    out = pl.pallas_call(
        functools.partial(_moba_decode_kernel, ppb=ppb, seq=seq), grid_spec=grid_spec,
        out_shape=jax.ShapeDtypeStruct((bsz, LANES, HD_B), F32),
        compiler_params=_params("arbitrary", "arbitrary"), name="moba_decode")(
            page_table, qt.astype(BF16), qt, bias, obias, kown, vown, *([ck] * ppb), *([cv] * ppb))
    out = out[:, :ncol].reshape(bsz, H_B, seq, HD_B)
    return jnp.transpose(out, (0, 2, 1, 3)).reshape(bsz * seq, W_BR)


def _reorder_w_in(w_in):
    qkv_a = 3 * W_BR
    o_small_a = qkv_a + W_BR
    o_b = o_small_a + 2 * H_A
    o_c = o_b + 4 * W_BR
    o_if = o_c + 5 * W_BR
    o_g = o_if + 2 * H_C
    d = w_in.shape[0]
    pad = jnp.zeros((d, N_SMALL - 2 * H_A - 2 * H_C), w_in.dtype)
    return jnp.concatenate([w_in[:, :o_small_a], w_in[:, o_b:o_if], w_in[:, o_g:],
                            w_in[:, o_small_a:o_b], w_in[:, o_if:o_g], pad], axis=1).astype(BF16)


def _mixer(x, lw, conv_buf, s_a, c_c, n_c, m_c, bsz, seq, attend):
    (ln_g, w_proj, conv_a, a_log, dt_bias, norm_a, qnorm_b, knorm_b, b_if, norm_c, wa, wb, wc, wo) = lw
    h = rms_cast(x, ln_g)
    p = in_proj(h, w_proj)
    oa, conv_new, s_new = delta_branch(p, conv_a, conv_buf, a_log, dt_bias, s_a, norm_a, bsz, seq)
    ob, kn, vb = attend(p, qnorm_b, knorm_b)
    oc, c_new, n_new, m_new = mlstm_branch(p, b_if, c_c, n_c, m_c, norm_c, bsz, seq)
    y = out_proj(x, merge(oa, ob, oc, p, wa, wb, wc), wo)
    kv_shape = (bsz, seq, H_B, HD_B)
    return y, (kn.reshape(kv_shape), vb.reshape(kv_shape), conv_new, s_new, c_new, n_new, m_new)


def kernel(x_prompt, x_sample, cache_k, cache_v, page_table, state_conv_a, state_delta_a, state_mlstm_c,
           state_mlstm_n, state_mlstm_m, ln_g, w_in, conv_a, a_log, dt_bias, norm_a, qnorm_b, knorm_b,
           b_if, norm_c, w_br_a, w_br_b, w_br_c, w_out):
    bp, sp, d = x_prompt.shape
    bs, ss, _ = x_sample.shape
    depth = w_in.shape[0]
    yp = x_prompt.reshape(bp * sp, d)
    ys = x_sample.reshape(bs * ss, d)
    new_p = [[] for _ in range(7)]
    new_s = [[] for _ in range(7)]
    for l in range(depth):
        lw = (ln_g[l], _reorder_w_in(w_in[l]), conv_a[l], a_log[l], dt_bias[l], norm_a[l], qnorm_b[l],
              knorm_b[l], b_if[l], norm_c[l], w_br_a[l].astype(BF16), w_br_b[l].astype(BF16),
              w_br_c[l].astype(BF16), w_out[l].astype(BF16))

        def attend_prompt(p, qg, kg):
            qn, kn, knb, vt, kbar = qk_norm(p, qg, kg, with_kv=True)
            ob = moba_prompt(qn, knb, vt, kbar.reshape(bp, sp // MOBA_BLOCK, W_BR), bp, sp)
            return ob, kn, p[:, COL_VB * W_BR:(COL_VB + 1) * W_BR]

        def attend_sample(p, qg, kg, l=l):
            qn, kn = qk_norm(p, qg, kg, with_kv=False)
            vb = p[:, COL_VB * W_BR:(COL_VB + 1) * W_BR]
            return moba_decode(qn, kn, vb, cache_k, cache_v, page_table, l, bs, ss), kn, vb

        zeros = lambda *s: jnp.zeros(s, F32)
        yp, st_p = _mixer(yp, lw, zeros(bp, CONV_W - 1, 3 * W_BR), zeros(bp, H_A, DK_A, DK_A),
                          zeros(bp, H_C, DK_C, DK_C), zeros(bp, H_C, DK_C), zeros(bp, H_C), bp, sp,
                          attend_prompt)
        ys, st_s = _mixer(ys, lw, state_conv_a[l], state_delta_a[l], state_mlstm_c[l], state_mlstm_n[l],
                          state_mlstm_m[l], bs, ss, attend_sample)
        for i in range(7):
            new_p[i].append(st_p[i])
            new_s[i].append(st_s[i])
    outs_p = [jnp.stack(t) for t in new_p]
    outs_s = [jnp.stack(t) for t in new_s]
    return (yp.reshape(bp, sp, d), ys.reshape(bs, ss, d), *outs_p, *outs_s)
```

```python
import functools

import jax
import jax.numpy as jnp
from jax import lax
from jax.experimental import pallas as pl
from jax.experimental.pallas import tpu as pltpu

F32 = jnp.float32
BF16 = jnp.bfloat16
HI = lax.Precision.HIGHEST
EPS = 1e-6
NEG_INF = float("-inf")

H_A, DK_A = 8, 128
H_B, HD_B = 8, 128
H_C, DK_C = 4, 256
CONV_W = 4
MOBA_BLOCK = 256
MOBA_TOPK = 3
PAGE_SIZE = 128
CHUNK = 64
W_BR = 1024
D_MODEL = 2048

LANES = 128
SUBLANES = 8
VMEM_LIMIT = 48 * 1024 * 1024

COL_QA, COL_KA, COL_VA, COL_ZA = 0, 1, 2, 3
COL_QB, COL_KB, COL_VB, COL_ZB = 4, 5, 6, 7
COL_QC, COL_KC, COL_VC, COL_ZC, COL_OC = 8, 9, 10, 11, 12
COL_GATE = 13
N_BIG = 19 * W_BR
N_SMALL = 256
COL_SMALL = N_BIG // LANES
N_PROJ = N_BIG + N_SMALL
LANE_BETA, LANE_ALPHA, LANE_I, LANE_F = 0, 8, 16, 20


def _params(*sem):
    return pltpu.CompilerParams(dimension_semantics=sem, vmem_limit_bytes=VMEM_LIMIT)


def _sigmoid(x):
    return 1.0 / (1.0 + jnp.exp(-x))


def _silu(x):
    return x * _sigmoid(x)


def _softplus(x):
    return jnp.maximum(x, 0.0) + jnp.log1p(jnp.exp(-jnp.abs(x)))


def _dot(a, b, precision=None):
    return jnp.dot(a, b, precision=precision, preferred_element_type=F32)


def _dot_nt(a, b, precision=None):
    return lax.dot_general(a, b, (((1,), (1,)), ((), ())), precision=precision,
                           preferred_element_type=F32)


def _dot_tn(a, b, precision=None):
    return lax.dot_general(a, b, (((0,), (0,)), ((), ())), precision=precision,
                           preferred_element_type=F32)


def _pick_tile(n, candidates):
    for c in candidates:
        if n % c == 0:
            return c
    return n


def _rms_kernel(x_ref, g_ref, o_ref):
    x = x_ref[...]
    y = x * lax.rsqrt(jnp.mean(x * x, axis=-1, keepdims=True) + EPS)
    o_ref[...] = (y * g_ref[...]).astype(o_ref.dtype)


def rms_cast(x, g):
    t, d = x.shape
    tm = _pick_tile(t, (512, 256, 128))
    return pl.pallas_call(
        _rms_kernel, grid=(t // tm,),
        in_specs=[pl.BlockSpec((tm, d), lambda i: (i, 0)), pl.BlockSpec((1, d), lambda i: (0, 0))],
        out_specs=pl.BlockSpec((tm, d), lambda i: (i, 0)),
        out_shape=jax.ShapeDtypeStruct((t, d), BF16),
        compiler_params=_params("arbitrary"), name="rms_cast")(x, g.reshape(1, d))


def _mm_kernel(a_ref, w_ref, o_ref):
    o_ref[...] = _dot(a_ref[...], w_ref[...])


def in_proj(h, w):
    t, k = h.shape
    n = w.shape[1]
    tm = _pick_tile(t, (1024, 768, 512, 256, 128))
    tn = _pick_tile(n, (1792, 1024, 512, 256))
    return pl.pallas_call(
        _mm_kernel, grid=(n // tn, t // tm),
        in_specs=[pl.BlockSpec((tm, k), lambda j, i: (i, 0)), pl.BlockSpec((k, tn), lambda j, i: (0, j))],
        out_specs=pl.BlockSpec((tm, tn), lambda j, i: (i, j)),
        out_shape=jax.ShapeDtypeStruct((t, n), F32),
        compiler_params=_params("arbitrary", "arbitrary"), name="in_proj")(h, w)


def _merge_kernel(oa_ref, ob_ref, oc_ref, zb_ref, ga_ref, gb_ref, gc_ref, wa_ref, wb_ref, wc_ref, o_ref):
    ob = ob_ref[...] * _silu(zb_ref[...])
    m = _sigmoid(ga_ref[...]) * _dot(oa_ref[...].astype(BF16), wa_ref[...])
    m += _sigmoid(gb_ref[...]) * _dot(ob.astype(BF16), wb_ref[...])
    m += _sigmoid(gc_ref[...]) * _dot(oc_ref[...].astype(BF16), wc_ref[...])
    o_ref[...] = m.astype(o_ref.dtype)


def merge(oa, ob, oc, p, wa, wb, wc):
    t = oa.shape[0]
    tm = _pick_tile(t, (512, 256, 128))
    tn = W_BR
    row = lambda j, i: (i, 0)
    return pl.pallas_call(
        _merge_kernel, grid=(D_MODEL // tn, t // tm),
        in_specs=[pl.BlockSpec((tm, W_BR), row), pl.BlockSpec((tm, W_BR), row), pl.BlockSpec((tm, W_BR), row),
                  pl.BlockSpec((tm, W_BR), lambda j, i: (i, COL_ZB)),
                  pl.BlockSpec((tm, tn), lambda j, i: (i, COL_GATE + j)),
                  pl.BlockSpec((tm, tn), lambda j, i: (i, COL_GATE + 2 + j)),
                  pl.BlockSpec((tm, tn), lambda j, i: (i, COL_GATE + 4 + j)),
                  pl.BlockSpec((W_BR, tn), lambda j, i: (0, j)),
                  pl.BlockSpec((W_BR, tn), lambda j, i: (0, j)),
                  pl.BlockSpec((W_BR, tn), lambda j, i: (0, j))],
        out_specs=pl.BlockSpec((tm, tn), lambda j, i: (i, j)),
        out_shape=jax.ShapeDtypeStruct((t, D_MODEL), BF16),
        compiler_params=_params("arbitrary", "arbitrary"), name="merge")(oa, ob, oc, p, p, p, p, wa, wb, wc)


def _outproj_kernel(x_ref, m_ref, w_ref, o_ref):
    o_ref[...] = x_ref[...] + _dot(m_ref[...], w_ref[...])


def out_proj(x, m, w):
    t, d = x.shape
    tm = _pick_tile(t, (512, 256, 128))
    tn = 1024
    return pl.pallas_call(
        _outproj_kernel, grid=(d // tn, t // tm),
        in_specs=[pl.BlockSpec((tm, tn), lambda j, i: (i, j)), pl.BlockSpec((tm, d), lambda j, i: (i, 0)),
                  pl.BlockSpec((d, tn), lambda j, i: (0, j))],
        out_specs=pl.BlockSpec((tm, tn), lambda j, i: (i, j)),
        out_shape=jax.ShapeDtypeStruct((t, d), F32),
        compiler_params=_params("arbitrary", "arbitrary"), name="out_proj")(x, m, w)


def _bdot(a, b):
    return _dot(a.astype(BF16), b.astype(BF16))


def _delta_kernel(q_ref, k_ref, v_ref, z_ref, sm_ref, cw_ref, cb_ref, al_ref, dtb_ref, s0_ref, ng_ref,
                  o_ref, cn_ref, sn_ref, xp_ref, s_ref, *, c_in):
    c = pl.program_id(1)
    C = CHUNK
    hd = DK_A

    @pl.when(c == 0)
    def _init():
        xp_ref[...] = jnp.zeros(xp_ref.shape, F32)
        xp_ref[0:SUBLANES, :] = cb_ref[0]
        s_ref[...] = s0_ref[0]

    xp_ref[SUBLANES:SUBLANES + c_in, 0:W_BR] = q_ref[...]
    xp_ref[SUBLANES:SUBLANES + c_in, W_BR:2 * W_BR] = k_ref[...]
    xp_ref[SUBLANES:SUBLANES + c_in, 2 * W_BR:3 * W_BR] = v_ref[...]

    valid = lax.broadcasted_iota(jnp.int32, (C, 1), 0) < c_in
    sm = sm_ref[...]
    if c_in < C:
        sm = jnp.concatenate([sm, jnp.zeros((C - c_in, sm.shape[1]), F32)], axis=0)
    beta = _sigmoid(sm)
    g = -jnp.exp(al_ref[...]) * _softplus(sm + dtb_ref[...])
    g = jnp.where(valid, g, 0.0)

    ri = lax.broadcasted_iota(jnp.int32, (C, C), 0)
    ci = lax.broadcasted_iota(jnp.int32, (C, C), 1)
    causal = ci <= ri
    strict = ci < ri
    gcum = _dot(causal.astype(F32), g, HI)
    gcum_t = gcum.T
    e_g = jnp.exp(gcum)
    g_last = gcum[C - 1:C, :]
    e_end = jnp.exp(g_last - gcum)
    e_last = jnp.exp(g_last)

    def conv(col):
        sl = slice(col, col + hd)
        y = xp_ref[5:5 + C, sl] * cw_ref[0:1, sl]
        for j in range(1, CONV_W):
            y = y + xp_ref[5 + j:5 + j + C, sl] * cw_ref[j:j + 1, sl]
        return jnp.where(valid, _silu(y), 0.0)

    heads = range(H_A)
    lane_a = [LANE_ALPHA + h for h in heads]
    q_l = [conv(h * hd) for h in heads]
    k_l = [conv(W_BR + h * hd) for h in heads]
    v_l = [conv(2 * W_BR + h * hd) for h in heads]
    q_l = [q * lax.rsqrt(jnp.sum(q * q, axis=-1, keepdims=True) + EPS) * (DK_A ** -0.5) for q in q_l]
    k_l = [k * lax.rsqrt(jnp.sum(k * k, axis=-1, keepdims=True) + EPS) for k in k_l]
    decay_l = [jnp.exp(jnp.where(causal, gcum[:, a:a + 1] - gcum_t[a:a + 1, :], NEG_INF)) for a in lane_a]
    b_l = [beta[:, LANE_BETA + h:LANE_BETA + h + 1] for h in heads]
    kb_l = [k.astype(BF16) for k in k_l]
    qk_kk = [_dot_nt(jnp.concatenate([q_l[h].astype(BF16), kb_l[h]], axis=0), kb_l[h]) for h in heads]
    a_l = [jnp.where(strict, b_l[h] * decay_l[h] * qk_kk[h][C:2 * C], 0.0) for h in heads]
    r_l = [-a for a in a_l]
    pw_l = a_l
    n = 2
    while n < C:
        pw_l = [_bdot(pw, pw) for pw in pw_l]
        r_l = [r_l[h] + pw_l[h] + _bdot(r_l[h], pw_l[h]) for h in heads]
        n *= 2
    rhs_l = [jnp.concatenate([b_l[h] * v_l[h], (b_l[h] * e_g[:, lane_a[h]:lane_a[h] + 1]) * k_l[h]], axis=1)
             for h in heads]
    sol_l = [rhs_l[h] + _bdot(r_l[h], rhs_l[h]) for h in heads]
    qg_l = [q_l[h] * e_g[:, lane_a[h]:lane_a[h] + 1] for h in heads]
    s_l = [s_ref[h] for h in heads]
    kq_s = [_bdot(jnp.concatenate([sol_l[h][:, hd:2 * hd], qg_l[h]], axis=0), s_l[h]) for h in heads]
    wb_l = [(sol_l[h][:, 0:hd] - kq_s[h][0:C]).astype(BF16) for h in heads]
    o_l = [kq_s[h][C:2 * C] + _dot((decay_l[h] * qk_kk[h][0:C]).astype(BF16), wb_l[h]) for h in heads]
    for h in heads:
        kend = (k_l[h] * e_end[:, lane_a[h]:lane_a[h] + 1]).astype(BF16)
        s_ref[h] = e_last[:, lane_a[h]:lane_a[h] + 1] * s_l[h] + _dot_tn(kend, wb_l[h])
    for h in heads:
        o = o_l[h]
        o = o * lax.rsqrt(jnp.mean(o * o, axis=-1, keepdims=True) + EPS) * ng_ref[...]
        o_ref[:, h * hd:(h + 1) * hd] = o[0:c_in] * _silu(z_ref[:, h * hd:(h + 1) * hd])

    tail = xp_ref[c_in:c_in + SUBLANES, :]
    xp_ref[0:SUBLANES, :] = tail

    @pl.when(c == pl.num_programs(1) - 1)
    def _fin():
        cn_ref[0] = tail
        sn_ref[0] = s_ref[...]


def delta_branch(p, conv_w, conv_buf, a_log, dt_bias, s0, norm_g, bsz, seq):
    c_in = min(seq, CHUNK)
    n_chunks = seq // c_in
    cb = jnp.pad(conv_buf, ((0, 0), (SUBLANES - (CONV_W - 1), 0), (0, 0)))
    lane_vec = lambda v: jnp.zeros((1, LANES), F32).at[0, LANE_ALPHA:LANE_ALPHA + H_A].set(v)
    rb = lambda col: pl.BlockSpec((c_in, W_BR), lambda b, c, col=col: (b * n_chunks + c, col))
    const2 = lambda b, c: (0, 0)
    o, cn, sn = pl.pallas_call(
        functools.partial(_delta_kernel, c_in=c_in), grid=(bsz, n_chunks),
        in_specs=[rb(COL_QA), rb(COL_KA), rb(COL_VA), rb(COL_ZA),
                  pl.BlockSpec((c_in, LANES), lambda b, c: (b * n_chunks + c, COL_SMALL)),
                  pl.BlockSpec((CONV_W, 3 * W_BR), const2),
                  pl.BlockSpec((1, SUBLANES, 3 * W_BR), lambda b, c: (b, 0, 0)),
                  pl.BlockSpec((1, LANES), const2), pl.BlockSpec((1, LANES), const2),
                  pl.BlockSpec((1, H_A, DK_A, DK_A), lambda b, c: (b, 0, 0, 0)),
                  pl.BlockSpec((1, DK_A), const2)],
        out_specs=[pl.BlockSpec((c_in, W_BR), lambda b, c: (b * n_chunks + c, 0)),
                   pl.BlockSpec((1, SUBLANES, 3 * W_BR), lambda b, c: (b, 0, 0)),
                   pl.BlockSpec((1, H_A, DK_A, DK_A), lambda b, c: (b, 0, 0, 0))],
        out_shape=[jax.ShapeDtypeStruct((bsz * seq, W_BR), F32),
                   jax.ShapeDtypeStruct((bsz, SUBLANES, 3 * W_BR), F32),
                   jax.ShapeDtypeStruct((bsz, H_A, DK_A, DK_A), F32)],
        scratch_shapes=[pltpu.VMEM((SUBLANES + CHUNK, 3 * W_BR), F32), pltpu.VMEM((H_A, DK_A, DK_A), F32)],
        compiler_params=_params("arbitrary", "arbitrary"), name="delta")(
            p, p, p, p, p, conv_w, cb, lane_vec(a_log), lane_vec(dt_bias), s0, norm_g.reshape(1, DK_A))
    return o, cn[:, SUBLANES - (CONV_W - 1):], sn


def _mlstm_kernel(q_ref, k_ref, v_ref, z_ref, og_ref, sm_ref, bif_ref, c0_ref, n0_ref, m0_ref, ng_ref,
                  o_ref, cn_ref, nn_ref, mn_ref, c_s, n_s, m_s, *, c_in):
    c = pl.program_id(1)
    C = CHUNK
    hd = DK_C

    @pl.when(c == 0)
    def _init():
        c_s[...] = c0_ref[0]
        n_s[...] = n0_ref[0]
        m_s[...] = m0_ref[0]

    def pad_rows(x):
        if c_in == C:
            return x
        return jnp.concatenate([x, jnp.zeros((C - c_in, x.shape[1]), F32)], axis=0)

    valid = lax.broadcasted_iota(jnp.int32, (C, 1), 0) < c_in
    pre = pad_rows(sm_ref[...]) + bif_ref[...]
    i_pre = jnp.where(valid, pre, NEG_INF)
    log_f = jnp.where(valid, -_softplus(-pre), 0.0)

    ri = lax.broadcasted_iota(jnp.int32, (C, C), 0)
    ci = lax.broadcasted_iota(jnp.int32, (C, C), 1)
    causal = ci <= ri
    bcum = _dot(causal.astype(F32), log_f, HI)
    bcum_t = bcum.T
    i_t = i_pre.T
    m_old = m_s[...]
    m_new_vec = m_old
    lane = lax.broadcasted_iota(jnp.int32, m_old.shape, 1)

    part = []
    for h in range(H_C):
        sl = slice(h * hd, (h + 1) * hd)
        q = pad_rows(q_ref[:, sl])
        ks = pad_rows(k_ref[:, sl]) * (DK_C ** -0.5)
        vb = pad_rows(v_ref[:, sl]).astype(BF16)
        b_col = bcum[:, LANE_F + h:LANE_F + h + 1]
        b_row = bcum_t[LANE_F + h:LANE_F + h + 1, :]
        i_col = i_pre[:, LANE_I + h:LANE_I + h + 1]
        i_row = i_t[LANE_I + h:LANE_I + h + 1, :]
        m_prev = m_old[:, h:h + 1]
        inter = b_col + m_prev
        intra = jnp.where(causal, b_col - b_row + i_row, NEG_INF)
        m_t = jnp.maximum(inter, jnp.max(intra, axis=-1, keepdims=True))
        w_inter = jnp.exp(inter - m_t)
        qb = q.astype(BF16)
        s = _dot_nt(qb, ks.astype(BF16)) * jnp.exp(intra - m_t)
        cm = c_s[h]
        n = n_s[h:h + 1, :]
        num_inter = w_inter * _dot(qb, cm.astype(BF16))
        den = w_inter * jnp.sum(q * n, axis=-1, keepdims=True) + jnp.sum(s, axis=-1, keepdims=True)
        m_new = m_t[C - 1:C, :]
        b_last = b_col[C - 1:C, :]
        w_c = jnp.exp(b_last + m_prev - m_new)
        w_j = jnp.exp(b_last - b_col + i_col - m_new)
        kw = w_j * ks
        c_s[h] = w_c * cm + _dot_tn(kw.astype(BF16), vb)
        n_s[h:h + 1, :] = w_c * n + jnp.sum(kw, axis=0, keepdims=True)
        m_new_vec = jnp.where(lane == h, m_new, m_new_vec)
        part.append((s.astype(BF16), vb, num_inter, jnp.maximum(jnp.abs(den), jnp.exp(-m_t))))
    m_s[...] = m_new_vec

    for h, (sb, vb, num_inter, den) in enumerate(part):
        sl = slice(h * hd, (h + 1) * hd)
        hh = (num_inter + _dot(sb, vb)) / den
        hc = _sigmoid(og_ref[:, sl]) * hh[0:c_in]
        hc = hc * lax.rsqrt(jnp.mean(hc * hc, axis=-1, keepdims=True) + EPS) * ng_ref[...]
        o_ref[:, sl] = hc * _silu(z_ref[:, sl])

    @pl.when(c == pl.num_programs(1) - 1)
    def _fin():
        cn_ref[0] = c_s[...]
        nn_ref[0] = n_s[...]
        mn_ref[0] = m_new_vec


def mlstm_branch(p, b_if, c0, n0, m0, norm_g, bsz, seq):
    c_in = min(seq, CHUNK)
    n_chunks = seq // c_in
    bif = jnp.zeros((1, LANES), F32).at[0, LANE_I:LANE_I + 2 * H_C].set(b_if)
    m0p = jnp.pad(m0, ((0, 0), (0, LANES - H_C))).reshape(bsz, 1, LANES)
    rb = lambda col: pl.BlockSpec((c_in, W_BR), lambda b, c, col=col: (b * n_chunks + c, col))
    const2 = lambda b, c: (0, 0)
    st4 = pl.BlockSpec((1, H_C, DK_C, DK_C), lambda b, c: (b, 0, 0, 0))
    st3 = pl.BlockSpec((1, H_C, DK_C), lambda b, c: (b, 0, 0))
    stm = pl.BlockSpec((1, 1, LANES), lambda b, c: (b, 0, 0))
    o, cn, nn, mn = pl.pallas_call(
        functools.partial(_mlstm_kernel, c_in=c_in), grid=(bsz, n_chunks),
        in_specs=[rb(COL_QC), rb(COL_KC), rb(COL_VC), rb(COL_ZC), rb(COL_OC),
                  pl.BlockSpec((c_in, LANES), lambda b, c: (b * n_chunks + c, COL_SMALL)),
                  pl.BlockSpec((1, LANES), const2), st4, st3, stm, pl.BlockSpec((1, DK_C), const2)],
        out_specs=[pl.BlockSpec((c_in, W_BR), lambda b, c: (b * n_chunks + c, 0)), st4, st3, stm],
        out_shape=[jax.ShapeDtypeStruct((bsz * seq, W_BR), F32),
                   jax.ShapeDtypeStruct((bsz, H_C, DK_C, DK_C), F32),
                   jax.ShapeDtypeStruct((bsz, H_C, DK_C), F32),
                   jax.ShapeDtypeStruct((bsz, 1, LANES), F32)],
        scratch_shapes=[pltpu.VMEM((H_C, DK_C, DK_C), F32), pltpu.VMEM((H_C, DK_C), F32),
                        pltpu.VMEM((1, LANES), F32)],
        compiler_params=_params("arbitrary", "arbitrary"), name="mlstm")(
            p, p, p, p, p, p, bif, c0, n0, m0p, norm_g.reshape(1, DK_C))
    return o, cn, nn, mn[:, 0, :H_C]


def _head_rms(x, g):
    return x * lax.rsqrt(jnp.mean(x * x, axis=-1, keepdims=True) + EPS) * g


def _qknorm_kernel(q_ref, k_ref, qg_ref, kg_ref, qn_ref, kn_ref):
    for h in range(H_B):
        sl = slice(h * HD_B, (h + 1) * HD_B)
        qn_ref[:, sl] = _head_rms(q_ref[:, sl], qg_ref[...]) * (HD_B ** -0.5)
        kn_ref[:, sl] = _head_rms(k_ref[:, sl], kg_ref[...])


def _qknorm_kv_kernel(q_ref, k_ref, v_ref, qg_ref, kg_ref, qn_ref, kn_ref, knb_ref, vt_ref, kbar_ref):
    for h in range(H_B):
        sl = slice(h * HD_B, (h + 1) * HD_B)
        qn_ref[:, sl] = _head_rms(q_ref[:, sl], qg_ref[...]) * (HD_B ** -0.5)
        kn = _head_rms(k_ref[:, sl], kg_ref[...])
        kn_ref[:, sl] = kn
        knb_ref[0, :, sl] = kn.astype(BF16)
        kbar_ref[0, :, sl] = jnp.mean(kn, axis=0, keepdims=True)
    vt_ref[0] = v_ref[...].T.astype(BF16)


def qk_norm(p, qg, kg, with_kv):
    t = p.shape[0]
    qg = qg.reshape(1, HD_B)
    kg = kg.reshape(1, HD_B)
    gs = pl.BlockSpec((1, HD_B), lambda i: (0, 0))
    if not with_kv:
        tm = _pick_tile(t, (256, 128))
        return pl.pallas_call(
            _qknorm_kernel, grid=(t // tm,),
            in_specs=[pl.BlockSpec((tm, W_BR), lambda i: (i, COL_QB)),
                      pl.BlockSpec((tm, W_BR), lambda i: (i, COL_KB)), gs, gs],
            out_specs=[pl.BlockSpec((tm, W_BR), lambda i: (i, 0))] * 2,
            out_shape=[jax.ShapeDtypeStruct((t, W_BR), F32)] * 2,
            compiler_params=_params("arbitrary"), name="qk_norm")(p, p, qg, kg)
    tm = MOBA_BLOCK
    nb = t // tm
    return pl.pallas_call(
        _qknorm_kv_kernel, grid=(nb,),
        in_specs=[pl.BlockSpec((tm, W_BR), lambda i: (i, COL_QB)),
                  pl.BlockSpec((tm, W_BR), lambda i: (i, COL_KB)),
                  pl.BlockSpec((tm, W_BR), lambda i: (i, COL_VB)), gs, gs],
        out_specs=[pl.BlockSpec((tm, W_BR), lambda i: (i, 0)), pl.BlockSpec((tm, W_BR), lambda i: (i, 0)),
                   pl.BlockSpec((1, tm, W_BR), lambda i: (i, 0, 0)),
                   pl.BlockSpec((1, W_BR, tm), lambda i: (i, 0, 0)),
                   pl.BlockSpec((1, 1, W_BR), lambda i: (i, 0, 0))],
        out_shape=[jax.ShapeDtypeStruct((t, W_BR), F32), jax.ShapeDtypeStruct((t, W_BR), F32),
                   jax.ShapeDtypeStruct((nb, tm, W_BR), BF16), jax.ShapeDtypeStruct((nb, W_BR, tm), BF16),
                   jax.ShapeDtypeStruct((nb, 1, W_BR), F32)],
        compiler_params=_params("arbitrary"), name="qk_norm_kv")(p, p, p, qg, kg)


def _select_topk(sc, valid, axis):
    nb = sc.shape[axis]
    idx = lax.broadcasted_iota(jnp.int32, sc.shape, axis).astype(F32)
    if valid is not None:
        sc = jnp.where(valid, sc, NEG_INF)
    sel = jnp.zeros(sc.shape, F32)
    for _ in range(MOBA_TOPK):
        mx = jnp.max(sc, axis=axis, keepdims=True)
        first = jnp.min(jnp.where(sc == mx, idx, float(nb)), axis=axis, keepdims=True)
        hit = idx == first
        sel = jnp.where(hit & (mx > NEG_INF), 1.0, sel)
        sc = jnp.where(hit, NEG_INF, sc)
    return sel


MOBA_HEADS_PER_STEP = 4


def _moba_prompt_kernel(q_ref, k_ref, vt_ref, kbar_ref, o_ref, sel_ref, qb_ref, acc_ref):
    i = pl.program_id(2)
    blk = MOBA_BLOCK
    hp = MOBA_HEADS_PER_STEP
    nb = kbar_ref.shape[1]
    past = lax.broadcasted_iota(jnp.int32, (nb, blk), 0) < i
    kpos = lax.broadcasted_iota(jnp.int32, (blk, blk), 0)
    qpos = lax.broadcasted_iota(jnp.int32, (blk, blk), 1)
    heads = [slice(hh * HD_B, (hh + 1) * HD_B) for hh in range(hp)]

    for hh, sl in enumerate(heads):
        qb_ref[hh] = q_ref[:, sl].astype(BF16)
    s_l = [_dot_nt(k_ref[i, :, sl], qb_ref[hh]) for hh, sl in enumerate(heads)]
    sc_l = [_dot_nt(kbar_ref[0, :, sl], q_ref[:, sl], HI) for sl in heads]
    ms, ls, pr_l = [], [], []
    for hh in range(hp):
        s = jnp.where(kpos <= qpos, s_l[hh], NEG_INF)
        m = jnp.max(s, axis=0, keepdims=True)
        pr = jnp.exp(s - m)
        ms.append(m)
        ls.append(jnp.sum(pr, axis=0, keepdims=True))
        pr_l.append(pr.astype(BF16))
    for hh, sl in enumerate(heads):
        acc_ref[hh] = _dot(vt_ref[i, sl, :], pr_l[hh])
        sel_ref[hh] = _select_topk(sc_l[hh], past, 0)

    def body(n, carry):
        ms, ls = carry
        s_l = [_dot_nt(k_ref[n, :, sl], qb_ref[hh]) for hh, sl in enumerate(heads)]
        ms_new, ls_new, pr_l, alpha_l = [], [], [], []
        for hh in range(hp):
            s = jnp.where(sel_ref[hh, pl.ds(n, 1), :] > 0.0, s_l[hh], NEG_INF)
            m_new = jnp.maximum(ms[hh], jnp.max(s, axis=0, keepdims=True))
            alpha = jnp.exp(ms[hh] - m_new)
            pr = jnp.exp(s - m_new)
            ms_new.append(m_new)
            ls_new.append(alpha * ls[hh] + jnp.sum(pr, axis=0, keepdims=True))
            pr_l.append(pr.astype(BF16))
            alpha_l.append(alpha)
        for hh, sl in enumerate(heads):
            acc_ref[hh] = alpha_l[hh] * acc_ref[hh] + _dot(vt_ref[n, sl, :], pr_l[hh])
        return tuple(ms_new), tuple(ls_new)

    ms, ls = lax.fori_loop(0, i, body, (tuple(ms), tuple(ls)))
    for hh, sl in enumerate(heads):
        o_ref[:, sl] = (acc_ref[hh] / ls[hh]).T


def moba_prompt(qn, knb, vt, kbar, bsz, seq):
    nb = seq // MOBA_BLOCK
    blk = MOBA_BLOCK
    hp = MOBA_HEADS_PER_STEP
    wid = hp * HD_B
    return pl.pallas_call(
        _moba_prompt_kernel, grid=(bsz, H_B // hp, nb),
        in_specs=[pl.BlockSpec((blk, wid), lambda b, h, i: (b * nb + i, h)),
                  pl.BlockSpec((nb, blk, wid), lambda b, h, i: (b, 0, h)),
                  pl.BlockSpec((nb, wid, blk), lambda b, h, i: (b, h, 0)),
                  pl.BlockSpec((1, nb, wid), lambda b, h, i: (b, 0, h))],
        out_specs=pl.BlockSpec((blk, wid), lambda b, h, i: (b * nb + i, h)),
        out_shape=jax.ShapeDtypeStruct((bsz * seq, W_BR), F32),
        scratch_shapes=[pltpu.VMEM((hp, nb, blk), F32), pltpu.VMEM((hp, blk, HD_B), BF16),
                        pltpu.VMEM((hp, HD_B, blk), F32)],
        compiler_params=_params("arbitrary", "arbitrary", "arbitrary"), name="moba_prompt")(qn, knb, vt, kbar)


DECODE_PAGES_PER_STEP = 4


def _moba_decode_kernel(pt_ref, q_ref, qf_ref, bias_ref, obias_ref, kown_ref, vown_ref, *rest, pps, seq):
    del pt_ref
    k_refs = rest[:pps]
    v_refs = rest[pps:2 * pps]
    o_ref = rest[2 * pps]
    m_s, l_s, o_s, ks_s = rest[2 * pps + 1:]
    n = pl.program_id(1)
    q = q_ref[0]
    ppb = MOBA_BLOCK // PAGE_SIZE
    npg = ks_s.shape[0] // H_B
    lane = lax.broadcasted_iota(jnp.int32, m_s.shape, 1)

    def partial_softmax(s):
        m = jnp.max(s, axis=-1, keepdims=True)
        pr = jnp.exp(s - m)
        return m, jnp.sum(pr, axis=-1, keepdims=True), pr

    @pl.when(n == 0)
    def _init():
        m_s[...] = jnp.zeros(m_s.shape, F32)
        l_s[...] = jnp.zeros(l_s.shape, F32)

    m_all = m_s[...]
    l_all = l_s[...]
    s_l = [_dot_nt(q, k_refs[j][0, 0].astype(BF16)) for j in range(pps)]
    pr_l = []
    for j in range(pps):
        m, l, pr = partial_softmax(s_l[j] + bias_ref[...])
        pg = n * pps + j
        m_all = jnp.where(lane == pg, m, m_all)
        l_all = jnp.where(lane == pg, l, l_all)
        pr_l.append(pr.astype(BF16))
    for j in range(pps):
        pg = n * pps + j
        o_s[pg] = _dot(pr_l[j], v_refs[j][0, 0].astype(BF16))
        ks_s[pl.ds(pg * H_B, H_B), :] = jnp.sum(k_refs[j][0, 0].reshape(PAGE_SIZE, H_B, HD_B), axis=0)
    m_s[...] = m_all
    l_s[...] = l_all

    @pl.when(n == pl.num_programs(1) - 1)
    def _combine():
        nb = npg // ppb
        m_o, l_o, pr_o = partial_softmax(_dot_nt(q, kown_ref[0].astype(BF16)) + obias_ref[...])
        o_s[npg] = _dot(pr_o.astype(BF16), vown_ref[0].astype(BF16))
        m_sl = jnp.where(lane == npg, m_o, m_all)
        l_sl = jnp.where(lane == npg, l_o, l_all)
        scp = _dot_nt(qf_ref[0], ks_s[...], HI)
        ch = lax.broadcasted_iota(jnp.int32, scp.shape, 1) % H_B
        rh = lax.broadcasted_iota(jnp.int32, scp.shape, 0) // seq
        scp = jnp.where(ch == rh, scp, 0.0)
        cols_per_blk = ppb * H_B
        gather = (lax.broadcasted_iota(jnp.int32, (npg * H_B, nb), 0) // cols_per_blk
                  == lax.broadcasted_iota(jnp.int32, (npg * H_B, nb), 1)).astype(F32)
        scb = _dot(scp, gather, HI) * (1.0 / MOBA_BLOCK)
        sel = _select_topk(scb, None, 1)
        slot = lax.broadcasted_iota(jnp.int32, (nb, m_s.shape[1]), 1)
        expand = (slot // ppb == lax.broadcasted_iota(jnp.int32, (nb, m_s.shape[1]), 0)) & (slot < npg)
        sel_slot = jnp.where(lane == npg, 1.0, _dot(sel, expand.astype(F32), HI))
        m_sel = jnp.where(sel_slot > 0.0, m_sl, NEG_INF)
        w = sel_slot * jnp.exp(m_sel - jnp.max(m_sel, axis=-1, keepdims=True))
        w = w / jnp.sum(w * l_sl, axis=-1, keepdims=True)
        acc = w[:, 0:1] * o_s[0]
        for pg in range(1, npg + 1):
            acc = acc + w[:, pg:pg + 1] * o_s[pg]
        o_ref[0] = acc


def moba_decode(qn, kn, v, cache_k, cache_v, page_table, layer, bsz, seq):
    ppb = MOBA_BLOCK // PAGE_SIZE
    pps = DECODE_PAGES_PER_STEP
    n_pages = page_table.shape[1]
    nrow = H_B * seq
    assert nrow % SUBLANES == 0 and n_pages // ppb >= MOBA_TOPK and n_pages % ppb == 0
    assert n_pages % pps == 0 and n_pages < LANES
    rows = PAGE_SIZE * H_B
    depth, n_phys = cache_k.shape[:2]
    ck = cache_k.reshape(depth, n_phys, rows, HD_B)
    cv = cache_v.reshape(depth, n_phys, rows, HD_B)
    qc = jnp.transpose(qn.reshape(bsz, seq, H_B, HD_B), (0, 2, 1, 3)).reshape(bsz, nrow, HD_B)
    c_h, c_q = jnp.arange(nrow) // seq, jnp.arange(nrow) % seq
    key_h = jnp.arange(rows) % H_B
    bias = jnp.where(c_h[:, None] == key_h[None, :], 0.0, NEG_INF).astype(F32)
    own = jnp.arange(seq * H_B)
    obias = jnp.where((c_h[:, None] == own[None, :] % H_B) & (own[None, :] // H_B <= c_q[:, None]),
                      0.0, NEG_INF).astype(F32)
    kown = kn.reshape(bsz, seq * H_B, HD_B)
    vown = v.reshape(bsz, seq * H_B, HD_B)

    page = lambda j: pl.BlockSpec((1, 1, rows, HD_B), lambda b, n, pt, j=j: (layer, pt[b, n * pps + j], 0, 0))
    seq3 = lambda shape: pl.BlockSpec((1,) + shape, lambda b, n, pt: (b, 0, 0))
    const2 = lambda shape: pl.BlockSpec(shape, lambda b, n, pt: (0, 0))
    grid_spec = pltpu.PrefetchScalarGridSpec(
        num_scalar_prefetch=1, grid=(bsz, n_pages // pps),
        in_specs=[seq3((nrow, HD_B)), seq3((nrow, HD_B)), const2((nrow, rows)), const2((nrow, seq * H_B)),
                  seq3((seq * H_B, HD_B)), seq3((seq * H_B, HD_B))]
                 + [page(j) for j in range(pps)] + [page(j) for j in range(pps)],
        out_specs=seq3((nrow, HD_B)),
        scratch_shapes=[pltpu.VMEM((nrow, LANES), F32), pltpu.VMEM((nrow, LANES), F32),
                        pltpu.VMEM((n_pages + 1, nrow, HD_B), F32), pltpu.VMEM((n_pages * H_B, HD_B), F32)])
    out = pl.pallas_call(
        functools.partial(_moba_decode_kernel, pps=pps, seq=seq), grid_spec=grid_spec,
        out_shape=jax.ShapeDtypeStruct((bsz, nrow, HD_B), F32),
        compiler_params=_params("arbitrary", "arbitrary"), name="moba_decode")(
            page_table, qc.astype(BF16), qc, bias, obias, kown, vown, *([ck] * pps), *([cv] * pps))
    return jnp.transpose(out.reshape(bsz, H_B, seq, HD_B), (0, 2, 1, 3)).reshape(bsz * seq, W_BR)


def _reorder_w_in(w_in):
    qkv_a = 3 * W_BR
    o_small_a = qkv_a + W_BR
    o_b = o_small_a + 2 * H_A
    o_c = o_b + 4 * W_BR
    o_if = o_c + 5 * W_BR
    o_g = o_if + 2 * H_C
    d = w_in.shape[0]
    pad = jnp.zeros((d, N_SMALL - 2 * H_A - 2 * H_C), w_in.dtype)
    return jnp.concatenate([w_in[:, :o_small_a], w_in[:, o_b:o_if], w_in[:, o_g:],
                            w_in[:, o_small_a:o_b], w_in[:, o_if:o_g], pad], axis=1).astype(BF16)


def _mixer(x, lw, conv_buf, s_a, c_c, n_c, m_c, bsz, seq, attend):
    (ln_g, w_proj, conv_a, a_log, dt_bias, norm_a, qnorm_b, knorm_b, b_if, norm_c, wa, wb, wc, wo) = lw
    h = rms_cast(x, ln_g)
    p = in_proj(h, w_proj)
    oa, conv_new, s_new = delta_branch(p, conv_a, conv_buf, a_log, dt_bias, s_a, norm_a, bsz, seq)
    ob, kn, vb = attend(p, qnorm_b, knorm_b)
    oc, c_new, n_new, m_new = mlstm_branch(p, b_if, c_c, n_c, m_c, norm_c, bsz, seq)
    y = out_proj(x, merge(oa, ob, oc, p, wa, wb, wc), wo)
    kv_shape = (bsz, seq, H_B, HD_B)
    return y, (kn.reshape(kv_shape), vb.reshape(kv_shape), conv_new, s_new, c_new, n_new, m_new)


def kernel(x_prompt, x_sample, cache_k, cache_v, page_table, state_conv_a, state_delta_a, state_mlstm_c,
           state_mlstm_n, state_mlstm_m, ln_g, w_in, conv_a, a_log, dt_bias, norm_a, qnorm_b, knorm_b,
           b_if, norm_c, w_br_a, w_br_b, w_br_c, w_out):
    bp, sp, d = x_prompt.shape
    bs, ss, _ = x_sample.shape
    depth = w_in.shape[0]
    yp = x_prompt.reshape(bp * sp, d)
    ys = x_sample.reshape(bs * ss, d)
    new_p = [[] for _ in range(7)]
    new_s = [[] for _ in range(7)]
    for l in range(depth):
        lw = (ln_g[l], _reorder_w_in(w_in[l]), conv_a[l], a_log[l], dt_bias[l], norm_a[l], qnorm_b[l],
              knorm_b[l], b_if[l], norm_c[l], w_br_a[l].astype(BF16), w_br_b[l].astype(BF16),
              w_br_c[l].astype(BF16), w_out[l].astype(BF16))

        def attend_prompt(p, qg, kg):
            qn, kn, knb, vt, kbar = qk_norm(p, qg, kg, with_kv=True)
            ob = moba_prompt(qn, knb, vt, kbar.reshape(bp, sp // MOBA_BLOCK, W_BR), bp, sp)
            return ob, kn, p[:, COL_VB * W_BR:(COL_VB + 1) * W_BR]

        def attend_sample(p, qg, kg, l=l):
            qn, kn = qk_norm(p, qg, kg, with_kv=False)
            vb = p[:, COL_VB * W_BR:(COL_VB + 1) * W_BR]
            return moba_decode(qn, kn, vb, cache_k, cache_v, page_table, l, bs, ss), kn, vb

        zeros = lambda *s: jnp.zeros(s, F32)
        yp, st_p = _mixer(yp, lw, zeros(bp, CONV_W - 1, 3 * W_BR), zeros(bp, H_A, DK_A, DK_A),
                          zeros(bp, H_C, DK_C, DK_C), zeros(bp, H_C, DK_C), zeros(bp, H_C), bp, sp,
                          attend_prompt)
        ys, st_s = _mixer(ys, lw, state_conv_a[l], state_delta_a[l], state_mlstm_c[l], state_mlstm_n[l],
                          state_mlstm_m[l], bs, ss, attend_sample)
        for i in range(7):
            new_p[i].append(st_p[i])
            new_s[i].append(st_s[i])
    outs_p = [jnp.stack(t) for t in new_p]
    outs_s = [jnp.stack(t) for t in new_s]
    return (yp.reshape(bp, sp, d), ys.reshape(bs, ss, d), *outs_p, *outs_s)
```

```python
import functools

import jax
import jax.numpy as jnp
from jax import lax
from jax.experimental import pallas as pl
from jax.experimental.pallas import tpu as pltpu

F32 = jnp.float32
BF16 = jnp.bfloat16
HI = lax.Precision.HIGHEST
EPS = 1e-6
NEG_INF = float("-inf")

H_A, DK_A = 8, 128
H_B, HD_B = 8, 128
H_C, DK_C = 4, 256
CONV_W = 4
MOBA_BLOCK = 256
MOBA_TOPK = 3
PAGE_SIZE = 128
CHUNK = 64
W_BR = 1024
D_MODEL = 2048

LANES = 128
SUBLANES = 8
VMEM_LIMIT = 48 * 1024 * 1024

COL_QA, COL_KA, COL_VA, COL_ZA = 0, 1, 2, 3
COL_QB, COL_KB, COL_VB, COL_ZB = 4, 5, 6, 7
COL_QC, COL_KC, COL_VC, COL_ZC, COL_OC = 8, 9, 10, 11, 12
COL_GATE = 13
N_BIG = 19 * W_BR
OFF_SMALL_A = 4 * W_BR
OFF_SMALL_C = OFF_SMALL_A + 2 * H_A + 9 * W_BR
OFF_GATE = OFF_SMALL_C + 2 * H_C
WIN_A, WIN_C = OFF_SMALL_A // LANES, OFF_SMALL_C // LANES
LANE_BETA, LANE_ALPHA = 0, H_A
LANE_I = OFF_SMALL_C % LANES
LANE_F = LANE_I + H_C
SHIFT_A, SHIFT_BC, SHIFT_G = 0, 2 * H_A, 2 * H_A + 2 * H_C
FIRST_BC_BLOCK, FIRST_G_BLOCK = COL_QB, COL_GATE


def _params(*sem):
    return pltpu.CompilerParams(dimension_semantics=sem, vmem_limit_bytes=VMEM_LIMIT)


def _sigmoid(x):
    return 1.0 / (1.0 + jnp.exp(-x))


def _silu(x):
    return x * _sigmoid(x)


def _softplus(x):
    return jnp.maximum(x, 0.0) + jnp.log1p(jnp.exp(-jnp.abs(x)))


def _dot(a, b, precision=None):
    return jnp.dot(a, b, precision=precision, preferred_element_type=F32)


def _dot_nt(a, b, precision=None):
    return lax.dot_general(a, b, (((1,), (1,)), ((), ())), precision=precision,
                           preferred_element_type=F32)


def _dot_tn(a, b, precision=None):
    return lax.dot_general(a, b, (((0,), (0,)), ((), ())), precision=precision,
                           preferred_element_type=F32)


def _pick_tile(n, candidates):
    for c in candidates:
        if n % c == 0:
            return c
    return n


def _rms_kernel(x_ref, g_ref, o_ref):
    x = x_ref[...]
    y = x * lax.rsqrt(jnp.mean(x * x, axis=-1, keepdims=True) + EPS)
    o_ref[...] = (y * g_ref[...]).astype(o_ref.dtype)


def rms_cast(x, g):
    t, d = x.shape
    tm = _pick_tile(t, (512, 256, 128))
    return pl.pallas_call(
        _rms_kernel, grid=(t // tm,),
        in_specs=[pl.BlockSpec((tm, d), lambda i: (i, 0)), pl.BlockSpec((1, d), lambda i: (0, 0))],
        out_specs=pl.BlockSpec((tm, d), lambda i: (i, 0)),
        out_shape=jax.ShapeDtypeStruct((t, d), BF16),
        compiler_params=_params("arbitrary"), name="rms_cast")(x, g.reshape(1, d))


def _prep_w_kernel(a_ref, b_ref, o_ref):
    j = pl.program_id(1)

    def emit(shift):
        def go():
            if shift == 0:
                o_ref[0] = a_ref[0].astype(BF16)
            else:
                o_ref[0] = jnp.concatenate([a_ref[0, :, shift:], b_ref[0, :, :shift]], axis=1).astype(BF16)
        return go

    pl.when(j < FIRST_BC_BLOCK)(emit(SHIFT_A))
    pl.when((j >= FIRST_BC_BLOCK) & (j < FIRST_G_BLOCK))(emit(SHIFT_BC))
    pl.when(j >= FIRST_G_BLOCK)(emit(SHIFT_G))


def prep_w(w_in):
    depth, d, _ = w_in.shape
    tr = _pick_tile(d, (512, 256))
    per = W_BR // LANES
    return pl.pallas_call(
        _prep_w_kernel, grid=(depth, N_BIG // W_BR, d // tr),
        in_specs=[pl.BlockSpec((1, tr, W_BR), lambda l, j, r: (l, r, j)),
                  pl.BlockSpec((1, tr, LANES), lambda l, j, r: (l, r, per * (j + 1)))],
        out_specs=pl.BlockSpec((1, tr, W_BR), lambda l, j, r: (l, r, j)),
        out_shape=jax.ShapeDtypeStruct((depth, d, N_BIG), BF16),
        compiler_params=_params("arbitrary", "arbitrary", "arbitrary"), name="prep_w")(w_in, w_in)


def _mm_kernel(a_ref, w_ref, o_ref):
    o_ref[...] = _dot(a_ref[...], w_ref[0])


def in_proj(h, w, layer):
    t, k = h.shape
    n = w.shape[2]
    tm = _pick_tile(t, (1024, 768, 512, 256, 128))
    tn = W_BR
    return pl.pallas_call(
        _mm_kernel, grid=(n // tn, t // tm),
        in_specs=[pl.BlockSpec((tm, k), lambda j, i: (i, 0)), pl.BlockSpec((1, k, tn), lambda j, i: (layer, 0, j))],
        out_specs=pl.BlockSpec((tm, tn), lambda j, i: (i, j)),
        out_shape=jax.ShapeDtypeStruct((t, n), F32),
        compiler_params=_params("arbitrary", "arbitrary"), name="in_proj")(h, w)


def _small_proj_kernel(a_ref, wa_ref, wc_ref, o_ref):
    a = a_ref[...]
    o_ref[:, 0:LANES] = _dot(a, wa_ref[0].astype(BF16))
    o_ref[:, LANES:2 * LANES] = _dot(a, wc_ref[0].astype(BF16))


def small_proj(h, w_in, layer):
    t, k = h.shape
    tm = _pick_tile(t, (1024, 512, 256, 128))
    win = lambda c: pl.BlockSpec((1, k, LANES), lambda i, c=c: (layer, 0, c))
    return pl.pallas_call(
        _small_proj_kernel, grid=(t // tm,),
        in_specs=[pl.BlockSpec((tm, k), lambda i: (i, 0)), win(WIN_A), win(WIN_C)],
        out_specs=pl.BlockSpec((tm, 2 * LANES), lambda i: (i, 0)),
        out_shape=jax.ShapeDtypeStruct((t, 2 * LANES), F32),
        compiler_params=_params("arbitrary"), name="small_proj")(h, w_in, w_in)


def _merge_kernel(oa_ref, ob_ref, oc_ref, zb_ref, ga_ref, gb_ref, gc_ref, wa_ref, wb_ref, wc_ref, o_ref):
    ob = ob_ref[...] * _silu(zb_ref[...])
    m = _sigmoid(ga_ref[...]) * _dot(oa_ref[...].astype(BF16), wa_ref[...])
    m += _sigmoid(gb_ref[...]) * _dot(ob.astype(BF16), wb_ref[...])
    m += _sigmoid(gc_ref[...]) * _dot(oc_ref[...].astype(BF16), wc_ref[...])
    o_ref[...] = m.astype(o_ref.dtype)


def merge(oa, ob, oc, p, wa, wb, wc):
    t = oa.shape[0]
    tm = _pick_tile(t, (512, 256, 128))
    tn = W_BR
    row = lambda j, i: (i, 0)
    return pl.pallas_call(
        _merge_kernel, grid=(D_MODEL // tn, t // tm),
        in_specs=[pl.BlockSpec((tm, W_BR), row), pl.BlockSpec((tm, W_BR), row), pl.BlockSpec((tm, W_BR), row),
                  pl.BlockSpec((tm, W_BR), lambda j, i: (i, COL_ZB)),
                  pl.BlockSpec((tm, tn), lambda j, i: (i, COL_GATE + j)),
                  pl.BlockSpec((tm, tn), lambda j, i: (i, COL_GATE + 2 + j)),
                  pl.BlockSpec((tm, tn), lambda j, i: (i, COL_GATE + 4 + j)),
                  pl.BlockSpec((W_BR, tn), lambda j, i: (0, j)),
                  pl.BlockSpec((W_BR, tn), lambda j, i: (0, j)),
                  pl.BlockSpec((W_BR, tn), lambda j, i: (0, j))],
        out_specs=pl.BlockSpec((tm, tn), lambda j, i: (i, j)),
        out_shape=jax.ShapeDtypeStruct((t, D_MODEL), BF16),
        compiler_params=_params("arbitrary", "arbitrary"), name="merge")(oa, ob, oc, p, p, p, p, wa, wb, wc)


def _outproj_kernel(x_ref, m_ref, w_ref, o_ref):
    o_ref[...] = x_ref[...] + _dot(m_ref[...], w_ref[...])


def out_proj(x, m, w):
    t, d = x.shape
    tm = _pick_tile(t, (512, 256, 128))
    tn = 1024
    return pl.pallas_call(
        _outproj_kernel, grid=(d // tn, t // tm),
        in_specs=[pl.BlockSpec((tm, tn), lambda j, i: (i, j)), pl.BlockSpec((tm, d), lambda j, i: (i, 0)),
                  pl.BlockSpec((d, tn), lambda j, i: (0, j))],
        out_specs=pl.BlockSpec((tm, tn), lambda j, i: (i, j)),
        out_shape=jax.ShapeDtypeStruct((t, d), F32),
        compiler_params=_params("arbitrary", "arbitrary"), name="out_proj")(x, m, w)


def _bdot(a, b):
    return _dot(a.astype(BF16), b.astype(BF16))


def _delta_kernel(q_ref, k_ref, v_ref, z_ref, sm_ref, cw_ref, cb_ref, al_ref, dtb_ref, s0_ref, ng_ref,
                  o_ref, cn_ref, sn_ref, xp_ref, s_ref, *, c_in):
    c = pl.program_id(1)
    C = CHUNK
    hd = DK_A

    @pl.when(c == 0)
    def _init():
        xp_ref[...] = jnp.zeros(xp_ref.shape, F32)
        xp_ref[0:SUBLANES, :] = cb_ref[0]
        s_ref[...] = s0_ref[0]

    xp_ref[SUBLANES:SUBLANES + c_in, 0:W_BR] = q_ref[...]
    xp_ref[SUBLANES:SUBLANES + c_in, W_BR:2 * W_BR] = k_ref[...]
    xp_ref[SUBLANES:SUBLANES + c_in, 2 * W_BR:3 * W_BR] = v_ref[...]

    valid = lax.broadcasted_iota(jnp.int32, (C, 1), 0) < c_in
    sm = sm_ref[...]
    if c_in < C:
        sm = jnp.concatenate([sm, jnp.zeros((C - c_in, sm.shape[1]), F32)], axis=0)
    beta = _sigmoid(sm)
    g = -jnp.exp(al_ref[...]) * _softplus(sm + dtb_ref[...])
    g = jnp.where(valid, g, 0.0)

    ri = lax.broadcasted_iota(jnp.int32, (C, C), 0)
    ci = lax.broadcasted_iota(jnp.int32, (C, C), 1)
    causal = ci <= ri
    strict = ci < ri
    gcum = _dot(causal.astype(F32), g, HI)
    gcum_t = gcum.T
    e_g = jnp.exp(gcum)
    g_last = gcum[C - 1:C, :]
    e_end = jnp.exp(g_last - gcum)
    e_last = jnp.exp(g_last)

    def conv(col):
        sl = slice(col, col + hd)
        y = xp_ref[5:5 + C, sl] * cw_ref[0:1, sl]
        for j in range(1, CONV_W):
            y = y + xp_ref[5 + j:5 + j + C, sl] * cw_ref[j:j + 1, sl]
        return jnp.where(valid, _silu(y), 0.0)

    heads = range(H_A)
    lane_a = [LANE_ALPHA + h for h in heads]
    q_l = [conv(h * hd) for h in heads]
    k_l = [conv(W_BR + h * hd) for h in heads]
    v_l = [conv(2 * W_BR + h * hd) for h in heads]
    q_l = [q * lax.rsqrt(jnp.sum(q * q, axis=-1, keepdims=True) + EPS) * (DK_A ** -0.5) for q in q_l]
    k_l = [k * lax.rsqrt(jnp.sum(k * k, axis=-1, keepdims=True) + EPS) for k in k_l]
    decay_l = [jnp.exp(jnp.where(causal, gcum[:, a:a + 1] - gcum_t[a:a + 1, :], NEG_INF)) for a in lane_a]
    b_l = [beta[:, LANE_BETA + h:LANE_BETA + h + 1] for h in heads]
    kb_l = [k.astype(BF16) for k in k_l]
    qk_kk = [_dot_nt(jnp.concatenate([q_l[h].astype(BF16), kb_l[h]], axis=0), kb_l[h]) for h in heads]
    a_l = [jnp.where(strict, b_l[h] * decay_l[h] * qk_kk[h][C:2 * C], 0.0) for h in heads]
    r_l = [-a for a in a_l]
    pw_l = a_l
    n = 2
    while n < C:
        pw_l = [_bdot(pw, pw) for pw in pw_l]
        r_l = [r_l[h] + pw_l[h] + _bdot(r_l[h], pw_l[h]) for h in heads]
        n *= 2
    rhs_l = [jnp.concatenate([b_l[h] * v_l[h], (b_l[h] * e_g[:, lane_a[h]:lane_a[h] + 1]) * k_l[h]], axis=1)
             for h in heads]
    sol_l = [rhs_l[h] + _bdot(r_l[h], rhs_l[h]) for h in heads]
    qg_l = [q_l[h] * e_g[:, lane_a[h]:lane_a[h] + 1] for h in heads]
    s_l = [s_ref[h] for h in heads]
    kq_s = [_bdot(jnp.concatenate([sol_l[h][:, hd:2 * hd], qg_l[h]], axis=0), s_l[h]) for h in heads]
    wb_l = [(sol_l[h][:, 0:hd] - kq_s[h][0:C]).astype(BF16) for h in heads]
    o_l = [kq_s[h][C:2 * C] + _dot((decay_l[h] * qk_kk[h][0:C]).astype(BF16), wb_l[h]) for h in heads]
    for h in heads:
        kend = (k_l[h] * e_end[:, lane_a[h]:lane_a[h] + 1]).astype(BF16)
        s_ref[h] = e_last[:, lane_a[h]:lane_a[h] + 1] * s_l[h] + _dot_tn(kend, wb_l[h])
    for h in heads:
        o = o_l[h]
        o = o * lax.rsqrt(jnp.mean(o * o, axis=-1, keepdims=True) + EPS) * ng_ref[...]
        o_ref[:, h * hd:(h + 1) * hd] = o[0:c_in] * _silu(z_ref[:, h * hd:(h + 1) * hd])

    tail = xp_ref[c_in:c_in + SUBLANES, :]
    xp_ref[0:SUBLANES, :] = tail

    @pl.when(c == pl.num_programs(1) - 1)
    def _fin():
        cn_ref[0] = tail
        sn_ref[0] = s_ref[...]


def delta_branch(p, ps, conv_w, conv_buf, a_log, dt_bias, s0, st, norm_g, bsz, seq):
    c_in = min(seq, CHUNK)
    n_chunks = seq // c_in
    lane_vec = lambda v: jnp.zeros((1, LANES), F32).at[0, LANE_ALPHA:LANE_ALPHA + H_A].set(v)
    rb = lambda col: pl.BlockSpec((c_in, W_BR), lambda b, c, col=col: (b * n_chunks + c, col))
    const2 = lambda b, c: (0, 0)
    o, cn, sn = pl.pallas_call(
        functools.partial(_delta_kernel, c_in=c_in), grid=(bsz, n_chunks),
        in_specs=[rb(COL_QA), rb(COL_KA), rb(COL_VA), rb(COL_ZA),
                  pl.BlockSpec((c_in, LANES), lambda b, c: (b * n_chunks + c, 0)),
                  pl.BlockSpec((CONV_W, 3 * W_BR), const2),
                  pl.BlockSpec((None, 1, SUBLANES, 3 * W_BR), lambda b, c: (st, b, 0, 0)),
                  pl.BlockSpec((1, LANES), const2), pl.BlockSpec((1, LANES), const2),
                  pl.BlockSpec((None, 1, H_A, DK_A, DK_A), lambda b, c: (st, b, 0, 0, 0)),
                  pl.BlockSpec((1, DK_A), const2)],
        out_specs=[pl.BlockSpec((c_in, W_BR), lambda b, c: (b * n_chunks + c, 0)),
                   pl.BlockSpec((1, SUBLANES, 3 * W_BR), lambda b, c: (b, 0, 0)),
                   pl.BlockSpec((1, H_A, DK_A, DK_A), lambda b, c: (b, 0, 0, 0))],
        out_shape=[jax.ShapeDtypeStruct((bsz * seq, W_BR), F32),
                   jax.ShapeDtypeStruct((bsz, SUBLANES, 3 * W_BR), F32),
                   jax.ShapeDtypeStruct((bsz, H_A, DK_A, DK_A), F32)],
        scratch_shapes=[pltpu.VMEM((SUBLANES + CHUNK, 3 * W_BR), F32), pltpu.VMEM((H_A, DK_A, DK_A), F32)],
        compiler_params=_params("arbitrary", "arbitrary"), name="delta")(
            p, p, p, p, ps, conv_w, conv_buf, lane_vec(a_log), lane_vec(dt_bias), s0, norm_g.reshape(1, DK_A))
    return o, cn[:, SUBLANES - (CONV_W - 1):], sn


def _mlstm_kernel(q_ref, k_ref, v_ref, z_ref, og_ref, sm_ref, bif_ref, c0_ref, n0_ref, m0_ref, ng_ref,
                  o_ref, cn_ref, nn_ref, mn_ref, c_s, n_s, m_s, *, c_in):
    c = pl.program_id(1)
    C = CHUNK
    hd = DK_C

    @pl.when(c == 0)
    def _init():
        c_s[...] = c0_ref[0]
        n_s[...] = n0_ref[0]
        m_s[...] = m0_ref[0]

    def pad_rows(x):
        if c_in == C:
            return x
        return jnp.concatenate([x, jnp.zeros((C - c_in, x.shape[1]), F32)], axis=0)

    valid = lax.broadcasted_iota(jnp.int32, (C, 1), 0) < c_in
    pre = pad_rows(sm_ref[...]) + bif_ref[...]
    i_pre = jnp.where(valid, pre, NEG_INF)
    log_f = jnp.where(valid, -_softplus(-pre), 0.0)

    ri = lax.broadcasted_iota(jnp.int32, (C, C), 0)
    ci = lax.broadcasted_iota(jnp.int32, (C, C), 1)
    causal = ci <= ri
    bcum = _dot(causal.astype(F32), log_f, HI)
    bcum_t = bcum.T
    i_t = i_pre.T
    m_old = m_s[...]
    m_new_vec = m_old
    lane = lax.broadcasted_iota(jnp.int32, m_old.shape, 1)

    part = []
    for h in range(H_C):
        sl = slice(h * hd, (h + 1) * hd)
        q = pad_rows(q_ref[:, sl])
        ks = pad_rows(k_ref[:, sl]) * (DK_C ** -0.5)
        vb = pad_rows(v_ref[:, sl]).astype(BF16)
        b_col = bcum[:, LANE_F + h:LANE_F + h + 1]
        b_row = bcum_t[LANE_F + h:LANE_F + h + 1, :]
        i_col = i_pre[:, LANE_I + h:LANE_I + h + 1]
        i_row = i_t[LANE_I + h:LANE_I + h + 1, :]
        m_prev = m_old[:, h:h + 1]
        inter = b_col + m_prev
        intra = jnp.where(causal, b_col - b_row + i_row, NEG_INF)
        m_t = jnp.maximum(inter, jnp.max(intra, axis=-1, keepdims=True))
        w_inter = jnp.exp(inter - m_t)
        qb = q.astype(BF16)
        s = _dot_nt(qb, ks.astype(BF16)) * jnp.exp(intra - m_t)
        cm = c_s[h]
        n = n_s[h:h + 1, :]
        num_inter = w_inter * _dot(qb, cm.astype(BF16))
        den = w_inter * jnp.sum(q * n, axis=-1, keepdims=True) + jnp.sum(s, axis=-1, keepdims=True)
        m_new = m_t[C - 1:C, :]
        b_last = b_col[C - 1:C, :]
        w_c = jnp.exp(b_last + m_prev - m_new)
        w_j = jnp.exp(b_last - b_col + i_col - m_new)
        kw = w_j * ks
        c_s[h] = w_c * cm + _dot_tn(kw.astype(BF16), vb)
        n_s[h:h + 1, :] = w_c * n + jnp.sum(kw, axis=0, keepdims=True)
        m_new_vec = jnp.where(lane == h, m_new, m_new_vec)
        part.append((s.astype(BF16), vb, num_inter, jnp.maximum(jnp.abs(den), jnp.exp(-m_t))))
    m_s[...] = m_new_vec

    for h, (sb, vb, num_inter, den) in enumerate(part):
        sl = slice(h * hd, (h + 1) * hd)
        hh = (num_inter + _dot(sb, vb)) / den
        hc = _sigmoid(og_ref[:, sl]) * hh[0:c_in]
        hc = hc * lax.rsqrt(jnp.mean(hc * hc, axis=-1, keepdims=True) + EPS) * ng_ref[...]
        o_ref[:, sl] = hc * _silu(z_ref[:, sl])

    @pl.when(c == pl.num_programs(1) - 1)
    def _fin():
        cn_ref[0] = c_s[...]
        nn_ref[0] = n_s[...]
        mn_ref[0] = m_new_vec


def mlstm_branch(p, ps, b_if, c0, n0, m0p, st, norm_g, bsz, seq):
    c_in = min(seq, CHUNK)
    n_chunks = seq // c_in
    bif = jnp.zeros((1, LANES), F32).at[0, LANE_I:LANE_I + 2 * H_C].set(b_if)
    rb = lambda col: pl.BlockSpec((c_in, W_BR), lambda b, c, col=col: (b * n_chunks + c, col))
    const2 = lambda b, c: (0, 0)
    st4 = pl.BlockSpec((1, H_C, DK_C, DK_C), lambda b, c: (b, 0, 0, 0))
    st3 = pl.BlockSpec((1, H_C, DK_C), lambda b, c: (b, 0, 0))
    stm = pl.BlockSpec((1, 1, LANES), lambda b, c: (b, 0, 0))
    in4 = pl.BlockSpec((None, 1, H_C, DK_C, DK_C), lambda b, c: (st, b, 0, 0, 0))
    in3 = pl.BlockSpec((None, 1, H_C, DK_C), lambda b, c: (st, b, 0, 0))
    inm = pl.BlockSpec((None, 1, 1, LANES), lambda b, c: (st, b, 0, 0))
    o, cn, nn, mn = pl.pallas_call(
        functools.partial(_mlstm_kernel, c_in=c_in), grid=(bsz, n_chunks),
        in_specs=[rb(COL_QC), rb(COL_KC), rb(COL_VC), rb(COL_ZC), rb(COL_OC),
                  pl.BlockSpec((c_in, LANES), lambda b, c: (b * n_chunks + c, 1)),
                  pl.BlockSpec((1, LANES), const2), in4, in3, inm, pl.BlockSpec((1, DK_C), const2)],
        out_specs=[pl.BlockSpec((c_in, W_BR), lambda b, c: (b * n_chunks + c, 0)), st4, st3, stm],
        out_shape=[jax.ShapeDtypeStruct((bsz * seq, W_BR), F32),
                   jax.ShapeDtypeStruct((bsz, H_C, DK_C, DK_C), F32),
                   jax.ShapeDtypeStruct((bsz, H_C, DK_C), F32),
                   jax.ShapeDtypeStruct((bsz, 1, LANES), F32)],
        scratch_shapes=[pltpu.VMEM((H_C, DK_C, DK_C), F32), pltpu.VMEM((H_C, DK_C), F32),
                        pltpu.VMEM((1, LANES), F32)],
        compiler_params=_params("arbitrary", "arbitrary"), name="mlstm")(
            p, p, p, p, p, ps, bif, c0, n0, m0p, norm_g.reshape(1, DK_C))
    return o, cn, nn, mn[:, 0, :H_C]


def _head_rms(x, g):
    return x * lax.rsqrt(jnp.mean(x * x, axis=-1, keepdims=True) + EPS) * g


def _qknorm_kernel(q_ref, k_ref, qg_ref, kg_ref, qn_ref, kn_ref):
    for h in range(H_B):
        sl = slice(h * HD_B, (h + 1) * HD_B)
        qn_ref[:, sl] = _head_rms(q_ref[:, sl], qg_ref[...]) * (HD_B ** -0.5)
        kn_ref[:, sl] = _head_rms(k_ref[:, sl], kg_ref[...])


def _qknorm_kv_kernel(q_ref, k_ref, v_ref, qg_ref, kg_ref, qn_ref, kn_ref, knb_ref, vt_ref, kbar_ref):
    for h in range(H_B):
        sl = slice(h * HD_B, (h + 1) * HD_B)
        qn_ref[:, sl] = _head_rms(q_ref[:, sl], qg_ref[...]) * (HD_B ** -0.5)
        kn = _head_rms(k_ref[:, sl], kg_ref[...])
        kn_ref[:, sl] = kn
        knb_ref[0, :, sl] = kn.astype(BF16)
        kbar_ref[0, :, sl] = jnp.mean(kn, axis=0, keepdims=True)
    vt_ref[0] = v_ref[...].T.astype(BF16)


def qk_norm(p, qg, kg, with_kv):
    t = p.shape[0]
    qg = qg.reshape(1, HD_B)
    kg = kg.reshape(1, HD_B)
    gs = pl.BlockSpec((1, HD_B), lambda i: (0, 0))
    if not with_kv:
        tm = _pick_tile(t, (256, 128))
        return pl.pallas_call(
            _qknorm_kernel, grid=(t // tm,),
            in_specs=[pl.BlockSpec((tm, W_BR), lambda i: (i, COL_QB)),
                      pl.BlockSpec((tm, W_BR), lambda i: (i, COL_KB)), gs, gs],
            out_specs=[pl.BlockSpec((tm, W_BR), lambda i: (i, 0))] * 2,
            out_shape=[jax.ShapeDtypeStruct((t, W_BR), F32)] * 2,
            compiler_params=_params("arbitrary"), name="qk_norm")(p, p, qg, kg)
    tm = MOBA_BLOCK
    nb = t // tm
    return pl.pallas_call(
        _qknorm_kv_kernel, grid=(nb,),
        in_specs=[pl.BlockSpec((tm, W_BR), lambda i: (i, COL_QB)),
                  pl.BlockSpec((tm, W_BR), lambda i: (i, COL_KB)),
                  pl.BlockSpec((tm, W_BR), lambda i: (i, COL_VB)), gs, gs],
        out_specs=[pl.BlockSpec((tm, W_BR), lambda i: (i, 0)), pl.BlockSpec((tm, W_BR), lambda i: (i, 0)),
                   pl.BlockSpec((1, tm, W_BR), lambda i: (i, 0, 0)),
                   pl.BlockSpec((1, W_BR, tm), lambda i: (i, 0, 0)),
                   pl.BlockSpec((1, 1, W_BR), lambda i: (i, 0, 0))],
        out_shape=[jax.ShapeDtypeStruct((t, W_BR), F32), jax.ShapeDtypeStruct((t, W_BR), F32),
                   jax.ShapeDtypeStruct((nb, tm, W_BR), BF16), jax.ShapeDtypeStruct((nb, W_BR, tm), BF16),
                   jax.ShapeDtypeStruct((nb, 1, W_BR), F32)],
        compiler_params=_params("arbitrary"), name="qk_norm_kv")(p, p, p, qg, kg)


def _select_topk(sc, valid, axis):
    nb = sc.shape[axis]
    idx = lax.broadcasted_iota(jnp.int32, sc.shape, axis).astype(F32)
    if valid is not None:
        sc = jnp.where(valid, sc, NEG_INF)
    sel = jnp.zeros(sc.shape, F32)
    for _ in range(MOBA_TOPK):
        mx = jnp.max(sc, axis=axis, keepdims=True)
        first = jnp.min(jnp.where(sc == mx, idx, float(nb)), axis=axis, keepdims=True)
        hit = idx == first
        sel = jnp.where(hit & (mx > NEG_INF), 1.0, sel)
        sc = jnp.where(hit, NEG_INF, sc)
    return sel


MOBA_HEADS_PER_STEP = 4


def _moba_prompt_kernel(q_ref, k_ref, vt_ref, kbar_ref, o_ref, sel_ref, qb_ref, acc_ref):
    i = pl.program_id(2)
    blk = MOBA_BLOCK
    hp = MOBA_HEADS_PER_STEP
    nb = kbar_ref.shape[1]
    past = lax.broadcasted_iota(jnp.int32, (nb, blk), 0) < i
    kpos = lax.broadcasted_iota(jnp.int32, (blk, blk), 0)
    qpos = lax.broadcasted_iota(jnp.int32, (blk, blk), 1)
    heads = [slice(hh * HD_B, (hh + 1) * HD_B) for hh in range(hp)]

    for hh, sl in enumerate(heads):
        qb_ref[hh] = q_ref[:, sl].astype(BF16)
    s_l = [_dot_nt(k_ref[i, :, sl], qb_ref[hh]) for hh, sl in enumerate(heads)]
    sc_l = [_dot_nt(kbar_ref[0, :, sl], q_ref[:, sl], HI) for sl in heads]
    ms, ls, pr_l = [], [], []
    for hh in range(hp):
        s = jnp.where(kpos <= qpos, s_l[hh], NEG_INF)
        m = jnp.max(s, axis=0, keepdims=True)
        pr = jnp.exp(s - m)
        ms.append(m)
        ls.append(jnp.sum(pr, axis=0, keepdims=True))
        pr_l.append(pr.astype(BF16))
    for hh, sl in enumerate(heads):
        acc_ref[hh] = _dot(vt_ref[i, sl, :], pr_l[hh])
        sel_ref[hh] = _select_topk(sc_l[hh], past, 0)

    def body(n, carry):
        ms, ls = carry
        s_l = [_dot_nt(k_ref[n, :, sl], qb_ref[hh]) for hh, sl in enumerate(heads)]
        ms_new, ls_new, pr_l, alpha_l = [], [], [], []
        for hh in range(hp):
            s = jnp.where(sel_ref[hh, pl.ds(n, 1), :] > 0.0, s_l[hh], NEG_INF)
            m_new = jnp.maximum(ms[hh], jnp.max(s, axis=0, keepdims=True))
            alpha = jnp.exp(ms[hh] - m_new)
            pr = jnp.exp(s - m_new)
            ms_new.append(m_new)
            ls_new.append(alpha * ls[hh] + jnp.sum(pr, axis=0, keepdims=True))
            pr_l.append(pr.astype(BF16))
            alpha_l.append(alpha)
        for hh, sl in enumerate(heads):
            acc_ref[hh] = alpha_l[hh] * acc_ref[hh] + _dot(vt_ref[n, sl, :], pr_l[hh])
        return tuple(ms_new), tuple(ls_new)

    ms, ls = lax.fori_loop(0, i, body, (tuple(ms), tuple(ls)))
    for hh, sl in enumerate(heads):
        o_ref[:, sl] = (acc_ref[hh] / ls[hh]).T


def moba_prompt(qn, knb, vt, kbar, bsz, seq):
    nb = seq // MOBA_BLOCK
    blk = MOBA_BLOCK
    hp = MOBA_HEADS_PER_STEP
    wid = hp * HD_B
    return pl.pallas_call(
        _moba_prompt_kernel, grid=(bsz, H_B // hp, nb),
        in_specs=[pl.BlockSpec((blk, wid), lambda b, h, i: (b * nb + i, h)),
                  pl.BlockSpec((nb, blk, wid), lambda b, h, i: (b, 0, h)),
                  pl.BlockSpec((nb, wid, blk), lambda b, h, i: (b, h, 0)),
                  pl.BlockSpec((1, nb, wid), lambda b, h, i: (b, 0, h))],
        out_specs=pl.BlockSpec((blk, wid), lambda b, h, i: (b * nb + i, h)),
        out_shape=jax.ShapeDtypeStruct((bsz * seq, W_BR), F32),
        scratch_shapes=[pltpu.VMEM((hp, nb, blk), F32), pltpu.VMEM((hp, blk, HD_B), BF16),
                        pltpu.VMEM((hp, HD_B, blk), F32)],
        compiler_params=_params("arbitrary", "arbitrary", "arbitrary"), name="moba_prompt")(qn, knb, vt, kbar)


DECODE_PAGES_PER_STEP = 8


def _moba_decode_kernel(pt_ref, q_ref, qf_ref, bias_ref, obias_ref, kown_ref, vown_ref, *rest, pps, seq):
    del pt_ref
    k_refs = rest[:pps]
    v_refs = rest[pps:2 * pps]
    o_ref = rest[2 * pps]
    m_s, l_s, o_s, ks_s = rest[2 * pps + 1:]
    n = pl.program_id(1)
    q = q_ref[0]
    ppb = MOBA_BLOCK // PAGE_SIZE
    npg = ks_s.shape[0] // H_B
    lane = lax.broadcasted_iota(jnp.int32, m_s.shape, 1)

    def partial_softmax(s):
        m = jnp.max(s, axis=-1, keepdims=True)
        pr = jnp.exp(s - m)
        return m, jnp.sum(pr, axis=-1, keepdims=True), pr

    @pl.when(n == 0)
    def _init():
        m_s[...] = jnp.zeros(m_s.shape, F32)
        l_s[...] = jnp.zeros(l_s.shape, F32)

    m_all = m_s[...]
    l_all = l_s[...]
    s_l = [_dot_nt(q, k_refs[j][0, 0].astype(BF16)) for j in range(pps)]
    pr_l = []
    for j in range(pps):
        m, l, pr = partial_softmax(s_l[j] + bias_ref[...])
        pg = n * pps + j
        m_all = jnp.where(lane == pg, m, m_all)
        l_all = jnp.where(lane == pg, l, l_all)
        pr_l.append(pr.astype(BF16))
    for j in range(pps):
        pg = n * pps + j
        o_s[pg] = _dot(pr_l[j], v_refs[j][0, 0].astype(BF16))
        ks_s[pl.ds(pg * H_B, H_B), :] = jnp.sum(k_refs[j][0, 0].reshape(PAGE_SIZE, H_B, HD_B), axis=0)
    m_s[...] = m_all
    l_s[...] = l_all

    @pl.when(n == pl.num_programs(1) - 1)
    def _combine():
        nb = npg // ppb
        m_o, l_o, pr_o = partial_softmax(_dot_nt(q, kown_ref[0].astype(BF16)) + obias_ref[...])
        o_s[npg] = _dot(pr_o.astype(BF16), vown_ref[0].astype(BF16))
        m_sl = jnp.where(lane == npg, m_o, m_all)
        l_sl = jnp.where(lane == npg, l_o, l_all)
        scp = _dot_nt(qf_ref[0], ks_s[...], HI)
        ch = lax.broadcasted_iota(jnp.int32, scp.shape, 1) % H_B
        rh = lax.broadcasted_iota(jnp.int32, scp.shape, 0) // seq
        scp = jnp.where(ch == rh, scp, 0.0)
        cols_per_blk = ppb * H_B
        gather = (lax.broadcasted_iota(jnp.int32, (npg * H_B, nb), 0) // cols_per_blk
                  == lax.broadcasted_iota(jnp.int32, (npg * H_B, nb), 1)).astype(F32)
        scb = _dot(scp, gather, HI) * (1.0 / MOBA_BLOCK)
        sel = _select_topk(scb, None, 1)
        slot = lax.broadcasted_iota(jnp.int32, (nb, m_s.shape[1]), 1)
        expand = (slot // ppb == lax.broadcasted_iota(jnp.int32, (nb, m_s.shape[1]), 0)) & (slot < npg)
        sel_slot = jnp.where(lane == npg, 1.0, _dot(sel, expand.astype(F32), HI))
        m_sel = jnp.where(sel_slot > 0.0, m_sl, NEG_INF)
        w = sel_slot * jnp.exp(m_sel - jnp.max(m_sel, axis=-1, keepdims=True))
        w = w / jnp.sum(w * l_sl, axis=-1, keepdims=True)
        acc = w[:, 0:1] * o_s[0]
        for pg in range(1, npg + 1):
            acc = acc + w[:, pg:pg + 1] * o_s[pg]
        o_ref[0] = acc


def moba_decode(qn, kn, v, cache_k, cache_v, page_table, layer, bsz, seq):
    ppb = MOBA_BLOCK // PAGE_SIZE
    pps = DECODE_PAGES_PER_STEP
    n_pages = page_table.shape[1]
    nrow = H_B * seq
    assert nrow % SUBLANES == 0 and n_pages // ppb >= MOBA_TOPK and n_pages % ppb == 0
    assert n_pages % pps == 0 and n_pages < LANES
    rows = PAGE_SIZE * H_B
    depth, n_phys = cache_k.shape[:2]
    ck = cache_k.reshape(depth, n_phys, rows, HD_B)
    cv = cache_v.reshape(depth, n_phys, rows, HD_B)
    qc = jnp.transpose(qn.reshape(bsz, seq, H_B, HD_B), (0, 2, 1, 3)).reshape(bsz, nrow, HD_B)
    c_h, c_q = jnp.arange(nrow) // seq, jnp.arange(nrow) % seq
    key_h = jnp.arange(rows) % H_B
    bias = jnp.where(c_h[:, None] == key_h[None, :], 0.0, NEG_INF).astype(F32)
    own = jnp.arange(seq * H_B)
    obias = jnp.where((c_h[:, None] == own[None, :] % H_B) & (own[None, :] // H_B <= c_q[:, None]),
                      0.0, NEG_INF).astype(F32)
    kown = kn.reshape(bsz, seq * H_B, HD_B)
    vown = v.reshape(bsz, seq * H_B, HD_B)

    page = lambda j: pl.BlockSpec((1, 1, rows, HD_B), lambda b, n, pt, j=j: (layer, pt[b, n * pps + j], 0, 0))
    seq3 = lambda shape: pl.BlockSpec((1,) + shape, lambda b, n, pt: (b, 0, 0))
    const2 = lambda shape: pl.BlockSpec(shape, lambda b, n, pt: (0, 0))
    grid_spec = pltpu.PrefetchScalarGridSpec(
        num_scalar_prefetch=1, grid=(bsz, n_pages // pps),
        in_specs=[seq3((nrow, HD_B)), seq3((nrow, HD_B)), const2((nrow, rows)), const2((nrow, seq * H_B)),
                  seq3((seq * H_B, HD_B)), seq3((seq * H_B, HD_B))]
                 + [page(j) for j in range(pps)] + [page(j) for j in range(pps)],
        out_specs=seq3((nrow, HD_B)),
        scratch_shapes=[pltpu.VMEM((nrow, LANES), F32), pltpu.VMEM((nrow, LANES), F32),
                        pltpu.VMEM((n_pages + 1, nrow, HD_B), F32), pltpu.VMEM((n_pages * H_B, HD_B), F32)])
    out = pl.pallas_call(
        functools.partial(_moba_decode_kernel, pps=pps, seq=seq), grid_spec=grid_spec,
        out_shape=jax.ShapeDtypeStruct((bsz, nrow, HD_B), F32),
        compiler_params=_params("arbitrary", "arbitrary"), name="moba_decode")(
            page_table, qc.astype(BF16), qc, bias, obias, kown, vown, *([ck] * pps), *([cv] * pps))
    return jnp.transpose(out.reshape(bsz, H_B, seq, HD_B), (0, 2, 1, 3)).reshape(bsz * seq, W_BR)


def _mixer(x, layer, lw, states, st, bsz, seq, attend):
    (ln_g, w_proj, w_in, conv_a, a_log, dt_bias, norm_a, qnorm_b, knorm_b, b_if, norm_c, wa, wb, wc, wo) = lw
    conv_buf, s_a, c_c, n_c, m_c = states
    h = rms_cast(x, ln_g)
    p = in_proj(h, w_proj, layer)
    ps = small_proj(h, w_in, layer)
    oa, conv_new, s_new = delta_branch(p, ps, conv_a, conv_buf, a_log, dt_bias, s_a, st, norm_a, bsz, seq)
    ob, kn, vb = attend(p, qnorm_b, knorm_b)
    oc, c_new, n_new, m_new = mlstm_branch(p, ps, b_if, c_c, n_c, m_c, st, norm_c, bsz, seq)
    y = out_proj(x, merge(oa, ob, oc, p, wa, wb, wc), wo)
    kv_shape = (bsz, seq, H_B, HD_B)
    return y, (kn.reshape(kv_shape), vb.reshape(kv_shape), conv_new, s_new, c_new, n_new, m_new)


def _pad_states(conv, delta, c, n, m):
    conv = jnp.pad(conv, ((0, 0), (0, 0), (SUBLANES - (CONV_W - 1), 0), (0, 0)))
    m = jnp.pad(m, ((0, 0), (0, 0), (0, LANES - H_C)))[:, :, None, :]
    return conv, delta, c, n, m


def kernel(x_prompt, x_sample, cache_k, cache_v, page_table, state_conv_a, state_delta_a, state_mlstm_c,
           state_mlstm_n, state_mlstm_m, ln_g, w_in, conv_a, a_log, dt_bias, norm_a, qnorm_b, knorm_b,
           b_if, norm_c, w_br_a, w_br_b, w_br_c, w_out):
    bp, sp, d = x_prompt.shape
    bs, ss, _ = x_sample.shape
    depth = w_in.shape[0]
    yp = x_prompt.reshape(bp * sp, d)
    ys = x_sample.reshape(bs * ss, d)
    new_p = [[] for _ in range(7)]
    new_s = [[] for _ in range(7)]
    w_proj = prep_w(w_in)
    zeros = lambda *s: jnp.zeros(s, F32)
    states_p = _pad_states(zeros(1, bp, CONV_W - 1, 3 * W_BR), zeros(1, bp, H_A, DK_A, DK_A),
                           zeros(1, bp, H_C, DK_C, DK_C), zeros(1, bp, H_C, DK_C), zeros(1, bp, H_C))
    states_s = _pad_states(state_conv_a, state_delta_a, state_mlstm_c, state_mlstm_n, state_mlstm_m)
    for l in range(depth):
        lw = (ln_g[l], w_proj, w_in, conv_a[l], a_log[l], dt_bias[l], norm_a[l], qnorm_b[l],
              knorm_b[l], b_if[l], norm_c[l], w_br_a[l].astype(BF16), w_br_b[l].astype(BF16),
              w_br_c[l].astype(BF16), w_out[l].astype(BF16))

        def attend_prompt(p, qg, kg):
            qn, kn, knb, vt, kbar = qk_norm(p, qg, kg, with_kv=True)
            ob = moba_prompt(qn, knb, vt, kbar.reshape(bp, sp // MOBA_BLOCK, W_BR), bp, sp)
            return ob, kn, p[:, COL_VB * W_BR:(COL_VB + 1) * W_BR]

        def attend_sample(p, qg, kg, l=l):
            qn, kn = qk_norm(p, qg, kg, with_kv=False)
            vb = p[:, COL_VB * W_BR:(COL_VB + 1) * W_BR]
            return moba_decode(qn, kn, vb, cache_k, cache_v, page_table, l, bs, ss), kn, vb

        yp, st_p = _mixer(yp, l, lw, states_p, 0, bp, sp, attend_prompt)
        ys, st_s = _mixer(ys, l, lw, states_s, l, bs, ss, attend_sample)
        for i in range(7):
            new_p[i].append(st_p[i])
            new_s[i].append(st_s[i])
    outs_p = [jnp.stack(t) for t in new_p]
    outs_s = [jnp.stack(t) for t in new_s]
    return (yp.reshape(bp, sp, d), ys.reshape(bs, ss, d), *outs_p, *outs_s)
```

```python
import functools

import jax
import jax.numpy as jnp
from jax import lax
from jax.experimental import pallas as pl
from jax.experimental.pallas import tpu as pltpu

F32 = jnp.float32
BF16 = jnp.bfloat16
HI = lax.Precision.HIGHEST
EPS = 1e-6
NEG_INF = float("-inf")

H_A, DK_A = 8, 128
H_B, HD_B = 8, 128
H_C, DK_C = 4, 256
CONV_W = 4
MOBA_BLOCK = 256
MOBA_TOPK = 3
PAGE_SIZE = 128
CHUNK = 64
W_BR = 1024
D_MODEL = 2048

LANES = 128
SUBLANES = 8
BF16_ROWS = 16
LOG2E = 1.4426950408889634
VMEM_LIMIT = 48 * 1024 * 1024

COL_QA, COL_KA, COL_VA, COL_ZA = 0, 1, 2, 3
COL_QB, COL_KB, COL_VB, COL_ZB = 4, 5, 6, 7
COL_QC, COL_KC, COL_VC, COL_ZC, COL_OC = 8, 9, 10, 11, 12
COL_GATE = 13
N_BIG = 19 * W_BR
OFF_SMALL_A = 4 * W_BR
OFF_SMALL_C = OFF_SMALL_A + 2 * H_A + 9 * W_BR
OFF_GATE = OFF_SMALL_C + 2 * H_C
WIN_A, WIN_C = OFF_SMALL_A // LANES, OFF_SMALL_C // LANES
LANE_BETA, LANE_ALPHA = 0, H_A
LANE_I = OFF_SMALL_C % LANES
LANE_F = LANE_I + H_C
SHIFT_A, SHIFT_BC, SHIFT_G = 0, 2 * H_A, 2 * H_A + 2 * H_C
FIRST_BC_BLOCK, FIRST_G_BLOCK = COL_QB, COL_GATE


def _params(*sem):
    return pltpu.CompilerParams(dimension_semantics=sem, vmem_limit_bytes=VMEM_LIMIT)


def _sigmoid(x):
    return 1.0 / (1.0 + jnp.exp(-x))


def _silu(x):
    return x * _sigmoid(x)


def _softplus(x):
    return jnp.maximum(x, 0.0) + jnp.log1p(jnp.exp(-jnp.abs(x)))


def _dot(a, b, precision=None):
    return jnp.dot(a, b, precision=precision, preferred_element_type=F32)


def _dot_nt(a, b, precision=None):
    return lax.dot_general(a, b, (((1,), (1,)), ((), ())), precision=precision,
                           preferred_element_type=F32)


def _dot_tn(a, b, precision=None):
    return lax.dot_general(a, b, (((0,), (0,)), ((), ())), precision=precision,
                           preferred_element_type=F32)


def _pick_tile(n, candidates):
    for c in candidates:
        if n % c == 0:
            return c
    return n


def _rms_kernel(x_ref, g_ref, o_ref):
    x = x_ref[...]
    y = x * lax.rsqrt(jnp.mean(x * x, axis=-1, keepdims=True) + EPS)
    o_ref[...] = (y * g_ref[...]).astype(o_ref.dtype)


def rms_cast(x, g):
    t, d = x.shape
    tm = _pick_tile(t, (512, 256, 128))
    return pl.pallas_call(
        _rms_kernel, grid=(t // tm,),
        in_specs=[pl.BlockSpec((tm, d), lambda i: (i, 0)), pl.BlockSpec((1, d), lambda i: (0, 0))],
        out_specs=pl.BlockSpec((tm, d), lambda i: (i, 0)),
        out_shape=jax.ShapeDtypeStruct((t, d), BF16),
        compiler_params=_params("arbitrary"), name="rms_cast")(x, g.reshape(1, d))


def _in_proj_kernel(a_ref, w_ref, o_ref, wb_s):
    @pl.when(pl.program_id(1) == 0)
    def _cast():
        wb_s[...] = w_ref[...].astype(BF16)

    o_ref[...] = _dot_nt(a_ref[...], wb_s[...])


def in_proj(h, w_t, layer):
    t, k = h.shape
    tm = _pick_tile(t, (1024, 768, 512, 256, 128))
    tn = W_BR

    def w_rows(j, i):
        shift = jnp.where(j < FIRST_BC_BLOCK, SHIFT_A // SUBLANES,
                          jnp.where(j < FIRST_G_BLOCK, SHIFT_BC // SUBLANES, SHIFT_G // SUBLANES))
        return layer, (j * (tn // SUBLANES) + shift) * SUBLANES, 0

    return pl.pallas_call(
        _in_proj_kernel, grid=(N_BIG // tn, t // tm),
        in_specs=[pl.BlockSpec((tm, k), lambda j, i: (i, 0)),
                  pl.BlockSpec((None, pl.Element(tn), pl.Element(k)), w_rows)],
        out_specs=pl.BlockSpec((tm, tn), lambda j, i: (i, j)),
        out_shape=jax.ShapeDtypeStruct((t, N_BIG), F32),
        scratch_shapes=[pltpu.VMEM((tn, k), BF16)],
        compiler_params=_params("arbitrary", "arbitrary"), name="in_proj")(h, w_t)


def _small_proj_kernel(a_ref, wa_ref, wc_ref, o_ref):
    a = a_ref[...]
    o_ref[:, 0:LANES] = _dot_nt(a, wa_ref[0].astype(BF16))
    o_ref[:, LANES:2 * LANES] = _dot_nt(a, wc_ref[0].astype(BF16))


def small_proj(h, w_t, layer):
    t, k = h.shape
    tm = _pick_tile(t, (1024, 512, 256, 128))
    win = lambda c: pl.BlockSpec((1, LANES, k), lambda i, c=c: (layer, c, 0))
    return pl.pallas_call(
        _small_proj_kernel, grid=(t // tm,),
        in_specs=[pl.BlockSpec((tm, k), lambda i: (i, 0)), win(WIN_A), win(WIN_C)],
        out_specs=pl.BlockSpec((tm, 2 * LANES), lambda i: (i, 0)),
        out_shape=jax.ShapeDtypeStruct((t, 2 * LANES), F32),
        compiler_params=_params("arbitrary"), name="small_proj")(h, w_t, w_t)


def _merge_kernel(oa_ref, ob_ref, oc_ref, zb_ref, ga_ref, gb_ref, gc_ref, wa_ref, wb_ref, wc_ref, o_ref):
    ob = ob_ref[...] * _silu(zb_ref[...])
    m = _sigmoid(ga_ref[...]) * _dot(oa_ref[...].astype(BF16), wa_ref[...])
    m += _sigmoid(gb_ref[...]) * _dot(ob.astype(BF16), wb_ref[...])
    m += _sigmoid(gc_ref[...]) * _dot(oc_ref[...].astype(BF16), wc_ref[...])
    o_ref[...] = m.astype(o_ref.dtype)


def merge(oa, ob, oc, p, wa, wb, wc):
    t = oa.shape[0]
    tm = _pick_tile(t, (512, 256, 128))
    tn = W_BR
    row = lambda j, i: (i, 0)
    return pl.pallas_call(
        _merge_kernel, grid=(D_MODEL // tn, t // tm),
        in_specs=[pl.BlockSpec((tm, W_BR), row), pl.BlockSpec((tm, W_BR), row), pl.BlockSpec((tm, W_BR), row),
                  pl.BlockSpec((tm, W_BR), lambda j, i: (i, COL_ZB)),
                  pl.BlockSpec((tm, tn), lambda j, i: (i, COL_GATE + j)),
                  pl.BlockSpec((tm, tn), lambda j, i: (i, COL_GATE + 2 + j)),
                  pl.BlockSpec((tm, tn), lambda j, i: (i, COL_GATE + 4 + j)),
                  pl.BlockSpec((W_BR, tn), lambda j, i: (0, j)),
                  pl.BlockSpec((W_BR, tn), lambda j, i: (0, j)),
                  pl.BlockSpec((W_BR, tn), lambda j, i: (0, j))],
        out_specs=pl.BlockSpec((tm, tn), lambda j, i: (i, j)),
        out_shape=jax.ShapeDtypeStruct((t, D_MODEL), BF16),
        compiler_params=_params("arbitrary", "arbitrary"), name="merge")(oa, ob, oc, p, p, p, p, wa, wb, wc)


def _outproj_kernel(x_ref, m_ref, w_ref, o_ref):
    o_ref[...] = x_ref[...] + _dot(m_ref[...], w_ref[...])


def out_proj(x, m, w):
    t, d = x.shape
    tm = _pick_tile(t, (512, 256, 128))
    tn = 1024
    return pl.pallas_call(
        _outproj_kernel, grid=(d // tn, t // tm),
        in_specs=[pl.BlockSpec((tm, tn), lambda j, i: (i, j)), pl.BlockSpec((tm, d), lambda j, i: (i, 0)),
                  pl.BlockSpec((d, tn), lambda j, i: (0, j))],
        out_specs=pl.BlockSpec((tm, tn), lambda j, i: (i, j)),
        out_shape=jax.ShapeDtypeStruct((t, d), F32),
        compiler_params=_params("arbitrary", "arbitrary"), name="out_proj")(x, m, w)


def _bdot(a, b):
    return _dot(a.astype(BF16), b.astype(BF16))


def _delta_kernel(q_ref, k_ref, v_ref, z_ref, sm_ref, cw_ref, cb_ref, al_ref, dtb_ref, s0_ref, ng_ref,
                  o_ref, cn_ref, sn_ref, xp_ref, s_ref, *, c_in):
    c = pl.program_id(1)
    C = CHUNK
    hd = DK_A

    @pl.when(c == 0)
    def _init():
        xp_ref[...] = jnp.zeros(xp_ref.shape, F32)
        xp_ref[0:SUBLANES, :] = cb_ref[0]
        s_ref[...] = s0_ref[0]

    xp_ref[SUBLANES:SUBLANES + c_in, 0:W_BR] = q_ref[...]
    xp_ref[SUBLANES:SUBLANES + c_in, W_BR:2 * W_BR] = k_ref[...]
    xp_ref[SUBLANES:SUBLANES + c_in, 2 * W_BR:3 * W_BR] = v_ref[...]

    valid = lax.broadcasted_iota(jnp.int32, (C, 1), 0) < c_in
    sm = sm_ref[...]
    if c_in < C:
        sm = jnp.concatenate([sm, jnp.zeros((C - c_in, sm.shape[1]), F32)], axis=0)
    beta = _sigmoid(sm)
    g = -jnp.exp(al_ref[...]) * _softplus(sm + dtb_ref[...])
    g = jnp.where(valid, g, 0.0)

    ri = lax.broadcasted_iota(jnp.int32, (C, C), 0)
    ci = lax.broadcasted_iota(jnp.int32, (C, C), 1)
    causal = ci <= ri
    strict = ci < ri
    gcum = _dot(causal.astype(F32), g, HI)
    gcum_t = gcum.T
    e_g = jnp.exp(gcum)
    g_last = gcum[C - 1:C, :]
    e_end = jnp.exp(g_last - gcum)
    e_last = jnp.exp(g_last)

    def conv(col):
        sl = slice(col, col + hd)
        y = xp_ref[5:5 + C, sl] * cw_ref[0:1, sl]
        for j in range(1, CONV_W):
            y = y + xp_ref[5 + j:5 + j + C, sl] * cw_ref[j:j + 1, sl]
        return jnp.where(valid, _silu(y), 0.0)

    heads = range(H_A)
    lane_a = [LANE_ALPHA + h for h in heads]
    q_l = [conv(h * hd) for h in heads]
    k_l = [conv(W_BR + h * hd) for h in heads]
    v_l = [conv(2 * W_BR + h * hd) for h in heads]
    q_l = [q * lax.rsqrt(jnp.sum(q * q, axis=-1, keepdims=True) + EPS) * (DK_A ** -0.5) for q in q_l]
    k_l = [k * lax.rsqrt(jnp.sum(k * k, axis=-1, keepdims=True) + EPS) for k in k_l]
    decay_l = [jnp.exp(jnp.where(causal, gcum[:, a:a + 1] - gcum_t[a:a + 1, :], NEG_INF)) for a in lane_a]
    b_l = [beta[:, LANE_BETA + h:LANE_BETA + h + 1] for h in heads]
    kb_l = [k.astype(BF16) for k in k_l]
    qk_kk = [_dot_nt(jnp.concatenate([q_l[h].astype(BF16), kb_l[h]], axis=0), kb_l[h]) for h in heads]
    a_l = [jnp.where(strict, b_l[h] * decay_l[h] * qk_kk[h][C:2 * C], 0.0) for h in heads]
    r_l = [-a for a in a_l]
    pw_l = a_l
    n = 2
    while n < C:
        pw_l = [_bdot(pw, pw) for pw in pw_l]
        r_l = [r_l[h] + pw_l[h] + _bdot(r_l[h], pw_l[h]) for h in heads]
        n *= 2
    rhs_l = [jnp.concatenate([b_l[h] * v_l[h], (b_l[h] * e_g[:, lane_a[h]:lane_a[h] + 1]) * k_l[h]], axis=1)
             for h in heads]
    sol_l = [rhs_l[h] + _bdot(r_l[h], rhs_l[h]) for h in heads]
    qg_l = [q_l[h] * e_g[:, lane_a[h]:lane_a[h] + 1] for h in heads]
    s_l = [s_ref[h] for h in heads]
    kq_s = [_bdot(jnp.concatenate([sol_l[h][:, hd:2 * hd], qg_l[h]], axis=0), s_l[h]) for h in heads]
    wb_l = [(sol_l[h][:, 0:hd] - kq_s[h][0:C]).astype(BF16) for h in heads]
    o_l = [kq_s[h][C:2 * C] + _dot((decay_l[h] * qk_kk[h][0:C]).astype(BF16), wb_l[h]) for h in heads]
    for h in heads:
        kend = (k_l[h] * e_end[:, lane_a[h]:lane_a[h] + 1]).astype(BF16)
        s_ref[h] = e_last[:, lane_a[h]:lane_a[h] + 1] * s_l[h] + _dot_tn(kend, wb_l[h])
    for h in heads:
        o = o_l[h]
        o = o * lax.rsqrt(jnp.mean(o * o, axis=-1, keepdims=True) + EPS) * ng_ref[...]
        o_ref[:, h * hd:(h + 1) * hd] = o[0:c_in] * _silu(z_ref[:, h * hd:(h + 1) * hd])

    tail = xp_ref[c_in:c_in + SUBLANES, :]
    xp_ref[0:SUBLANES, :] = tail

    @pl.when(c == pl.num_programs(1) - 1)
    def _fin():
        cn_ref[0] = tail
        sn_ref[0] = s_ref[...]


def delta_branch(p, ps, conv_w, conv_buf, a_log, dt_bias, s0, st, norm_g, bsz, seq):
    c_in = min(seq, CHUNK)
    n_chunks = seq // c_in
    lane_vec = lambda v: jnp.zeros((1, LANES), F32).at[0, LANE_ALPHA:LANE_ALPHA + H_A].set(v)
    rb = lambda col: pl.BlockSpec((c_in, W_BR), lambda b, c, col=col: (b * n_chunks + c, col))
    const2 = lambda b, c: (0, 0)
    o, cn, sn = pl.pallas_call(
        functools.partial(_delta_kernel, c_in=c_in), grid=(bsz, n_chunks),
        in_specs=[rb(COL_QA), rb(COL_KA), rb(COL_VA), rb(COL_ZA),
                  pl.BlockSpec((c_in, LANES), lambda b, c: (b * n_chunks + c, 0)),
                  pl.BlockSpec((CONV_W, 3 * W_BR), const2),
                  pl.BlockSpec((None, 1, SUBLANES, 3 * W_BR), lambda b, c: (st, b, 0, 0)),
                  pl.BlockSpec((1, LANES), const2), pl.BlockSpec((1, LANES), const2),
                  pl.BlockSpec((None, 1, H_A, DK_A, DK_A), lambda b, c: (st, b, 0, 0, 0)),
                  pl.BlockSpec((1, DK_A), const2)],
        out_specs=[pl.BlockSpec((c_in, W_BR), lambda b, c: (b * n_chunks + c, 0)),
                   pl.BlockSpec((1, SUBLANES, 3 * W_BR), lambda b, c: (b, 0, 0)),
                   pl.BlockSpec((1, H_A, DK_A, DK_A), lambda b, c: (b, 0, 0, 0))],
        out_shape=[jax.ShapeDtypeStruct((bsz * seq, W_BR), F32),
                   jax.ShapeDtypeStruct((bsz, SUBLANES, 3 * W_BR), F32),
                   jax.ShapeDtypeStruct((bsz, H_A, DK_A, DK_A), F32)],
        scratch_shapes=[pltpu.VMEM((SUBLANES + CHUNK, 3 * W_BR), F32), pltpu.VMEM((H_A, DK_A, DK_A), F32)],
        compiler_params=_params("arbitrary", "arbitrary"), name="delta")(
            p, p, p, p, ps, conv_w, conv_buf, lane_vec(a_log), lane_vec(dt_bias), s0, norm_g.reshape(1, DK_A))
    return o, cn[:, SUBLANES - (CONV_W - 1):], sn


def _mlstm_kernel(q_ref, k_ref, v_ref, z_ref, og_ref, sm_ref, bif_ref, c0_ref, n0_ref, m0_ref, ng_ref,
                  o_ref, cn_ref, nn_ref, mn_ref, c_s, n_s, m_s, *, c_in):
    c = pl.program_id(1)
    C = CHUNK
    hd = DK_C

    @pl.when(c == 0)
    def _init():
        c_s[...] = c0_ref[0]
        n_s[...] = n0_ref[0]
        m_s[...] = m0_ref[0]

    def pad_rows(x):
        if c_in == C:
            return x
        return jnp.concatenate([x, jnp.zeros((C - c_in, x.shape[1]), F32)], axis=0)

    valid = lax.broadcasted_iota(jnp.int32, (C, 1), 0) < c_in
    pre = pad_rows(sm_ref[...]) + bif_ref[...]
    i_pre = jnp.where(valid, pre, NEG_INF)
    log_f = jnp.where(valid, -_softplus(-pre), 0.0)

    ri = lax.broadcasted_iota(jnp.int32, (C, C), 0)
    ci = lax.broadcasted_iota(jnp.int32, (C, C), 1)
    causal = ci <= ri
    bcum = _dot(causal.astype(F32), log_f, HI)
    bcum_t = bcum.T
    i_t = i_pre.T
    m_old = m_s[...]
    m_new_vec = m_old
    lane = lax.broadcasted_iota(jnp.int32, m_old.shape, 1)

    part = []
    for h in range(H_C):
        sl = slice(h * hd, (h + 1) * hd)
        q = pad_rows(q_ref[:, sl])
        ks = pad_rows(k_ref[:, sl]) * (DK_C ** -0.5)
        vb = pad_rows(v_ref[:, sl]).astype(BF16)
        b_col = bcum[:, LANE_F + h:LANE_F + h + 1]
        b_row = bcum_t[LANE_F + h:LANE_F + h + 1, :]
        i_col = i_pre[:, LANE_I + h:LANE_I + h + 1]
        i_row = i_t[LANE_I + h:LANE_I + h + 1, :]
        m_prev = m_old[:, h:h + 1]
        inter = b_col + m_prev
        intra = jnp.where(causal, b_col - b_row + i_row, NEG_INF)
        m_t = jnp.maximum(inter, jnp.max(intra, axis=-1, keepdims=True))
        w_inter = jnp.exp(inter - m_t)
        qb = q.astype(BF16)
        s = _dot_nt(qb, ks.astype(BF16)) * jnp.exp(intra - m_t)
        cm = c_s[h]
        n = n_s[h:h + 1, :]
        num_inter = w_inter * _dot(qb, cm.astype(BF16))
        den = w_inter * jnp.sum(q * n, axis=-1, keepdims=True) + jnp.sum(s, axis=-1, keepdims=True)
        m_new = m_t[C - 1:C, :]
        b_last = b_col[C - 1:C, :]
        w_c = jnp.exp(b_last + m_prev - m_new)
        w_j = jnp.exp(b_last - b_col + i_col - m_new)
        kw = w_j * ks
        c_s[h] = w_c * cm + _dot_tn(kw.astype(BF16), vb)
        n_s[h:h + 1, :] = w_c * n + jnp.sum(kw, axis=0, keepdims=True)
        m_new_vec = jnp.where(lane == h, m_new, m_new_vec)
        part.append((s.astype(BF16), vb, num_inter, jnp.maximum(jnp.abs(den), jnp.exp(-m_t))))
    m_s[...] = m_new_vec

    for h, (sb, vb, num_inter, den) in enumerate(part):
        sl = slice(h * hd, (h + 1) * hd)
        hh = (num_inter + _dot(sb, vb)) / den
        hc = _sigmoid(og_ref[:, sl]) * hh[0:c_in]
        hc = hc * lax.rsqrt(jnp.mean(hc * hc, axis=-1, keepdims=True) + EPS) * ng_ref[...]
        o_ref[:, sl] = hc * _silu(z_ref[:, sl])

    @pl.when(c == pl.num_programs(1) - 1)
    def _fin():
        cn_ref[0] = c_s[...]
        nn_ref[0] = n_s[...]
        mn_ref[0] = m_new_vec


def mlstm_branch(p, ps, b_if, c0, n0, m0p, st, norm_g, bsz, seq):
    c_in = min(seq, CHUNK)
    n_chunks = seq // c_in
    bif = jnp.zeros((1, LANES), F32).at[0, LANE_I:LANE_I + 2 * H_C].set(b_if)
    rb = lambda col: pl.BlockSpec((c_in, W_BR), lambda b, c, col=col: (b * n_chunks + c, col))
    const2 = lambda b, c: (0, 0)
    st4 = pl.BlockSpec((1, H_C, DK_C, DK_C), lambda b, c: (b, 0, 0, 0))
    st3 = pl.BlockSpec((1, H_C, DK_C), lambda b, c: (b, 0, 0))
    stm = pl.BlockSpec((1, 1, LANES), lambda b, c: (b, 0, 0))
    in4 = pl.BlockSpec((None, 1, H_C, DK_C, DK_C), lambda b, c: (st, b, 0, 0, 0))
    in3 = pl.BlockSpec((None, 1, H_C, DK_C), lambda b, c: (st, b, 0, 0))
    inm = pl.BlockSpec((None, 1, 1, LANES), lambda b, c: (st, b, 0, 0))
    o, cn, nn, mn = pl.pallas_call(
        functools.partial(_mlstm_kernel, c_in=c_in), grid=(bsz, n_chunks),
        in_specs=[rb(COL_QC), rb(COL_KC), rb(COL_VC), rb(COL_ZC), rb(COL_OC),
                  pl.BlockSpec((c_in, LANES), lambda b, c: (b * n_chunks + c, 1)),
                  pl.BlockSpec((1, LANES), const2), in4, in3, inm, pl.BlockSpec((1, DK_C), const2)],
        out_specs=[pl.BlockSpec((c_in, W_BR), lambda b, c: (b * n_chunks + c, 0)), st4, st3, stm],
        out_shape=[jax.ShapeDtypeStruct((bsz * seq, W_BR), F32),
                   jax.ShapeDtypeStruct((bsz, H_C, DK_C, DK_C), F32),
                   jax.ShapeDtypeStruct((bsz, H_C, DK_C), F32),
                   jax.ShapeDtypeStruct((bsz, 1, LANES), F32)],
        scratch_shapes=[pltpu.VMEM((H_C, DK_C, DK_C), F32), pltpu.VMEM((H_C, DK_C), F32),
                        pltpu.VMEM((1, LANES), F32)],
        compiler_params=_params("arbitrary", "arbitrary"), name="mlstm")(
            p, p, p, p, p, ps, bif, c0, n0, m0p, norm_g.reshape(1, DK_C))
    return o, cn, nn, mn[:, 0, :H_C]


def _head_rms(x, g):
    return x * lax.rsqrt(jnp.mean(x * x, axis=-1, keepdims=True) + EPS) * g


def _qknorm_kernel(q_ref, k_ref, qg_ref, kg_ref, qn_ref, kn_ref):
    for h in range(H_B):
        sl = slice(h * HD_B, (h + 1) * HD_B)
        qn_ref[:, sl] = _head_rms(q_ref[:, sl], qg_ref[...]) * (HD_B ** -0.5)
        kn_ref[:, sl] = _head_rms(k_ref[:, sl], kg_ref[...])


def _qknorm_kv_kernel(q_ref, k_ref, v_ref, qg_ref, kg_ref, qn_ref, kn_ref, knb_ref, vt_ref, kbar_ref):
    for h in range(H_B):
        sl = slice(h * HD_B, (h + 1) * HD_B)
        qn_ref[:, sl] = _head_rms(q_ref[:, sl], qg_ref[...]) * (HD_B ** -0.5)
        kn = _head_rms(k_ref[:, sl], kg_ref[...])
        kn_ref[:, sl] = kn
        knb_ref[0, :, sl] = kn.astype(BF16)
        kbar_ref[0, :, sl] = jnp.mean(kn, axis=0, keepdims=True)
    vt_ref[0] = v_ref[...].T.astype(BF16)


def qk_norm(p, qg, kg, with_kv):
    t = p.shape[0]
    qg = qg.reshape(1, HD_B)
    kg = kg.reshape(1, HD_B)
    gs = pl.BlockSpec((1, HD_B), lambda i: (0, 0))
    if not with_kv:
        tm = _pick_tile(t, (256, 128))
        return pl.pallas_call(
            _qknorm_kernel, grid=(t // tm,),
            in_specs=[pl.BlockSpec((tm, W_BR), lambda i: (i, COL_QB)),
                      pl.BlockSpec((tm, W_BR), lambda i: (i, COL_KB)), gs, gs],
            out_specs=[pl.BlockSpec((tm, W_BR), lambda i: (i, 0))] * 2,
            out_shape=[jax.ShapeDtypeStruct((t, W_BR), F32)] * 2,
            compiler_params=_params("arbitrary"), name="qk_norm")(p, p, qg, kg)
    tm = MOBA_BLOCK
    nb = t // tm
    return pl.pallas_call(
        _qknorm_kv_kernel, grid=(nb,),
        in_specs=[pl.BlockSpec((tm, W_BR), lambda i: (i, COL_QB)),
                  pl.BlockSpec((tm, W_BR), lambda i: (i, COL_KB)),
                  pl.BlockSpec((tm, W_BR), lambda i: (i, COL_VB)), gs, gs],
        out_specs=[pl.BlockSpec((tm, W_BR), lambda i: (i, 0)), pl.BlockSpec((tm, W_BR), lambda i: (i, 0)),
                   pl.BlockSpec((1, tm, W_BR), lambda i: (i, 0, 0)),
                   pl.BlockSpec((1, W_BR, tm), lambda i: (i, 0, 0)),
                   pl.BlockSpec((1, 1, W_BR), lambda i: (i, 0, 0))],
        out_shape=[jax.ShapeDtypeStruct((t, W_BR), F32), jax.ShapeDtypeStruct((t, W_BR), F32),
                   jax.ShapeDtypeStruct((nb, tm, W_BR), BF16), jax.ShapeDtypeStruct((nb, W_BR, tm), BF16),
                   jax.ShapeDtypeStruct((nb, 1, W_BR), F32)],
        compiler_params=_params("arbitrary"), name="qk_norm_kv")(p, p, p, qg, kg)


def _select_topk(sc, valid, axis):
    nb = sc.shape[axis]
    idx = lax.broadcasted_iota(jnp.int32, sc.shape, axis).astype(F32)
    if valid is not None:
        sc = jnp.where(valid, sc, NEG_INF)
    sel = jnp.zeros(sc.shape, F32)
    for _ in range(MOBA_TOPK):
        mx = jnp.max(sc, axis=axis, keepdims=True)
        first = jnp.min(jnp.where(sc == mx, idx, float(nb)), axis=axis, keepdims=True)
        hit = idx == first
        sel = jnp.where(hit & (mx > NEG_INF), 1.0, sel)
        sc = jnp.where(hit, NEG_INF, sc)
    return sel


MOBA_HEADS_PER_STEP = 8


def _moba_prompt_kernel(q_ref, k_ref, vt_ref, kbar_ref, o_ref, sel_ref, qb_ref, acc_ref):
    i = pl.program_id(2)
    blk = MOBA_BLOCK
    hp = MOBA_HEADS_PER_STEP
    nb = kbar_ref.shape[1]
    past = lax.broadcasted_iota(jnp.int32, (nb, blk), 0) < i
    kpos = lax.broadcasted_iota(jnp.int32, (blk, blk), 0)
    qpos = lax.broadcasted_iota(jnp.int32, (blk, blk), 1)
    heads = [slice(hh * HD_B, (hh + 1) * HD_B) for hh in range(hp)]

    ones = jnp.ones((BF16_ROWS, blk), BF16)

    def pv(n, sl, pr):
        return _dot(jnp.concatenate([vt_ref[n, sl, :], ones], axis=0), pr)

    for hh, sl in enumerate(heads):
        qb_ref[hh] = (q_ref[:, sl] * LOG2E).astype(BF16)
    s_l = [_dot_nt(k_ref[i, :, sl], qb_ref[hh]) for hh, sl in enumerate(heads)]
    sc_l = [_dot_nt(kbar_ref[0, :, sl], q_ref[:, sl], HI) for sl in heads]
    ms, pr_l = [], []
    for hh in range(hp):
        s = jnp.where(kpos <= qpos, s_l[hh], NEG_INF)
        m = jnp.max(s, axis=0, keepdims=True)
        ms.append(m)
        pr_l.append(jnp.exp2(s - m).astype(BF16))
    for hh, sl in enumerate(heads):
        acc_ref[hh] = pv(i, sl, pr_l[hh])
        sel_ref[hh] = _select_topk(sc_l[hh], past, 0)

    def body(n, ms):
        s_l = [_dot_nt(k_ref[n, :, sl], qb_ref[hh]) for hh, sl in enumerate(heads)]
        ms_new, pr_l, alpha_l = [], [], []
        for hh in range(hp):
            s = jnp.where(sel_ref[hh, pl.ds(n, 1), :] > 0.0, s_l[hh], NEG_INF)
            m_new = jnp.maximum(ms[hh], jnp.max(s, axis=0, keepdims=True))
            alpha_l.append(jnp.exp2(ms[hh] - m_new))
            ms_new.append(m_new)
            pr_l.append(jnp.exp2(s - m_new).astype(BF16))
        for hh, sl in enumerate(heads):
            acc_ref[hh] = alpha_l[hh] * acc_ref[hh] + pv(n, sl, pr_l[hh])
        return tuple(ms_new)

    lax.fori_loop(0, i, body, tuple(ms))
    for hh, sl in enumerate(heads):
        o_ref[:, sl] = (acc_ref[hh, 0:HD_B, :] / acc_ref[hh, HD_B:HD_B + 1, :]).T


def moba_prompt(qn, knb, vt, kbar, bsz, seq):
    nb = seq // MOBA_BLOCK
    blk = MOBA_BLOCK
    hp = MOBA_HEADS_PER_STEP
    wid = hp * HD_B
    return pl.pallas_call(
        _moba_prompt_kernel, grid=(bsz, H_B // hp, nb),
        in_specs=[pl.BlockSpec((blk, wid), lambda b, h, i: (b * nb + i, h)),
                  pl.BlockSpec((nb, blk, wid), lambda b, h, i: (b, 0, h)),
                  pl.BlockSpec((nb, wid, blk), lambda b, h, i: (b, h, 0)),
                  pl.BlockSpec((1, nb, wid), lambda b, h, i: (b, 0, h))],
        out_specs=pl.BlockSpec((blk, wid), lambda b, h, i: (b * nb + i, h)),
        out_shape=jax.ShapeDtypeStruct((bsz * seq, W_BR), F32),
        scratch_shapes=[pltpu.VMEM((hp, nb, blk), F32), pltpu.VMEM((hp, blk, HD_B), BF16),
                        pltpu.VMEM((hp, HD_B + BF16_ROWS, blk), F32)],
        compiler_params=_params("arbitrary", "arbitrary", "arbitrary"), name="moba_prompt")(qn, knb, vt, kbar)


DECODE_PAGES_PER_STEP = 8


def _moba_decode_kernel(pt_ref, q_ref, qf_ref, bias_ref, obias_ref, kown_ref, vown_ref, *rest, pps, seq):
    del pt_ref
    k_refs = rest[:pps]
    v_refs = rest[pps:2 * pps]
    o_ref = rest[2 * pps]
    m_s, l_s, o_s, ks_s = rest[2 * pps + 1:]
    n = pl.program_id(1)
    q = q_ref[0]
    ppb = MOBA_BLOCK // PAGE_SIZE
    npg = ks_s.shape[0] // H_B
    lane = lax.broadcasted_iota(jnp.int32, m_s.shape, 1)

    def partial_softmax(s):
        m = jnp.max(s, axis=-1, keepdims=True)
        pr = jnp.exp(s - m)
        return m, jnp.sum(pr, axis=-1, keepdims=True), pr

    @pl.when(n == 0)
    def _init():
        m_s[...] = jnp.zeros(m_s.shape, F32)
        l_s[...] = jnp.zeros(l_s.shape, F32)

    m_all = m_s[...]
    l_all = l_s[...]
    s_l = [_dot_nt(q, k_refs[j][0, 0].astype(BF16)) for j in range(pps)]
    pr_l = []
    for j in range(pps):
        m, l, pr = partial_softmax(s_l[j] + bias_ref[...])
        pg = n * pps + j
        m_all = jnp.where(lane == pg, m, m_all)
        l_all = jnp.where(lane == pg, l, l_all)
        pr_l.append(pr.astype(BF16))
    for j in range(pps):
        pg = n * pps + j
        o_s[pg] = _dot(pr_l[j], v_refs[j][0, 0].astype(BF16))
        ks_s[pl.ds(pg * H_B, H_B), :] = jnp.sum(k_refs[j][0, 0].reshape(PAGE_SIZE, H_B, HD_B), axis=0)
    m_s[...] = m_all
    l_s[...] = l_all

    @pl.when(n == pl.num_programs(1) - 1)
    def _combine():
        nb = npg // ppb
        m_o, l_o, pr_o = partial_softmax(_dot_nt(q, kown_ref[0].astype(BF16)) + obias_ref[...])
        o_s[npg] = _dot(pr_o.astype(BF16), vown_ref[0].astype(BF16))
        m_sl = jnp.where(lane == npg, m_o, m_all)
        l_sl = jnp.where(lane == npg, l_o, l_all)
        scp = _dot_nt(qf_ref[0], ks_s[...], HI)
        ch = lax.broadcasted_iota(jnp.int32, scp.shape, 1) % H_B
        rh = lax.broadcasted_iota(jnp.int32, scp.shape, 0) // seq
        scp = jnp.where(ch == rh, scp, 0.0)
        cols_per_blk = ppb * H_B
        gather = (lax.broadcasted_iota(jnp.int32, (npg * H_B, nb), 0) // cols_per_blk
                  == lax.broadcasted_iota(jnp.int32, (npg * H_B, nb), 1)).astype(F32)
        scb = _dot(scp, gather, HI) * (1.0 / MOBA_BLOCK)
        sel = _select_topk(scb, None, 1)
        slot = lax.broadcasted_iota(jnp.int32, (nb, m_s.shape[1]), 1)
        expand = (slot // ppb == lax.broadcasted_iota(jnp.int32, (nb, m_s.shape[1]), 0)) & (slot < npg)
        sel_slot = jnp.where(lane == npg, 1.0, _dot(sel, expand.astype(F32), HI))
        m_sel = jnp.where(sel_slot > 0.0, m_sl, NEG_INF)
        w = sel_slot * jnp.exp(m_sel - jnp.max(m_sel, axis=-1, keepdims=True))
        w = w / jnp.sum(w * l_sl, axis=-1, keepdims=True)
        acc = w[:, 0:1] * o_s[0]
        for pg in range(1, npg + 1):
            acc = acc + w[:, pg:pg + 1] * o_s[pg]
        o_ref[0] = acc


def moba_decode(qn, kn, v, cache_k, cache_v, page_table, layer, bsz, seq):
    ppb = MOBA_BLOCK // PAGE_SIZE
    pps = DECODE_PAGES_PER_STEP
    n_pages = page_table.shape[1]
    nrow = H_B * seq
    assert nrow % SUBLANES == 0 and n_pages // ppb >= MOBA_TOPK and n_pages % ppb == 0
    assert n_pages % pps == 0 and n_pages < LANES
    rows = PAGE_SIZE * H_B
    depth, n_phys = cache_k.shape[:2]
    ck = cache_k.reshape(depth, n_phys, rows, HD_B)
    cv = cache_v.reshape(depth, n_phys, rows, HD_B)
    qc = jnp.transpose(qn.reshape(bsz, seq, H_B, HD_B), (0, 2, 1, 3)).reshape(bsz, nrow, HD_B)
    c_h, c_q = jnp.arange(nrow) // seq, jnp.arange(nrow) % seq
    key_h = jnp.arange(rows) % H_B
    bias = jnp.where(c_h[:, None] == key_h[None, :], 0.0, NEG_INF).astype(F32)
    own = jnp.arange(seq * H_B)
    obias = jnp.where((c_h[:, None] == own[None, :] % H_B) & (own[None, :] // H_B <= c_q[:, None]),
                      0.0, NEG_INF).astype(F32)
    kown = kn.reshape(bsz, seq * H_B, HD_B)
    vown = v.reshape(bsz, seq * H_B, HD_B)

    page = lambda j: pl.BlockSpec((1, 1, rows, HD_B), lambda b, n, pt, j=j: (layer, pt[b, n * pps + j], 0, 0))
    seq3 = lambda shape: pl.BlockSpec((1,) + shape, lambda b, n, pt: (b, 0, 0))
    const2 = lambda shape: pl.BlockSpec(shape, lambda b, n, pt: (0, 0))
    grid_spec = pltpu.PrefetchScalarGridSpec(
        num_scalar_prefetch=1, grid=(bsz, n_pages // pps),
        in_specs=[seq3((nrow, HD_B)), seq3((nrow, HD_B)), const2((nrow, rows)), const2((nrow, seq * H_B)),
                  seq3((seq * H_B, HD_B)), seq3((seq * H_B, HD_B))]
                 + [page(j) for j in range(pps)] + [page(j) for j in range(pps)],
        out_specs=seq3((nrow, HD_B)),
        scratch_shapes=[pltpu.VMEM((nrow, LANES), F32), pltpu.VMEM((nrow, LANES), F32),
                        pltpu.VMEM((n_pages + 1, nrow, HD_B), F32), pltpu.VMEM((n_pages * H_B, HD_B), F32)])
    out = pl.pallas_call(
        functools.partial(_moba_decode_kernel, pps=pps, seq=seq), grid_spec=grid_spec,
        out_shape=jax.ShapeDtypeStruct((bsz, nrow, HD_B), F32),
        compiler_params=_params("arbitrary", "arbitrary"), name="moba_decode")(
            page_table, qc.astype(BF16), qc, bias, obias, kown, vown, *([ck] * pps), *([cv] * pps))
    return jnp.transpose(out.reshape(bsz, H_B, seq, HD_B), (0, 2, 1, 3)).reshape(bsz * seq, W_BR)


def _mixer(x, layer, lw, states, st, bsz, seq, attend):
    (ln_g, w_t, conv_a, a_log, dt_bias, norm_a, qnorm_b, knorm_b, b_if, norm_c, wa, wb, wc, wo) = lw
    conv_buf, s_a, c_c, n_c, m_c = states
    h = rms_cast(x, ln_g)
    p = in_proj(h, w_t, layer)
    ps = small_proj(h, w_t, layer)
    oa, conv_new, s_new = delta_branch(p, ps, conv_a, conv_buf, a_log, dt_bias, s_a, st, norm_a, bsz, seq)
    ob, kn, vb = attend(p, qnorm_b, knorm_b)
    oc, c_new, n_new, m_new = mlstm_branch(p, ps, b_if, c_c, n_c, m_c, st, norm_c, bsz, seq)
    y = out_proj(x, merge(oa, ob, oc, p, wa, wb, wc), wo)
    kv_shape = (bsz, seq, H_B, HD_B)
    return y, (kn.reshape(kv_shape), vb.reshape(kv_shape), conv_new, s_new, c_new, n_new, m_new)


def _pad_states(conv, delta, c, n, m):
    conv = jnp.pad(conv, ((0, 0), (0, 0), (SUBLANES - (CONV_W - 1), 0), (0, 0)))
    m = jnp.pad(m, ((0, 0), (0, 0), (0, LANES - H_C)))[:, :, None, :]
    return conv, delta, c, n, m


def kernel(x_prompt, x_sample, cache_k, cache_v, page_table, state_conv_a, state_delta_a, state_mlstm_c,
           state_mlstm_n, state_mlstm_m, ln_g, w_in, conv_a, a_log, dt_bias, norm_a, qnorm_b, knorm_b,
           b_if, norm_c, w_br_a, w_br_b, w_br_c, w_out):
    bp, sp, d = x_prompt.shape
    bs, ss, _ = x_sample.shape
    depth = w_in.shape[0]
    yp = x_prompt.reshape(bp * sp, d)
    ys = x_sample.reshape(bs * ss, d)
    new_p = [[] for _ in range(7)]
    new_s = [[] for _ in range(7)]
    w_t = jnp.swapaxes(w_in, 1, 2)
    zeros = lambda *s: jnp.zeros(s, F32)
    states_p = _pad_states(zeros(1, bp, CONV_W - 1, 3 * W_BR), zeros(1, bp, H_A, DK_A, DK_A),
                           zeros(1, bp, H_C, DK_C, DK_C), zeros(1, bp, H_C, DK_C), zeros(1, bp, H_C))
    states_s = _pad_states(state_conv_a, state_delta_a, state_mlstm_c, state_mlstm_n, state_mlstm_m)
    for l in range(depth):
        lw = (ln_g[l], w_t, conv_a[l], a_log[l], dt_bias[l], norm_a[l], qnorm_b[l],
              knorm_b[l], b_if[l], norm_c[l], w_br_a[l].astype(BF16), w_br_b[l].astype(BF16),
              w_br_c[l].astype(BF16), w_out[l].astype(BF16))

        def attend_prompt(p, qg, kg):
            qn, kn, knb, vt, kbar = qk_norm(p, qg, kg, with_kv=True)
            ob = moba_prompt(qn, knb, vt, kbar.reshape(bp, sp // MOBA_BLOCK, W_BR), bp, sp)
            return ob, kn, p[:, COL_VB * W_BR:(COL_VB + 1) * W_BR]

        def attend_sample(p, qg, kg, l=l):
            qn, kn = qk_norm(p, qg, kg, with_kv=False)
            vb = p[:, COL_VB * W_BR:(COL_VB + 1) * W_BR]
            return moba_decode(qn, kn, vb, cache_k, cache_v, page_table, l, bs, ss), kn, vb

        yp, st_p = _mixer(yp, l, lw, states_p, 0, bp, sp, attend_prompt)
        ys, st_s = _mixer(ys, l, lw, states_s, l, bs, ss, attend_sample)
        for i in range(7):
            new_p[i].append(st_p[i])
            new_s[i].append(st_s[i])
    outs_p = [jnp.stack(t) for t in new_p]
    outs_s = [jnp.stack(t) for t in new_s]
    return (yp.reshape(bp, sp, d), ys.reshape(bs, ss, d), *outs_p, *outs_s)
```

```python
import functools

import jax
import jax.numpy as jnp
from jax import lax
from jax.experimental import pallas as pl
from jax.experimental.pallas import tpu as pltpu

F32 = jnp.float32
BF16 = jnp.bfloat16
HI = lax.Precision.HIGHEST
EPS = 1e-6
NEG_INF = float("-inf")

H_A, DK_A = 8, 128
H_B, HD_B = 8, 128
H_C, DK_C = 4, 256
CONV_W = 4
MOBA_BLOCK = 256
MOBA_TOPK = 3
PAGE_SIZE = 128
CHUNK = 64
W_BR = 1024
D_MODEL = 2048

LANES = 128
SUBLANES = 8
BF16_ROWS = 16
LOG2E = 1.4426950408889634
VMEM_LIMIT = 48 * 1024 * 1024

COL_QA, COL_KA, COL_VA, COL_ZA = 0, 1, 2, 3
COL_QB, COL_KB, COL_VB, COL_ZB = 4, 5, 6, 7
COL_QC, COL_KC, COL_VC, COL_ZC, COL_OC = 8, 9, 10, 11, 12
COL_GATE = 13
N_BIG = 19 * W_BR
OFF_SMALL_A = 4 * W_BR
OFF_SMALL_C = OFF_SMALL_A + 2 * H_A + 9 * W_BR
OFF_GATE = OFF_SMALL_C + 2 * H_C
WIN_A, WIN_C = OFF_SMALL_A // LANES, OFF_SMALL_C // LANES
LANE_BETA, LANE_ALPHA = 0, H_A
LANE_I = OFF_SMALL_C % LANES
LANE_F = LANE_I + H_C
SHIFT_A, SHIFT_BC, SHIFT_G = 0, 2 * H_A, 2 * H_A + 2 * H_C
FIRST_BC_BLOCK, FIRST_G_BLOCK = COL_QB, COL_GATE


def _params(*sem):
    return pltpu.CompilerParams(dimension_semantics=sem, vmem_limit_bytes=VMEM_LIMIT)


def _sigmoid(x):
    return 1.0 / (1.0 + jnp.exp(-x))


def _silu(x):
    return x * _sigmoid(x)


def _softplus(x):
    return jnp.maximum(x, 0.0) + jnp.log(1.0 + jnp.exp(-jnp.abs(x)))


def _dot(a, b, precision=None):
    return jnp.dot(a, b, precision=precision, preferred_element_type=F32)


def _dot_nt(a, b, precision=None):
    return lax.dot_general(a, b, (((1,), (1,)), ((), ())), precision=precision,
                           preferred_element_type=F32)


def _dot_tn(a, b, precision=None):
    return lax.dot_general(a, b, (((0,), (0,)), ((), ())), precision=precision,
                           preferred_element_type=F32)


def _cumsum_rows(x):
    rid = lax.broadcasted_iota(jnp.int32, x.shape, 0)
    d = 1
    while d < x.shape[0]:
        x = x + jnp.where(rid >= d, pltpu.roll(x, d, 0), 0.0)
        d *= 2
    return x


def _pick_tile(n, candidates):
    for c in candidates:
        if n % c == 0:
            return c
    return n


def _rms_kernel(x_ref, g_ref, o_ref):
    x = x_ref[...]
    y = x * lax.rsqrt(jnp.mean(x * x, axis=-1, keepdims=True) + EPS)
    o_ref[...] = (y * g_ref[...]).astype(o_ref.dtype)


def rms_cast(x, g):
    t, d = x.shape
    tm = _pick_tile(t, (512, 256, 128))
    return pl.pallas_call(
        _rms_kernel, grid=(t // tm,),
        in_specs=[pl.BlockSpec((tm, d), lambda i: (i, 0)), pl.BlockSpec((1, d), lambda i: (0, 0))],
        out_specs=pl.BlockSpec((tm, d), lambda i: (i, 0)),
        out_shape=jax.ShapeDtypeStruct((t, d), BF16),
        compiler_params=_params("arbitrary"), name="rms_cast")(x, g.reshape(1, d))


def _in_proj_kernel(a_ref, w_ref, o_ref, wb_s):
    @pl.when(pl.program_id(1) == 0)
    def _cast():
        wb_s[...] = w_ref[...].astype(BF16)

    o_ref[...] = _dot_nt(a_ref[...], wb_s[...])


def in_proj(h, w_t, layer):
    t, k = h.shape
    tm = _pick_tile(t, (1024, 768, 512, 256, 128))
    tn = W_BR

    def w_rows(j, i):
        shift = jnp.where(j < FIRST_BC_BLOCK, SHIFT_A // SUBLANES,
                          jnp.where(j < FIRST_G_BLOCK, SHIFT_BC // SUBLANES, SHIFT_G // SUBLANES))
        return layer, (j * (tn // SUBLANES) + shift) * SUBLANES, 0

    return pl.pallas_call(
        _in_proj_kernel, grid=(N_BIG // tn, t // tm),
        in_specs=[pl.BlockSpec((tm, k), lambda j, i: (i, 0)),
                  pl.BlockSpec((None, pl.Element(tn), pl.Element(k)), w_rows)],
        out_specs=pl.BlockSpec((tm, tn), lambda j, i: (i, j)),
        out_shape=jax.ShapeDtypeStruct((t, N_BIG), F32),
        scratch_shapes=[pltpu.VMEM((tn, k), BF16)],
        compiler_params=_params("arbitrary", "arbitrary"), name="in_proj")(h, w_t)


def _small_proj_kernel(a_ref, wa_ref, wc_ref, o_ref):
    a = a_ref[...]
    o_ref[:, 0:LANES] = _dot_nt(a, wa_ref[0].astype(BF16))
    o_ref[:, LANES:2 * LANES] = _dot_nt(a, wc_ref[0].astype(BF16))


def small_proj(h, w_t, layer):
    t, k = h.shape
    tm = _pick_tile(t, (1024, 512, 256, 128))
    win = lambda c: pl.BlockSpec((1, LANES, k), lambda i, c=c: (layer, c, 0))
    return pl.pallas_call(
        _small_proj_kernel, grid=(t // tm,),
        in_specs=[pl.BlockSpec((tm, k), lambda i: (i, 0)), win(WIN_A), win(WIN_C)],
        out_specs=pl.BlockSpec((tm, 2 * LANES), lambda i: (i, 0)),
        out_shape=jax.ShapeDtypeStruct((t, 2 * LANES), F32),
        compiler_params=_params("arbitrary"), name="small_proj")(h, w_t, w_t)


def _merge_kernel(oa_ref, ob_ref, oc_ref, zb_ref, ga_ref, gb_ref, gc_ref, wa_ref, wb_ref, wc_ref, o_ref):
    ob = ob_ref[...] * _silu(zb_ref[...])
    m = _sigmoid(ga_ref[...]) * _dot(oa_ref[...].astype(BF16), wa_ref[...])
    m += _sigmoid(gb_ref[...]) * _dot(ob.astype(BF16), wb_ref[...])
    m += _sigmoid(gc_ref[...]) * _dot(oc_ref[...].astype(BF16), wc_ref[...])
    o_ref[...] = m.astype(o_ref.dtype)


def merge(oa, ob, oc, p, wa, wb, wc):
    t = oa.shape[0]
    tm = _pick_tile(t, (512, 256, 128))
    tn = W_BR
    row = lambda j, i: (i, 0)
    return pl.pallas_call(
        _merge_kernel, grid=(D_MODEL // tn, t // tm),
        in_specs=[pl.BlockSpec((tm, W_BR), row), pl.BlockSpec((tm, W_BR), row), pl.BlockSpec((tm, W_BR), row),
                  pl.BlockSpec((tm, W_BR), lambda j, i: (i, COL_ZB)),
                  pl.BlockSpec((tm, tn), lambda j, i: (i, COL_GATE + j)),
                  pl.BlockSpec((tm, tn), lambda j, i: (i, COL_GATE + 2 + j)),
                  pl.BlockSpec((tm, tn), lambda j, i: (i, COL_GATE + 4 + j)),
                  pl.BlockSpec((W_BR, tn), lambda j, i: (0, j)),
                  pl.BlockSpec((W_BR, tn), lambda j, i: (0, j)),
                  pl.BlockSpec((W_BR, tn), lambda j, i: (0, j))],
        out_specs=pl.BlockSpec((tm, tn), lambda j, i: (i, j)),
        out_shape=jax.ShapeDtypeStruct((t, D_MODEL), BF16),
        compiler_params=_params("arbitrary", "arbitrary"), name="merge")(oa, ob, oc, p, p, p, p, wa, wb, wc)


def _outproj_kernel(x_ref, m_ref, w_ref, o_ref):
    o_ref[...] = x_ref[...] + _dot(m_ref[...], w_ref[...])


def out_proj(x, m, w):
    t, d = x.shape
    tm = _pick_tile(t, (512, 256, 128))
    tn = 1024
    return pl.pallas_call(
        _outproj_kernel, grid=(d // tn, t // tm),
        in_specs=[pl.BlockSpec((tm, tn), lambda j, i: (i, j)), pl.BlockSpec((tm, d), lambda j, i: (i, 0)),
                  pl.BlockSpec((d, tn), lambda j, i: (0, j))],
        out_specs=pl.BlockSpec((tm, tn), lambda j, i: (i, j)),
        out_shape=jax.ShapeDtypeStruct((t, d), F32),
        compiler_params=_params("arbitrary", "arbitrary"), name="out_proj")(x, m, w)


def _bdot(a, b):
    return _dot(a.astype(BF16), b.astype(BF16))


DELTA_CHUNKS_PER_STEP = 4


def _delta_kernel(q_ref, k_ref, v_ref, z_ref, sm_ref, cw_ref, cb_ref, al_ref, dtb_ref, s0_ref, ng_ref,
                  o_ref, cn_ref, sn_ref, xp_ref, s_ref, *, c_in):
    c = pl.program_id(1)
    C = CHUNK
    hd = DK_A

    @pl.when(c == 0)
    def _init():
        xp_ref[...] = jnp.zeros(xp_ref.shape, F32)
        xp_ref[0:SUBLANES, :] = cb_ref[0]
        s_ref[...] = s0_ref[0]

    xp_ref[SUBLANES:SUBLANES + c_in, 0:W_BR] = q_ref[...]
    xp_ref[SUBLANES:SUBLANES + c_in, W_BR:2 * W_BR] = k_ref[...]
    xp_ref[SUBLANES:SUBLANES + c_in, 2 * W_BR:3 * W_BR] = v_ref[...]

    nch = max(c_in // C, 1)
    padded = c_in < C
    valid = lax.broadcasted_iota(jnp.int32, (C, 1), 0) < c_in
    mask = (lambda x, fill: jnp.where(valid, x, fill)) if padded else (lambda x, fill: x)
    ri = lax.broadcasted_iota(jnp.int32, (C, C), 0)
    ci = lax.broadcasted_iota(jnp.int32, (C, C), 1)
    causal = ci <= ri
    strict = ci < ri

    beta_c, gcum_c, gcum_t_c, e_g_c, e_end_c, e_last_c = [], [], [], [], [], []
    for kc in range(nch):
        if padded:
            sm = jnp.concatenate([sm_ref[...], jnp.zeros((C - c_in, LANES), F32)], axis=0)
        else:
            sm = sm_ref[kc * C:(kc + 1) * C, :]
        g = mask(-jnp.exp(al_ref[...]) * _softplus(sm + dtb_ref[...]), 0.0)
        gcum = _cumsum_rows(g)
        g_last = gcum[C - 1:C, :]
        beta_c.append(_sigmoid(sm))
        gcum_c.append(gcum)
        gcum_t_c.append(gcum.T)
        e_g_c.append(jnp.exp(gcum))
        e_end_c.append(jnp.exp(g_last - gcum))
        e_last_c.append(jnp.exp(g_last))

    def conv(kc, col):
        sl = slice(col, col + hd)
        r0 = 5 + kc * C
        y = xp_ref[r0:r0 + C, sl] * cw_ref[0:1, sl]
        for j in range(1, CONV_W):
            y = y + xp_ref[r0 + j:r0 + j + C, sl] * cw_ref[j:j + 1, sl]
        return mask(_silu(y), 0.0)

    heads = range(H_A)
    pairs = [(kc, h) for kc in range(nch) for h in heads]
    npair = range(len(pairs))
    col = lambda vec, kc, lane: vec[kc][:, lane:lane + 1]
    q_l = [conv(kc, h * hd) for kc, h in pairs]
    k_l = [conv(kc, W_BR + h * hd) for kc, h in pairs]
    v_l = [conv(kc, 2 * W_BR + h * hd) for kc, h in pairs]
    q_l = [q * lax.rsqrt(jnp.sum(q * q, axis=-1, keepdims=True) + EPS) * (DK_A ** -0.5) for q in q_l]
    k_l = [k * lax.rsqrt(jnp.sum(k * k, axis=-1, keepdims=True) + EPS) for k in k_l]
    decay_l = [jnp.exp(jnp.where(causal, col(gcum_c, kc, LANE_ALPHA + h)
                                 - gcum_t_c[kc][LANE_ALPHA + h:LANE_ALPHA + h + 1, :], NEG_INF))
               for kc, h in pairs]
    b_l = [col(beta_c, kc, LANE_BETA + h) for kc, h in pairs]
    eg_l = [col(e_g_c, kc, LANE_ALPHA + h) for kc, h in pairs]
    kb_l = [k.astype(BF16) for k in k_l]
    qk_kk = [_dot_nt(jnp.concatenate([q_l[i].astype(BF16), kb_l[i]], axis=0), kb_l[i]) for i in npair]
    a_l = [jnp.where(strict, b_l[i] * decay_l[i] * qk_kk[i][C:2 * C], 0.0) for i in npair]
    r_l = [-a for a in a_l]
    pw_l = a_l
    n = 2
    while n < C:
        pw_l = [_bdot(pw, pw) for pw in pw_l]
        r_l = [r_l[i] + pw_l[i] + _bdot(r_l[i], pw_l[i]) for i in npair]
        n *= 2
    rhs_l = [jnp.concatenate([b_l[i] * v_l[i], (b_l[i] * eg_l[i]) * k_l[i]], axis=1) for i in npair]
    sol_l = [rhs_l[i] + _bdot(r_l[i], rhs_l[i]) for i in npair]
    kq_l = [jnp.concatenate([sol_l[i][:, hd:2 * hd], q_l[i] * eg_l[i]], axis=0).astype(BF16) for i in npair]
    p_l = [(decay_l[i] * qk_kk[i][0:C]).astype(BF16) for i in npair]
    kend_l = [(k_l[i] * col(e_end_c, kc, LANE_ALPHA + h)).astype(BF16) for i, (kc, h) in enumerate(pairs)]

    s_l = [s_ref[h] for h in heads]
    o_l = []
    for kc in range(nch):
        idx = [kc * H_A + h for h in heads]
        kq_s = [_dot(kq_l[i], s_l[h].astype(BF16)) for h, i in enumerate(idx)]
        wb_l = [(sol_l[i][:, 0:hd] - kq_s[h][0:C]).astype(BF16) for h, i in enumerate(idx)]
        o_l += [kq_s[h][C:2 * C] + _dot(p_l[i], wb_l[h]) for h, i in enumerate(idx)]
        s_l = [col(e_last_c, kc, LANE_ALPHA + h) * s_l[h] + _dot_tn(kend_l[i], wb_l[h])
               for h, i in enumerate(idx)]
    for h in heads:
        s_ref[h] = s_l[h]
    for i, (kc, h) in enumerate(pairs):
        o = o_l[i]
        o = o * lax.rsqrt(jnp.mean(o * o, axis=-1, keepdims=True) + EPS) * ng_ref[...]
        rows = slice(0, c_in) if padded else slice(kc * C, (kc + 1) * C)
        o_ref[rows, h * hd:(h + 1) * hd] = o[0:min(c_in, C)] * _silu(z_ref[rows, h * hd:(h + 1) * hd])

    tail = xp_ref[c_in:c_in + SUBLANES, :]
    xp_ref[0:SUBLANES, :] = tail

    @pl.when(c == pl.num_programs(1) - 1)
    def _fin():
        cn_ref[0] = tail
        sn_ref[0] = s_ref[...]


def delta_branch(p, ps, conv_w, conv_buf, a_log, dt_bias, s0, st, norm_g, bsz, seq):
    c_in = min(seq, DELTA_CHUNKS_PER_STEP * CHUNK)
    assert seq % c_in == 0 and (c_in % CHUNK == 0 or c_in < CHUNK)
    n_chunks = seq // c_in
    lane_vec = lambda v: jnp.zeros((1, LANES), F32).at[0, LANE_ALPHA:LANE_ALPHA + H_A].set(v)
    rb = lambda col: pl.BlockSpec((c_in, W_BR), lambda b, c, col=col: (b * n_chunks + c, col))
    const2 = lambda b, c: (0, 0)
    o, cn, sn = pl.pallas_call(
        functools.partial(_delta_kernel, c_in=c_in), grid=(bsz, n_chunks),
        in_specs=[rb(COL_QA), rb(COL_KA), rb(COL_VA), rb(COL_ZA),
                  pl.BlockSpec((c_in, LANES), lambda b, c: (b * n_chunks + c, 0)),
                  pl.BlockSpec((CONV_W, 3 * W_BR), const2),
                  pl.BlockSpec((None, 1, SUBLANES, 3 * W_BR), lambda b, c: (st, b, 0, 0)),
                  pl.BlockSpec((1, LANES), const2), pl.BlockSpec((1, LANES), const2),
                  pl.BlockSpec((None, 1, H_A, DK_A, DK_A), lambda b, c: (st, b, 0, 0, 0)),
                  pl.BlockSpec((1, DK_A), const2)],
        out_specs=[pl.BlockSpec((c_in, W_BR), lambda b, c: (b * n_chunks + c, 0)),
                   pl.BlockSpec((1, SUBLANES, 3 * W_BR), lambda b, c: (b, 0, 0)),
                   pl.BlockSpec((1, H_A, DK_A, DK_A), lambda b, c: (b, 0, 0, 0))],
        out_shape=[jax.ShapeDtypeStruct((bsz * seq, W_BR), F32),
                   jax.ShapeDtypeStruct((bsz, SUBLANES, 3 * W_BR), F32),
                   jax.ShapeDtypeStruct((bsz, H_A, DK_A, DK_A), F32)],
        scratch_shapes=[pltpu.VMEM((SUBLANES + max(c_in, CHUNK), 3 * W_BR), F32),
                        pltpu.VMEM((H_A, DK_A, DK_A), F32)],
        compiler_params=_params("arbitrary", "arbitrary"), name="delta")(
            p, p, p, p, ps, conv_w, conv_buf, lane_vec(a_log), lane_vec(dt_bias), s0, norm_g.reshape(1, DK_A))
    return o, cn[:, SUBLANES - (CONV_W - 1):], sn


def _mlstm_kernel(q_ref, k_ref, v_ref, z_ref, og_ref, sm_ref, bif_ref, c0_ref, n0_ref, m0_ref, ng_ref,
                  o_ref, cn_ref, nn_ref, mn_ref, c_s, n_s, m_s, *, c_in):
    c = pl.program_id(1)
    C = CHUNK
    hd = DK_C

    @pl.when(c == 0)
    def _init():
        c_s[...] = c0_ref[0]
        n_s[...] = n0_ref[0]
        m_s[...] = m0_ref[0]

    def pad_rows(x):
        if c_in == C:
            return x
        return jnp.concatenate([x, jnp.zeros((C - c_in, x.shape[1]), F32)], axis=0)

    valid = lax.broadcasted_iota(jnp.int32, (C, 1), 0) < c_in
    pre = pad_rows(sm_ref[...]) + bif_ref[...]
    i_pre = jnp.where(valid, pre, NEG_INF)
    log_f = jnp.where(valid, -_softplus(-pre), 0.0)

    ri = lax.broadcasted_iota(jnp.int32, (C, C), 0)
    ci = lax.broadcasted_iota(jnp.int32, (C, C), 1)
    causal = ci <= ri
    bcum = _dot(causal.astype(F32), log_f, HI)
    bcum_t = bcum.T
    i_t = i_pre.T
    m_old = m_s[...]
    m_new_vec = m_old
    lane = lax.broadcasted_iota(jnp.int32, m_old.shape, 1)

    part = []
    for h in range(H_C):
        sl = slice(h * hd, (h + 1) * hd)
        q = pad_rows(q_ref[:, sl])
        ks = pad_rows(k_ref[:, sl]) * (DK_C ** -0.5)
        vb = pad_rows(v_ref[:, sl]).astype(BF16)
        b_col = bcum[:, LANE_F + h:LANE_F + h + 1]
        b_row = bcum_t[LANE_F + h:LANE_F + h + 1, :]
        i_col = i_pre[:, LANE_I + h:LANE_I + h + 1]
        i_row = i_t[LANE_I + h:LANE_I + h + 1, :]
        m_prev = m_old[:, h:h + 1]
        inter = b_col + m_prev
        intra = jnp.where(causal, b_col - b_row + i_row, NEG_INF)
        m_t = jnp.maximum(inter, jnp.max(intra, axis=-1, keepdims=True))
        w_inter = jnp.exp(inter - m_t)
        qb = q.astype(BF16)
        s = _dot_nt(qb, ks.astype(BF16)) * jnp.exp(intra - m_t)
        cm = c_s[h]
        n = n_s[h:h + 1, :]
        num_inter = w_inter * _dot(qb, cm.astype(BF16))
        den = w_inter * jnp.sum(q * n, axis=-1, keepdims=True) + jnp.sum(s, axis=-1, keepdims=True)
        m_new = m_t[C - 1:C, :]
        b_last = b_col[C - 1:C, :]
        w_c = jnp.exp(b_last + m_prev - m_new)
        w_j = jnp.exp(b_last - b_col + i_col - m_new)
        kw = w_j * ks
        c_s[h] = w_c * cm + _dot_tn(kw.astype(BF16), vb)
        n_s[h:h + 1, :] = w_c * n + jnp.sum(kw, axis=0, keepdims=True)
        m_new_vec = jnp.where(lane == h, m_new, m_new_vec)
        part.append((s.astype(BF16), vb, num_inter, jnp.maximum(jnp.abs(den), jnp.exp(-m_t))))
    m_s[...] = m_new_vec

    for h, (sb, vb, num_inter, den) in enumerate(part):
        sl = slice(h * hd, (h + 1) * hd)
        hh = (num_inter + _dot(sb, vb)) / den
        hc = _sigmoid(og_ref[:, sl]) * hh[0:c_in]
        hc = hc * lax.rsqrt(jnp.mean(hc * hc, axis=-1, keepdims=True) + EPS) * ng_ref[...]
        o_ref[:, sl] = hc * _silu(z_ref[:, sl])

    @pl.when(c == pl.num_programs(1) - 1)
    def _fin():
        cn_ref[0] = c_s[...]
        nn_ref[0] = n_s[...]
        mn_ref[0] = m_new_vec


def mlstm_branch(p, ps, b_if, c0, n0, m0p, st, norm_g, bsz, seq):
    c_in = min(seq, CHUNK)
    n_chunks = seq // c_in
    bif = jnp.zeros((1, LANES), F32).at[0, LANE_I:LANE_I + 2 * H_C].set(b_if)
    rb = lambda col: pl.BlockSpec((c_in, W_BR), lambda b, c, col=col: (b * n_chunks + c, col))
    const2 = lambda b, c: (0, 0)
    st4 = pl.BlockSpec((1, H_C, DK_C, DK_C), lambda b, c: (b, 0, 0, 0))
    st3 = pl.BlockSpec((1, H_C, DK_C), lambda b, c: (b, 0, 0))
    stm = pl.BlockSpec((1, 1, LANES), lambda b, c: (b, 0, 0))
    in4 = pl.BlockSpec((None, 1, H_C, DK_C, DK_C), lambda b, c: (st, b, 0, 0, 0))
    in3 = pl.BlockSpec((None, 1, H_C, DK_C), lambda b, c: (st, b, 0, 0))
    inm = pl.BlockSpec((None, 1, 1, LANES), lambda b, c: (st, b, 0, 0))
    o, cn, nn, mn = pl.pallas_call(
        functools.partial(_mlstm_kernel, c_in=c_in), grid=(bsz, n_chunks),
        in_specs=[rb(COL_QC), rb(COL_KC), rb(COL_VC), rb(COL_ZC), rb(COL_OC),
                  pl.BlockSpec((c_in, LANES), lambda b, c: (b * n_chunks + c, 1)),
                  pl.BlockSpec((1, LANES), const2), in4, in3, inm, pl.BlockSpec((1, DK_C), const2)],
        out_specs=[pl.BlockSpec((c_in, W_BR), lambda b, c: (b * n_chunks + c, 0)), st4, st3, stm],
        out_shape=[jax.ShapeDtypeStruct((bsz * seq, W_BR), F32),
                   jax.ShapeDtypeStruct((bsz, H_C, DK_C, DK_C), F32),
                   jax.ShapeDtypeStruct((bsz, H_C, DK_C), F32),
                   jax.ShapeDtypeStruct((bsz, 1, LANES), F32)],
        scratch_shapes=[pltpu.VMEM((H_C, DK_C, DK_C), F32), pltpu.VMEM((H_C, DK_C), F32),
                        pltpu.VMEM((1, LANES), F32)],
        compiler_params=_params("arbitrary", "arbitrary"), name="mlstm")(
            p, p, p, p, p, ps, bif, c0, n0, m0p, norm_g.reshape(1, DK_C))
    return o, cn, nn, mn[:, 0, :H_C]


def _head_rms(x, g):
    return x * lax.rsqrt(jnp.mean(x * x, axis=-1, keepdims=True) + EPS) * g


def _qknorm_kernel(q_ref, k_ref, qg_ref, kg_ref, qn_ref, kn_ref):
    for h in range(H_B):
        sl = slice(h * HD_B, (h + 1) * HD_B)
        qn_ref[:, sl] = _head_rms(q_ref[:, sl], qg_ref[...]) * (HD_B ** -0.5)
        kn_ref[:, sl] = _head_rms(k_ref[:, sl], kg_ref[...])


def _qknorm_kv_kernel(q_ref, k_ref, v_ref, qg_ref, kg_ref, qn_ref, kn_ref, knb_ref, vt_ref, kbar_ref):
    for h in range(H_B):
        sl = slice(h * HD_B, (h + 1) * HD_B)
        qn_ref[:, sl] = _head_rms(q_ref[:, sl], qg_ref[...]) * (HD_B ** -0.5)
        kn = _head_rms(k_ref[:, sl], kg_ref[...])
        kn_ref[:, sl] = kn
        knb_ref[0, :, sl] = kn.astype(BF16)
        kbar_ref[0, :, sl] = jnp.mean(kn, axis=0, keepdims=True)
    vt_ref[0] = v_ref[...].T.astype(BF16)


def qk_norm(p, qg, kg, with_kv):
    t = p.shape[0]
    qg = qg.reshape(1, HD_B)
    kg = kg.reshape(1, HD_B)
    gs = pl.BlockSpec((1, HD_B), lambda i: (0, 0))
    if not with_kv:
        tm = _pick_tile(t, (256, 128))
        return pl.pallas_call(
            _qknorm_kernel, grid=(t // tm,),
            in_specs=[pl.BlockSpec((tm, W_BR), lambda i: (i, COL_QB)),
                      pl.BlockSpec((tm, W_BR), lambda i: (i, COL_KB)), gs, gs],
            out_specs=[pl.BlockSpec((tm, W_BR), lambda i: (i, 0))] * 2,
            out_shape=[jax.ShapeDtypeStruct((t, W_BR), F32)] * 2,
            compiler_params=_params("arbitrary"), name="qk_norm")(p, p, qg, kg)
    tm = MOBA_BLOCK
    nb = t // tm
    return pl.pallas_call(
        _qknorm_kv_kernel, grid=(nb,),
        in_specs=[pl.BlockSpec((tm, W_BR), lambda i: (i, COL_QB)),
                  pl.BlockSpec((tm, W_BR), lambda i: (i, COL_KB)),
                  pl.BlockSpec((tm, W_BR), lambda i: (i, COL_VB)), gs, gs],
        out_specs=[pl.BlockSpec((tm, W_BR), lambda i: (i, 0)), pl.BlockSpec((tm, W_BR), lambda i: (i, 0)),
                   pl.BlockSpec((1, tm, W_BR), lambda i: (i, 0, 0)),
                   pl.BlockSpec((1, W_BR, tm), lambda i: (i, 0, 0)),
                   pl.BlockSpec((1, 1, W_BR), lambda i: (i, 0, 0))],
        out_shape=[jax.ShapeDtypeStruct((t, W_BR), F32), jax.ShapeDtypeStruct((t, W_BR), F32),
                   jax.ShapeDtypeStruct((nb, tm, W_BR), BF16), jax.ShapeDtypeStruct((nb, W_BR, tm), BF16),
                   jax.ShapeDtypeStruct((nb, 1, W_BR), F32)],
        compiler_params=_params("arbitrary"), name="qk_norm_kv")(p, p, p, qg, kg)


def _select_topk(sc, valid, axis):
    nb = sc.shape[axis]
    idx = lax.broadcasted_iota(jnp.int32, sc.shape, axis).astype(F32)
    if valid is not None:
        sc = jnp.where(valid, sc, NEG_INF)
    sel = jnp.zeros(sc.shape, F32)
    for _ in range(MOBA_TOPK):
        mx = jnp.max(sc, axis=axis, keepdims=True)
        first = jnp.min(jnp.where(sc == mx, idx, float(nb)), axis=axis, keepdims=True)
        hit = idx == first
        sel = jnp.where(hit & (mx > NEG_INF), 1.0, sel)
        sc = jnp.where(hit, NEG_INF, sc)
    return sel


MOBA_HEADS_PER_STEP = 8


def _moba_prompt_kernel(q_ref, k_ref, vt_ref, kbar_ref, o_ref, sel_ref, qb_ref, acc_ref):
    i = pl.program_id(2)
    blk = MOBA_BLOCK
    hp = MOBA_HEADS_PER_STEP
    nb = kbar_ref.shape[1]
    past = lax.broadcasted_iota(jnp.int32, (nb, blk), 0) < i
    kpos = lax.broadcasted_iota(jnp.int32, (blk, blk), 0)
    qpos = lax.broadcasted_iota(jnp.int32, (blk, blk), 1)
    heads = [slice(hh * HD_B, (hh + 1) * HD_B) for hh in range(hp)]

    ones = jnp.ones((BF16_ROWS, blk), BF16)

    def pv(n, sl, pr):
        return _dot(jnp.concatenate([vt_ref[n, sl, :], ones], axis=0), pr)

    for hh, sl in enumerate(heads):
        qb_ref[hh] = (q_ref[:, sl] * LOG2E).astype(BF16)
    s_l = [_dot_nt(k_ref[i, :, sl], qb_ref[hh]) for hh, sl in enumerate(heads)]
    sc_l = [_dot_nt(kbar_ref[0, :, sl], q_ref[:, sl], HI) for sl in heads]
    ms, pr_l = [], []
    for hh in range(hp):
        s = jnp.where(kpos <= qpos, s_l[hh], NEG_INF)
        m = jnp.max(s, axis=0, keepdims=True)
        ms.append(m)
        pr_l.append(jnp.exp2(s - m).astype(BF16))
    for hh, sl in enumerate(heads):
        acc_ref[hh] = pv(i, sl, pr_l[hh])
        sel_ref[hh] = _select_topk(sc_l[hh], past, 0)

    def body(n, ms):
        s_l = [_dot_nt(k_ref[n, :, sl], qb_ref[hh]) for hh, sl in enumerate(heads)]
        ms_new, pr_l, alpha_l = [], [], []
        for hh in range(hp):
            s = jnp.where(sel_ref[hh, pl.ds(n, 1), :] > 0.0, s_l[hh], NEG_INF)
            m_new = jnp.maximum(ms[hh], jnp.max(s, axis=0, keepdims=True))
            alpha_l.append(jnp.exp2(ms[hh] - m_new))
            ms_new.append(m_new)
            pr_l.append(jnp.exp2(s - m_new).astype(BF16))
        for hh, sl in enumerate(heads):
            acc_ref[hh] = alpha_l[hh] * acc_ref[hh] + pv(n, sl, pr_l[hh])
        return tuple(ms_new)

    lax.fori_loop(0, i, body, tuple(ms))
    for hh, sl in enumerate(heads):
        o_ref[:, sl] = (acc_ref[hh, 0:HD_B, :] / acc_ref[hh, HD_B:HD_B + 1, :]).T


def moba_prompt(qn, knb, vt, kbar, bsz, seq):
    nb = seq // MOBA_BLOCK
    blk = MOBA_BLOCK
    hp = MOBA_HEADS_PER_STEP
    wid = hp * HD_B
    return pl.pallas_call(
        _moba_prompt_kernel, grid=(bsz, H_B // hp, nb),
        in_specs=[pl.BlockSpec((blk, wid), lambda b, h, i: (b * nb + i, h)),
                  pl.BlockSpec((nb, blk, wid), lambda b, h, i: (b, 0, h)),
                  pl.BlockSpec((nb, wid, blk), lambda b, h, i: (b, h, 0)),
                  pl.BlockSpec((1, nb, wid), lambda b, h, i: (b, 0, h))],
        out_specs=pl.BlockSpec((blk, wid), lambda b, h, i: (b * nb + i, h)),
        out_shape=jax.ShapeDtypeStruct((bsz * seq, W_BR), F32),
        scratch_shapes=[pltpu.VMEM((hp, nb, blk), F32), pltpu.VMEM((hp, blk, HD_B), BF16),
                        pltpu.VMEM((hp, HD_B + BF16_ROWS, blk), F32)],
        compiler_params=_params("arbitrary", "arbitrary", "arbitrary"), name="moba_prompt")(qn, knb, vt, kbar)


DECODE_PAGES_PER_STEP = 16


def _moba_decode_kernel(pt_ref, q_ref, qf_ref, bias_ref, obias_ref, kown_ref, vown_ref, *rest, pps, seq):
    del pt_ref
    k_refs = rest[:pps]
    v_refs = rest[pps:2 * pps]
    o_ref = rest[2 * pps]
    m_s, l_s, o_s, ks_s = rest[2 * pps + 1:]
    n = pl.program_id(1)
    q = q_ref[0]
    ppb = MOBA_BLOCK // PAGE_SIZE
    npg = ks_s.shape[0] // H_B
    lane = lax.broadcasted_iota(jnp.int32, m_s.shape, 1)

    def partial_softmax(s):
        m = jnp.max(s, axis=-1, keepdims=True)
        pr = jnp.exp(s - m)
        return m, jnp.sum(pr, axis=-1, keepdims=True), pr

    @pl.when(n == 0)
    def _init():
        m_s[...] = jnp.zeros(m_s.shape, F32)
        l_s[...] = jnp.zeros(l_s.shape, F32)

    m_all = m_s[...]
    l_all = l_s[...]
    s_l = [_dot_nt(q, k_refs[j][0, 0].astype(BF16)) for j in range(pps)]
    pr_l = []
    for j in range(pps):
        m, l, pr = partial_softmax(s_l[j] + bias_ref[...])
        pg = n * pps + j
        m_all = jnp.where(lane == pg, m, m_all)
        l_all = jnp.where(lane == pg, l, l_all)
        pr_l.append(pr.astype(BF16))
    for j in range(pps):
        pg = n * pps + j
        o_s[pg] = _dot(pr_l[j], v_refs[j][0, 0].astype(BF16))
        ks_s[pl.ds(pg * H_B, H_B), :] = jnp.sum(k_refs[j][0, 0].reshape(PAGE_SIZE, H_B, HD_B), axis=0)
    m_s[...] = m_all
    l_s[...] = l_all

    @pl.when(n == pl.num_programs(1) - 1)
    def _combine():
        nb = npg // ppb
        m_o, l_o, pr_o = partial_softmax(_dot_nt(q, kown_ref[0].astype(BF16)) + obias_ref[...])
        o_s[npg] = _dot(pr_o.astype(BF16), vown_ref[0].astype(BF16))
        m_sl = jnp.where(lane == npg, m_o, m_all)
        l_sl = jnp.where(lane == npg, l_o, l_all)
        scp = _dot_nt(qf_ref[0], ks_s[...], HI)
        ch = lax.broadcasted_iota(jnp.int32, scp.shape, 1) % H_B
        rh = lax.broadcasted_iota(jnp.int32, scp.shape, 0) // seq
        scp = jnp.where(ch == rh, scp, 0.0)
        cols_per_blk = ppb * H_B
        gather = (lax.broadcasted_iota(jnp.int32, (npg * H_B, nb), 0) // cols_per_blk
                  == lax.broadcasted_iota(jnp.int32, (npg * H_B, nb), 1)).astype(F32)
        scb = _dot(scp, gather, HI) * (1.0 / MOBA_BLOCK)
        sel = _select_topk(scb, None, 1)
        slot = lax.broadcasted_iota(jnp.int32, (nb, m_s.shape[1]), 1)
        expand = (slot // ppb == lax.broadcasted_iota(jnp.int32, (nb, m_s.shape[1]), 0)) & (slot < npg)
        sel_slot = jnp.where(lane == npg, 1.0, _dot(sel, expand.astype(F32), HI))
        m_sel = jnp.where(sel_slot > 0.0, m_sl, NEG_INF)
        w = sel_slot * jnp.exp(m_sel - jnp.max(m_sel, axis=-1, keepdims=True))
        w = w / jnp.sum(w * l_sl, axis=-1, keepdims=True)
        acc = w[:, 0:1] * o_s[0]
        for pg in range(1, npg + 1):
            acc = acc + w[:, pg:pg + 1] * o_s[pg]
        o_ref[0] = acc


def moba_decode(qn, kn, v, cache_k, cache_v, page_table, layer, bsz, seq):
    ppb = MOBA_BLOCK // PAGE_SIZE
    pps = DECODE_PAGES_PER_STEP
    n_pages = page_table.shape[1]
    nrow = H_B * seq
    assert nrow % SUBLANES == 0 and n_pages // ppb >= MOBA_TOPK and n_pages % ppb == 0
    assert n_pages % pps == 0 and n_pages < LANES
    rows = PAGE_SIZE * H_B
    depth, n_phys = cache_k.shape[:2]
    ck = cache_k.reshape(depth, n_phys, rows, HD_B)
    cv = cache_v.reshape(depth, n_phys, rows, HD_B)
    qc = jnp.transpose(qn.reshape(bsz, seq, H_B, HD_B), (0, 2, 1, 3)).reshape(bsz, nrow, HD_B)
    c_h, c_q = jnp.arange(nrow) // seq, jnp.arange(nrow) % seq
    key_h = jnp.arange(rows) % H_B
    bias = jnp.where(c_h[:, None] == key_h[None, :], 0.0, NEG_INF).astype(F32)
    own = jnp.arange(seq * H_B)
    obias = jnp.where((c_h[:, None] == own[None, :] % H_B) & (own[None, :] // H_B <= c_q[:, None]),
                      0.0, NEG_INF).astype(F32)
    kown = kn.reshape(bsz, seq * H_B, HD_B)
    vown = v.reshape(bsz, seq * H_B, HD_B)

    page = lambda j: pl.BlockSpec((1, 1, rows, HD_B), lambda b, n, pt, j=j: (layer, pt[b, n * pps + j], 0, 0))
    seq3 = lambda shape: pl.BlockSpec((1,) + shape, lambda b, n, pt: (b, 0, 0))
    const2 = lambda shape: pl.BlockSpec(shape, lambda b, n, pt: (0, 0))
    grid_spec = pltpu.PrefetchScalarGridSpec(
        num_scalar_prefetch=1, grid=(bsz, n_pages // pps),
        in_specs=[seq3((nrow, HD_B)), seq3((nrow, HD_B)), const2((nrow, rows)), const2((nrow, seq * H_B)),
                  seq3((seq * H_B, HD_B)), seq3((seq * H_B, HD_B))]
                 + [page(j) for j in range(pps)] + [page(j) for j in range(pps)],
        out_specs=seq3((nrow, HD_B)),
        scratch_shapes=[pltpu.VMEM((nrow, LANES), F32), pltpu.VMEM((nrow, LANES), F32),
                        pltpu.VMEM((n_pages + 1, nrow, HD_B), F32), pltpu.VMEM((n_pages * H_B, HD_B), F32)])
    out = pl.pallas_call(
        functools.partial(_moba_decode_kernel, pps=pps, seq=seq), grid_spec=grid_spec,
        out_shape=jax.ShapeDtypeStruct((bsz, nrow, HD_B), F32),
        compiler_params=_params("arbitrary", "arbitrary"), name="moba_decode")(
            page_table, qc.astype(BF16), qc, bias, obias, kown, vown, *([ck] * pps), *([cv] * pps))
    return jnp.transpose(out.reshape(bsz, H_B, seq, HD_B), (0, 2, 1, 3)).reshape(bsz * seq, W_BR)


def _mixer(x, layer, lw, states, st, bsz, seq, attend):
    (ln_g, w_t, conv_a, a_log, dt_bias, norm_a, qnorm_b, knorm_b, b_if, norm_c, wa, wb, wc, wo) = lw
    conv_buf, s_a, c_c, n_c, m_c = states
    h = rms_cast(x, ln_g)
    p = in_proj(h, w_t, layer)
    ps = small_proj(h, w_t, layer)
    oa, conv_new, s_new = delta_branch(p, ps, conv_a, conv_buf, a_log, dt_bias, s_a, st, norm_a, bsz, seq)
    ob, kn, vb = attend(p, qnorm_b, knorm_b)
    oc, c_new, n_new, m_new = mlstm_branch(p, ps, b_if, c_c, n_c, m_c, st, norm_c, bsz, seq)
    y = out_proj(x, merge(oa, ob, oc, p, wa, wb, wc), wo)
    kv_shape = (bsz, seq, H_B, HD_B)
    return y, (kn.reshape(kv_shape), vb.reshape(kv_shape), conv_new, s_new, c_new, n_new, m_new)


def _pad_states(conv, delta, c, n, m):
    conv = jnp.pad(conv, ((0, 0), (0, 0), (SUBLANES - (CONV_W - 1), 0), (0, 0)))
    m = jnp.pad(m, ((0, 0), (0, 0), (0, LANES - H_C)))[:, :, None, :]
    return conv, delta, c, n, m


def kernel(x_prompt, x_sample, cache_k, cache_v, page_table, state_conv_a, state_delta_a, state_mlstm_c,
           state_mlstm_n, state_mlstm_m, ln_g, w_in, conv_a, a_log, dt_bias, norm_a, qnorm_b, knorm_b,
           b_if, norm_c, w_br_a, w_br_b, w_br_c, w_out):
    bp, sp, d = x_prompt.shape
    bs, ss, _ = x_sample.shape
    depth = w_in.shape[0]
    yp = x_prompt.reshape(bp * sp, d)
    ys = x_sample.reshape(bs * ss, d)
    new_p = [[] for _ in range(7)]
    new_s = [[] for _ in range(7)]
    w_t = jnp.swapaxes(w_in, 1, 2)
    zeros = lambda *s: jnp.zeros(s, F32)
    states_p = _pad_states(zeros(1, bp, CONV_W - 1, 3 * W_BR), zeros(1, bp, H_A, DK_A, DK_A),
                           zeros(1, bp, H_C, DK_C, DK_C), zeros(1, bp, H_C, DK_C), zeros(1, bp, H_C))
    states_s = _pad_states(state_conv_a, state_delta_a, state_mlstm_c, state_mlstm_n, state_mlstm_m)
    for l in range(depth):
        lw = (ln_g[l], w_t, conv_a[l], a_log[l], dt_bias[l], norm_a[l], qnorm_b[l],
              knorm_b[l], b_if[l], norm_c[l], w_br_a[l].astype(BF16), w_br_b[l].astype(BF16),
              w_br_c[l].astype(BF16), w_out[l].astype(BF16))

        def attend_prompt(p, qg, kg):
            qn, kn, knb, vt, kbar = qk_norm(p, qg, kg, with_kv=True)
            ob = moba_prompt(qn, knb, vt, kbar.reshape(bp, sp // MOBA_BLOCK, W_BR), bp, sp)
            return ob, kn, p[:, COL_VB * W_BR:(COL_VB + 1) * W_BR]

        def attend_sample(p, qg, kg, l=l):
            qn, kn = qk_norm(p, qg, kg, with_kv=False)
            vb = p[:, COL_VB * W_BR:(COL_VB + 1) * W_BR]
            return moba_decode(qn, kn, vb, cache_k, cache_v, page_table, l, bs, ss), kn, vb

        yp, st_p = _mixer(yp, l, lw, states_p, 0, bp, sp, attend_prompt)
        ys, st_s = _mixer(ys, l, lw, states_s, l, bs, ss, attend_sample)
        for i in range(7):
            new_p[i].append(st_p[i])
            new_s[i].append(st_s[i])
    outs_p = [jnp.stack(t) for t in new_p]
    outs_s = [jnp.stack(t) for t in new_s]
    return (yp.reshape(bp, sp, d), ys.reshape(bs, ss, d), *outs_p, *outs_s)
```

```python
import functools

import jax
import jax.numpy as jnp
from jax import lax
from jax.experimental import pallas as pl
from jax.experimental.pallas import tpu as pltpu

F32 = jnp.float32
BF16 = jnp.bfloat16
HI = lax.Precision.HIGHEST
EPS = 1e-6
NEG_INF = float("-inf")

H_A, DK_A = 8, 128
H_B, HD_B = 8, 128
H_C, DK_C = 4, 256
CONV_W = 4
MOBA_BLOCK = 256
MOBA_TOPK = 3
PAGE_SIZE = 128
CHUNK = 64
W_BR = 1024
D_MODEL = 2048

LANES = 128
SUBLANES = 8
BF16_ROWS = 16
LOG2E = 1.4426950408889634
VMEM_LIMIT = 48 * 1024 * 1024

COL_QA, COL_KA, COL_VA, COL_ZA = 0, 1, 2, 3
COL_QB, COL_KB, COL_VB, COL_ZB = 4, 5, 6, 7
COL_QC, COL_KC, COL_VC, COL_ZC, COL_OC = 8, 9, 10, 11, 12
COL_GATE = 13
N_BIG = 19 * W_BR
OFF_SMALL_A = 4 * W_BR
OFF_SMALL_C = OFF_SMALL_A + 2 * H_A + 9 * W_BR
OFF_GATE = OFF_SMALL_C + 2 * H_C
WIN_A, WIN_C = OFF_SMALL_A // LANES, OFF_SMALL_C // LANES
LANE_BETA, LANE_ALPHA = 0, H_A
LANE_I = OFF_SMALL_C % LANES
LANE_F = LANE_I + H_C
SHIFT_A, SHIFT_BC, SHIFT_G = 0, 2 * H_A, 2 * H_A + 2 * H_C
FIRST_BC_BLOCK, FIRST_G_BLOCK = COL_QB, COL_GATE


def _params(*sem):
    return pltpu.CompilerParams(dimension_semantics=sem, vmem_limit_bytes=VMEM_LIMIT)


def _sigmoid(x):
    return 1.0 / (1.0 + jnp.exp(-x))


def _silu(x):
    return x * _sigmoid(x)


def _softplus(x):
    return jnp.maximum(x, 0.0) + jnp.log(1.0 + jnp.exp(-jnp.abs(x)))


def _dot(a, b, precision=None):
    return jnp.dot(a, b, precision=precision, preferred_element_type=F32)


def _dot_nt(a, b, precision=None):
    return lax.dot_general(a, b, (((1,), (1,)), ((), ())), precision=precision,
                           preferred_element_type=F32)


def _dot_tn(a, b, precision=None):
    return lax.dot_general(a, b, (((0,), (0,)), ((), ())), precision=precision,
                           preferred_element_type=F32)


def _cumsum_rows(x):
    rid = lax.broadcasted_iota(jnp.int32, x.shape, 0)
    d = 1
    while d < x.shape[0]:
        x = x + jnp.where(rid >= d, pltpu.roll(x, d, 0), 0.0)
        d *= 2
    return x


def _branch_dtype(block_rows):
    return BF16 if block_rows % BF16_ROWS == 0 else F32


def _pick_tile(n, candidates):
    for c in candidates:
        if n % c == 0:
            return c
    return n


def _rms_kernel(x_ref, g_ref, o_ref):
    x = x_ref[...]
    y = x * lax.rsqrt(jnp.mean(x * x, axis=-1, keepdims=True) + EPS)
    o_ref[...] = (y * g_ref[...]).astype(o_ref.dtype)


def rms_cast(x, g):
    t, d = x.shape
    tm = _pick_tile(t, (512, 256, 128))
    return pl.pallas_call(
        _rms_kernel, grid=(t // tm,),
        in_specs=[pl.BlockSpec((tm, d), lambda i: (i, 0)), pl.BlockSpec((1, d), lambda i: (0, 0))],
        out_specs=pl.BlockSpec((tm, d), lambda i: (i, 0)),
        out_shape=jax.ShapeDtypeStruct((t, d), BF16),
        compiler_params=_params("arbitrary"), name="rms_cast")(x, g.reshape(1, d))


def _in_proj_kernel(a_ref, w_ref, o_ref, wb_s):
    @pl.when(pl.program_id(1) == 0)
    def _cast():
        wb_s[...] = w_ref[...].astype(BF16)

    o_ref[...] = _dot_nt(a_ref[...], wb_s[...])


def in_proj(h, w_t, layer):
    t, k = h.shape
    tm = _pick_tile(t, (1024, 768, 512, 256, 128))
    tn = W_BR

    def w_rows(j, i):
        shift = jnp.where(j < FIRST_BC_BLOCK, SHIFT_A // SUBLANES,
                          jnp.where(j < FIRST_G_BLOCK, SHIFT_BC // SUBLANES, SHIFT_G // SUBLANES))
        return layer, (j * (tn // SUBLANES) + shift) * SUBLANES, 0

    return pl.pallas_call(
        _in_proj_kernel, grid=(N_BIG // tn, t // tm),
        in_specs=[pl.BlockSpec((tm, k), lambda j, i: (i, 0)),
                  pl.BlockSpec((None, pl.Element(tn), pl.Element(k)), w_rows)],
        out_specs=pl.BlockSpec((tm, tn), lambda j, i: (i, j)),
        out_shape=jax.ShapeDtypeStruct((t, N_BIG), F32),
        scratch_shapes=[pltpu.VMEM((tn, k), BF16)],
        compiler_params=_params("arbitrary", "arbitrary"), name="in_proj")(h, w_t)


def _small_proj_kernel(a_ref, wa_ref, wc_ref, o_ref):
    a = a_ref[...]
    o_ref[:, 0:LANES] = _dot_nt(a, wa_ref[0].astype(BF16))
    o_ref[:, LANES:2 * LANES] = _dot_nt(a, wc_ref[0].astype(BF16))


def small_proj(h, w_t, layer):
    t, k = h.shape
    tm = _pick_tile(t, (1024, 512, 256, 128))
    win = lambda c: pl.BlockSpec((1, LANES, k), lambda i, c=c: (layer, c, 0))
    return pl.pallas_call(
        _small_proj_kernel, grid=(t // tm,),
        in_specs=[pl.BlockSpec((tm, k), lambda i: (i, 0)), win(WIN_A), win(WIN_C)],
        out_specs=pl.BlockSpec((tm, 2 * LANES), lambda i: (i, 0)),
        out_shape=jax.ShapeDtypeStruct((t, 2 * LANES), F32),
        compiler_params=_params("arbitrary"), name="small_proj")(h, w_t, w_t)


def _merge_kernel(oa_ref, ob_ref, oc_ref, zb_ref, ga_ref, gb_ref, gc_ref, wa_ref, wb_ref, wc_ref, o_ref):
    ob = ob_ref[...] * _silu(zb_ref[...])
    m = _sigmoid(ga_ref[...]) * _dot(oa_ref[...].astype(BF16), wa_ref[...])
    m += _sigmoid(gb_ref[...]) * _dot(ob.astype(BF16), wb_ref[...])
    m += _sigmoid(gc_ref[...]) * _dot(oc_ref[...].astype(BF16), wc_ref[...])
    o_ref[...] = m.astype(o_ref.dtype)


def merge(oa, ob, oc, p, wa, wb, wc):
    t = oa.shape[0]
    tm = _pick_tile(t, (512, 256, 128))
    tn = W_BR
    row = lambda j, i: (i, 0)
    return pl.pallas_call(
        _merge_kernel, grid=(D_MODEL // tn, t // tm),
        in_specs=[pl.BlockSpec((tm, W_BR), row), pl.BlockSpec((tm, W_BR), row), pl.BlockSpec((tm, W_BR), row),
                  pl.BlockSpec((tm, W_BR), lambda j, i: (i, COL_ZB)),
                  pl.BlockSpec((tm, tn), lambda j, i: (i, COL_GATE + j)),
                  pl.BlockSpec((tm, tn), lambda j, i: (i, COL_GATE + 2 + j)),
                  pl.BlockSpec((tm, tn), lambda j, i: (i, COL_GATE + 4 + j)),
                  pl.BlockSpec((W_BR, tn), lambda j, i: (0, j)),
                  pl.BlockSpec((W_BR, tn), lambda j, i: (0, j)),
                  pl.BlockSpec((W_BR, tn), lambda j, i: (0, j))],
        out_specs=pl.BlockSpec((tm, tn), lambda j, i: (i, j)),
        out_shape=jax.ShapeDtypeStruct((t, D_MODEL), BF16),
        compiler_params=_params("arbitrary", "arbitrary"), name="merge")(oa, ob, oc, p, p, p, p, wa, wb, wc)


def _outproj_kernel(x_ref, m_ref, w_ref, o_ref):
    o_ref[...] = x_ref[...] + _dot(m_ref[...], w_ref[...])


def out_proj(x, m, w):
    t, d = x.shape
    tm = _pick_tile(t, (512, 256, 128))
    tn = 1024
    return pl.pallas_call(
        _outproj_kernel, grid=(d // tn, t // tm),
        in_specs=[pl.BlockSpec((tm, tn), lambda j, i: (i, j)), pl.BlockSpec((tm, d), lambda j, i: (i, 0)),
                  pl.BlockSpec((d, tn), lambda j, i: (0, j))],
        out_specs=pl.BlockSpec((tm, tn), lambda j, i: (i, j)),
        out_shape=jax.ShapeDtypeStruct((t, d), F32),
        compiler_params=_params("arbitrary", "arbitrary"), name="out_proj")(x, m, w)


def _bdot(a, b):
    return _dot(a.astype(BF16), b.astype(BF16))


DELTA_CHUNKS_PER_STEP = 4


def _delta_kernel(q_ref, k_ref, v_ref, z_ref, sm_ref, cw_ref, cb_ref, al_ref, dtb_ref, s0_ref, ng_ref,
                  o_ref, cn_ref, sn_ref, xp_ref, s_ref, *, c_in):
    c = pl.program_id(1)
    C = CHUNK
    hd = DK_A

    @pl.when(c == 0)
    def _init():
        xp_ref[...] = jnp.zeros(xp_ref.shape, F32)
        xp_ref[0:SUBLANES, :] = cb_ref[0]
        s_ref[...] = s0_ref[0]

    xp_ref[SUBLANES:SUBLANES + c_in, 0:W_BR] = q_ref[...]
    xp_ref[SUBLANES:SUBLANES + c_in, W_BR:2 * W_BR] = k_ref[...]
    xp_ref[SUBLANES:SUBLANES + c_in, 2 * W_BR:3 * W_BR] = v_ref[...]

    nch = max(c_in // C, 1)
    padded = c_in < C
    valid = lax.broadcasted_iota(jnp.int32, (C, 1), 0) < c_in
    mask = (lambda x, fill: jnp.where(valid, x, fill)) if padded else (lambda x, fill: x)
    ri = lax.broadcasted_iota(jnp.int32, (C, C), 0)
    ci = lax.broadcasted_iota(jnp.int32, (C, C), 1)
    causal = ci <= ri
    strict = ci < ri

    beta_c, gcum_c, gcum_t_c, e_g_c, e_end_c, e_last_c = [], [], [], [], [], []
    for kc in range(nch):
        if padded:
            sm = jnp.concatenate([sm_ref[...], jnp.zeros((C - c_in, LANES), F32)], axis=0)
        else:
            sm = sm_ref[kc * C:(kc + 1) * C, :]
        g = mask(-jnp.exp(al_ref[...]) * _softplus(sm + dtb_ref[...]), 0.0)
        gcum = _cumsum_rows(g)
        g_last = gcum[C - 1:C, :]
        beta_c.append(_sigmoid(sm))
        gcum_c.append(gcum)
        gcum_t_c.append(gcum.T)
        e_g_c.append(jnp.exp(gcum))
        e_end_c.append(jnp.exp(g_last - gcum))
        e_last_c.append(jnp.exp(g_last))

    def conv(kc, col):
        sl = slice(col, col + hd)
        r0 = 5 + kc * C
        y = xp_ref[r0:r0 + C, sl] * cw_ref[0:1, sl]
        for j in range(1, CONV_W):
            y = y + xp_ref[r0 + j:r0 + j + C, sl] * cw_ref[j:j + 1, sl]
        return mask(_silu(y), 0.0)

    heads = range(H_A)
    pairs = [(kc, h) for kc in range(nch) for h in heads]
    npair = range(len(pairs))
    col = lambda vec, kc, lane: vec[kc][:, lane:lane + 1]
    q_l = [conv(kc, h * hd) for kc, h in pairs]
    k_l = [conv(kc, W_BR + h * hd) for kc, h in pairs]
    v_l = [conv(kc, 2 * W_BR + h * hd) for kc, h in pairs]
    q_l = [q * lax.rsqrt(jnp.sum(q * q, axis=-1, keepdims=True) + EPS) * (DK_A ** -0.5) for q in q_l]
    k_l = [k * lax.rsqrt(jnp.sum(k * k, axis=-1, keepdims=True) + EPS) for k in k_l]
    decay_l = [jnp.exp(jnp.where(causal, col(gcum_c, kc, LANE_ALPHA + h)
                                 - gcum_t_c[kc][LANE_ALPHA + h:LANE_ALPHA + h + 1, :], NEG_INF))
               for kc, h in pairs]
    b_l = [col(beta_c, kc, LANE_BETA + h) for kc, h in pairs]
    eg_l = [col(e_g_c, kc, LANE_ALPHA + h) for kc, h in pairs]
    kb_l = [k.astype(BF16) for k in k_l]
    qk_kk = [_dot_nt(jnp.concatenate([q_l[i].astype(BF16), kb_l[i]], axis=0), kb_l[i]) for i in npair]
    a_l = [jnp.where(strict, b_l[i] * decay_l[i] * qk_kk[i][C:2 * C], 0.0) for i in npair]
    r_l = [-a for a in a_l]
    pw_l = a_l
    n = 2
    while n < C:
        pw_l = [_bdot(pw, pw) for pw in pw_l]
        r_l = [r_l[i] + pw_l[i] + _bdot(r_l[i], pw_l[i]) for i in npair]
        n *= 2
    rhs_l = [jnp.concatenate([b_l[i] * v_l[i], (b_l[i] * eg_l[i]) * k_l[i]], axis=1) for i in npair]
    sol_l = [rhs_l[i] + _bdot(r_l[i], rhs_l[i]) for i in npair]
    kq_l = [jnp.concatenate([sol_l[i][:, hd:2 * hd], q_l[i] * eg_l[i]], axis=0).astype(BF16) for i in npair]
    p_l = [(decay_l[i] * qk_kk[i][0:C]).astype(BF16) for i in npair]
    kend_l = [(k_l[i] * col(e_end_c, kc, LANE_ALPHA + h)).astype(BF16) for i, (kc, h) in enumerate(pairs)]

    s_l = [s_ref[h] for h in heads]
    o_l = []
    for kc in range(nch):
        idx = [kc * H_A + h for h in heads]
        kq_s = [_dot(kq_l[i], s_l[h].astype(BF16)) for h, i in enumerate(idx)]
        wb_l = [(sol_l[i][:, 0:hd] - kq_s[h][0:C]).astype(BF16) for h, i in enumerate(idx)]
        o_l += [kq_s[h][C:2 * C] + _dot(p_l[i], wb_l[h]) for h, i in enumerate(idx)]
        s_l = [col(e_last_c, kc, LANE_ALPHA + h) * s_l[h] + _dot_tn(kend_l[i], wb_l[h])
               for h, i in enumerate(idx)]
    for h in heads:
        s_ref[h] = s_l[h]
    for i, (kc, h) in enumerate(pairs):
        o = o_l[i]
        o = o * lax.rsqrt(jnp.mean(o * o, axis=-1, keepdims=True) + EPS) * ng_ref[...]
        rows = slice(0, c_in) if padded else slice(kc * C, (kc + 1) * C)
        o = o[0:min(c_in, C)] * _silu(z_ref[rows, h * hd:(h + 1) * hd])
        o_ref[rows, h * hd:(h + 1) * hd] = o.astype(o_ref.dtype)

    tail = xp_ref[c_in:c_in + SUBLANES, :]
    xp_ref[0:SUBLANES, :] = tail

    @pl.when(c == pl.num_programs(1) - 1)
    def _fin():
        cn_ref[0] = tail
        sn_ref[0] = s_ref[...]


def delta_branch(p, ps, conv_w, conv_buf, a_log, dt_bias, s0, st, norm_g, bsz, seq):
    c_in = min(seq, DELTA_CHUNKS_PER_STEP * CHUNK)
    assert seq % c_in == 0 and (c_in % CHUNK == 0 or c_in < CHUNK)
    n_chunks = seq // c_in
    lane_vec = lambda v: jnp.zeros((1, LANES), F32).at[0, LANE_ALPHA:LANE_ALPHA + H_A].set(v)
    rb = lambda col: pl.BlockSpec((c_in, W_BR), lambda b, c, col=col: (b * n_chunks + c, col))
    const2 = lambda b, c: (0, 0)
    o, cn, sn = pl.pallas_call(
        functools.partial(_delta_kernel, c_in=c_in), grid=(bsz, n_chunks),
        in_specs=[rb(COL_QA), rb(COL_KA), rb(COL_VA), rb(COL_ZA),
                  pl.BlockSpec((c_in, LANES), lambda b, c: (b * n_chunks + c, 0)),
                  pl.BlockSpec((CONV_W, 3 * W_BR), const2),
                  pl.BlockSpec((None, 1, SUBLANES, 3 * W_BR), lambda b, c: (st, b, 0, 0)),
                  pl.BlockSpec((1, LANES), const2), pl.BlockSpec((1, LANES), const2),
                  pl.BlockSpec((None, 1, H_A, DK_A, DK_A), lambda b, c: (st, b, 0, 0, 0)),
                  pl.BlockSpec((1, DK_A), const2)],
        out_specs=[pl.BlockSpec((c_in, W_BR), lambda b, c: (b * n_chunks + c, 0)),
                   pl.BlockSpec((1, SUBLANES, 3 * W_BR), lambda b, c: (b, 0, 0)),
                   pl.BlockSpec((1, H_A, DK_A, DK_A), lambda b, c: (b, 0, 0, 0))],
        out_shape=[jax.ShapeDtypeStruct((bsz * seq, W_BR), _branch_dtype(c_in)),
                   jax.ShapeDtypeStruct((bsz, SUBLANES, 3 * W_BR), F32),
                   jax.ShapeDtypeStruct((bsz, H_A, DK_A, DK_A), F32)],
        scratch_shapes=[pltpu.VMEM((SUBLANES + max(c_in, CHUNK), 3 * W_BR), F32),
                        pltpu.VMEM((H_A, DK_A, DK_A), F32)],
        compiler_params=_params("arbitrary", "arbitrary"), name="delta")(
            p, p, p, p, ps, conv_w, conv_buf, lane_vec(a_log), lane_vec(dt_bias), s0, norm_g.reshape(1, DK_A))
    return o, cn[:, SUBLANES - (CONV_W - 1):], sn


MLSTM_CHUNKS_PER_STEP = 4


def _mlstm_kernel(q_ref, k_ref, v_ref, z_ref, og_ref, sm_ref, bif_ref, c0_ref, n0_ref, m0_ref, ng_ref,
                  o_ref, cn_ref, nn_ref, mn_ref, c_s, n_s, m_s, *, c_in):
    c = pl.program_id(1)
    C = CHUNK
    hd = DK_C

    @pl.when(c == 0)
    def _init():
        c_s[...] = c0_ref[0]
        n_s[...] = n0_ref[0]
        m_s[...] = m0_ref[0]

    nch = max(c_in // C, 1)
    padded = c_in < C

    def rows_of(ref, kc, sl):
        if padded:
            return jnp.concatenate([ref[:, sl], jnp.zeros((C - c_in, sl.stop - sl.start), F32)], axis=0)
        return ref[kc * C:(kc + 1) * C, sl]

    valid = lax.broadcasted_iota(jnp.int32, (C, 1), 0) < c_in
    ri = lax.broadcasted_iota(jnp.int32, (C, C), 0)
    ci = lax.broadcasted_iota(jnp.int32, (C, C), 1)
    causal = ci <= ri
    m_old = m_s[...]
    lane = lax.broadcasted_iota(jnp.int32, m_old.shape, 1)
    heads = range(H_C)
    hsl = [slice(h * hd, (h + 1) * hd) for h in heads]
    pairs = [(kc, h) for kc in range(nch) for h in heads]

    i_c, i_t_c, b_c, b_t_c = [], [], [], []
    for kc in range(nch):
        pre = rows_of(sm_ref, kc, slice(0, LANES)) + bif_ref[...]
        i_pre = jnp.where(valid, pre, NEG_INF) if padded else pre
        log_f = -_softplus(-pre)
        if padded:
            log_f = jnp.where(valid, log_f, 0.0)
        bcum = _dot(causal.astype(F32), log_f, HI)
        i_c.append(i_pre)
        i_t_c.append(i_pre.T)
        b_c.append(bcum)
        b_t_c.append(bcum.T)

    q_l = [rows_of(q_ref, kc, hsl[h]) for kc, h in pairs]
    ks_l = [rows_of(k_ref, kc, hsl[h]) * (DK_C ** -0.5) for kc, h in pairs]
    vb_l = [rows_of(v_ref, kc, hsl[h]).astype(BF16) for kc, h in pairs]
    qb_l = [q.astype(BF16) for q in q_l]
    qk_l = [_dot_nt(qb_l[i], ks_l[i].astype(BF16)) for i in range(len(pairs))]
    bcol_l = [b_c[kc][:, LANE_F + h:LANE_F + h + 1] for kc, h in pairs]
    icol_l = [i_c[kc][:, LANE_I + h:LANE_I + h + 1] for kc, h in pairs]
    intra_l = [jnp.where(causal, bcol_l[i] - b_t_c[kc][LANE_F + h:LANE_F + h + 1, :]
                         + i_t_c[kc][LANE_I + h:LANE_I + h + 1, :], NEG_INF) for i, (kc, h) in enumerate(pairs)]
    imax_l = [jnp.max(x, axis=-1, keepdims=True) for x in intra_l]
    m_prev = [m_old[:, h:h + 1] for h in heads]
    mt_l, winter_l, wc_l, wj_l = [], [], [], []
    for i, (kc, h) in enumerate(pairs):
        inter = bcol_l[i] + m_prev[h]
        m_t = jnp.maximum(inter, imax_l[i])
        m_new = m_t[C - 1:C, :]
        b_last = bcol_l[i][C - 1:C, :]
        mt_l.append(m_t)
        winter_l.append(jnp.exp(inter - m_t))
        wc_l.append(jnp.exp(b_last + m_prev[h] - m_new))
        wj_l.append(jnp.exp(b_last - bcol_l[i] + icol_l[i] - m_new))
        m_prev[h] = m_new
    m_new_vec = m_old
    for h in heads:
        m_new_vec = jnp.where(lane == h, m_prev[h], m_new_vec)
    m_s[...] = m_new_vec

    s_l = [qk_l[i] * jnp.exp(intra_l[i] - mt_l[i]) for i in range(len(pairs))]
    sv_l = [_dot(s_l[i].astype(BF16), vb_l[i]) for i in range(len(pairs))]
    ssum_l = [jnp.sum(s, axis=-1, keepdims=True) for s in s_l]
    kw_l = [wj_l[i] * ks_l[i] for i in range(len(pairs))]
    upd_l = [_dot_tn(kw_l[i].astype(BF16), vb_l[i]) for i in range(len(pairs))]
    nsum_l = [jnp.sum(kw, axis=0, keepdims=True) for kw in kw_l]

    cm_l = [c_s[h] for h in heads]
    n_l = [n_s[h:h + 1, :] for h in heads]
    for i, (kc, h) in enumerate(pairs):
        num = winter_l[i] * _dot(qb_l[i], cm_l[h].astype(BF16)) + sv_l[i]
        den = winter_l[i] * jnp.sum(q_l[i] * n_l[h], axis=-1, keepdims=True) + ssum_l[i]
        hh = num / jnp.maximum(jnp.abs(den), jnp.exp(-mt_l[i]))
        cm_l[h] = wc_l[i] * cm_l[h] + upd_l[i]
        n_l[h] = wc_l[i] * n_l[h] + nsum_l[i]
        rows = slice(0, c_in) if padded else slice(kc * C, (kc + 1) * C)
        hc = _sigmoid(og_ref[rows, hsl[h]]) * hh[0:min(c_in, C)]
        hc = hc * lax.rsqrt(jnp.mean(hc * hc, axis=-1, keepdims=True) + EPS) * ng_ref[...]
        o_ref[rows, hsl[h]] = (hc * _silu(z_ref[rows, hsl[h]])).astype(o_ref.dtype)
    for h in heads:
        c_s[h] = cm_l[h]
        n_s[h:h + 1, :] = n_l[h]

    @pl.when(c == pl.num_programs(1) - 1)
    def _fin():
        cn_ref[0] = c_s[...]
        nn_ref[0] = n_s[...]
        mn_ref[0] = m_new_vec


def mlstm_branch(p, ps, b_if, c0, n0, m0p, st, norm_g, bsz, seq):
    c_in = min(seq, MLSTM_CHUNKS_PER_STEP * CHUNK)
    assert seq % c_in == 0 and (c_in % CHUNK == 0 or c_in < CHUNK)
    n_chunks = seq // c_in
    bif = jnp.zeros((1, LANES), F32).at[0, LANE_I:LANE_I + 2 * H_C].set(b_if)
    rb = lambda col: pl.BlockSpec((c_in, W_BR), lambda b, c, col=col: (b * n_chunks + c, col))
    const2 = lambda b, c: (0, 0)
    st4 = pl.BlockSpec((1, H_C, DK_C, DK_C), lambda b, c: (b, 0, 0, 0))
    st3 = pl.BlockSpec((1, H_C, DK_C), lambda b, c: (b, 0, 0))
    stm = pl.BlockSpec((1, 1, LANES), lambda b, c: (b, 0, 0))
    in4 = pl.BlockSpec((None, 1, H_C, DK_C, DK_C), lambda b, c: (st, b, 0, 0, 0))
    in3 = pl.BlockSpec((None, 1, H_C, DK_C), lambda b, c: (st, b, 0, 0))
    inm = pl.BlockSpec((None, 1, 1, LANES), lambda b, c: (st, b, 0, 0))
    o, cn, nn, mn = pl.pallas_call(
        functools.partial(_mlstm_kernel, c_in=c_in), grid=(bsz, n_chunks),
        in_specs=[rb(COL_QC), rb(COL_KC), rb(COL_VC), rb(COL_ZC), rb(COL_OC),
                  pl.BlockSpec((c_in, LANES), lambda b, c: (b * n_chunks + c, 1)),
                  pl.BlockSpec((1, LANES), const2), in4, in3, inm, pl.BlockSpec((1, DK_C), const2)],
        out_specs=[pl.BlockSpec((c_in, W_BR), lambda b, c: (b * n_chunks + c, 0)), st4, st3, stm],
        out_shape=[jax.ShapeDtypeStruct((bsz * seq, W_BR), _branch_dtype(c_in)),
                   jax.ShapeDtypeStruct((bsz, H_C, DK_C, DK_C), F32),
                   jax.ShapeDtypeStruct((bsz, H_C, DK_C), F32),
                   jax.ShapeDtypeStruct((bsz, 1, LANES), F32)],
        scratch_shapes=[pltpu.VMEM((H_C, DK_C, DK_C), F32), pltpu.VMEM((H_C, DK_C), F32),
                        pltpu.VMEM((1, LANES), F32)],
        compiler_params=_params("arbitrary", "arbitrary"), name="mlstm")(
            p, p, p, p, p, ps, bif, c0, n0, m0p, norm_g.reshape(1, DK_C))
    return o, cn, nn, mn[:, 0, :H_C]


def _head_rms(x, g):
    return x * lax.rsqrt(jnp.mean(x * x, axis=-1, keepdims=True) + EPS) * g


def _qknorm_kernel(q_ref, k_ref, qg_ref, kg_ref, qn_ref, kn_ref):
    for h in range(H_B):
        sl = slice(h * HD_B, (h + 1) * HD_B)
        qn_ref[:, sl] = _head_rms(q_ref[:, sl], qg_ref[...]) * (HD_B ** -0.5)
        kn_ref[:, sl] = _head_rms(k_ref[:, sl], kg_ref[...])


def _qknorm_kv_kernel(q_ref, k_ref, v_ref, qg_ref, kg_ref, qn_ref, kn_ref, knb_ref, vt_ref, kbar_ref):
    for h in range(H_B):
        sl = slice(h * HD_B, (h + 1) * HD_B)
        qn_ref[:, sl] = _head_rms(q_ref[:, sl], qg_ref[...]) * (HD_B ** -0.5)
        kn = _head_rms(k_ref[:, sl], kg_ref[...])
        kn_ref[:, sl] = kn
        knb_ref[0, :, sl] = kn.astype(BF16)
        kbar_ref[0, :, sl] = jnp.mean(kn, axis=0, keepdims=True)
    eye = (lax.broadcasted_iota(jnp.int32, (HD_B, HD_B), 0)
           == lax.broadcasted_iota(jnp.int32, (HD_B, HD_B), 1)).astype(BF16)
    for h in range(H_B):
        sl = slice(h * HD_B, (h + 1) * HD_B)
        vt_ref[0, sl, :] = _dot_nt(eye, v_ref[:, sl].astype(BF16)).astype(BF16)


def qk_norm(p, qg, kg, with_kv):
    t = p.shape[0]
    qg = qg.reshape(1, HD_B)
    kg = kg.reshape(1, HD_B)
    gs = pl.BlockSpec((1, HD_B), lambda i: (0, 0))
    if not with_kv:
        tm = _pick_tile(t, (256, 128))
        return pl.pallas_call(
            _qknorm_kernel, grid=(t // tm,),
            in_specs=[pl.BlockSpec((tm, W_BR), lambda i: (i, COL_QB)),
                      pl.BlockSpec((tm, W_BR), lambda i: (i, COL_KB)), gs, gs],
            out_specs=[pl.BlockSpec((tm, W_BR), lambda i: (i, 0))] * 2,
            out_shape=[jax.ShapeDtypeStruct((t, W_BR), F32)] * 2,
            compiler_params=_params("arbitrary"), name="qk_norm")(p, p, qg, kg)
    tm = MOBA_BLOCK
    nb = t // tm
    return pl.pallas_call(
        _qknorm_kv_kernel, grid=(nb,),
        in_specs=[pl.BlockSpec((tm, W_BR), lambda i: (i, COL_QB)),
                  pl.BlockSpec((tm, W_BR), lambda i: (i, COL_KB)),
                  pl.BlockSpec((tm, W_BR), lambda i: (i, COL_VB)), gs, gs],
        out_specs=[pl.BlockSpec((tm, W_BR), lambda i: (i, 0)), pl.BlockSpec((tm, W_BR), lambda i: (i, 0)),
                   pl.BlockSpec((1, tm, W_BR), lambda i: (i, 0, 0)),
                   pl.BlockSpec((1, W_BR, tm), lambda i: (i, 0, 0)),
                   pl.BlockSpec((1, 1, W_BR), lambda i: (i, 0, 0))],
        out_shape=[jax.ShapeDtypeStruct((t, W_BR), F32), jax.ShapeDtypeStruct((t, W_BR), F32),
                   jax.ShapeDtypeStruct((nb, tm, W_BR), BF16), jax.ShapeDtypeStruct((nb, W_BR, tm), BF16),
                   jax.ShapeDtypeStruct((nb, 1, W_BR), F32)],
        compiler_params=_params("arbitrary"), name="qk_norm_kv")(p, p, p, qg, kg)


def _select_topk(sc, valid, axis):
    nb = sc.shape[axis]
    idx = lax.broadcasted_iota(jnp.int32, sc.shape, axis).astype(F32)
    if valid is not None:
        sc = jnp.where(valid, sc, NEG_INF)
    sel = jnp.zeros(sc.shape, F32)
    for _ in range(MOBA_TOPK):
        mx = jnp.max(sc, axis=axis, keepdims=True)
        first = jnp.min(jnp.where(sc == mx, idx, float(nb)), axis=axis, keepdims=True)
        hit = idx == first
        sel = jnp.where(hit & (mx > NEG_INF), 1.0, sel)
        sc = jnp.where(hit, NEG_INF, sc)
    return sel


MOBA_HEADS_PER_STEP = 8


def _moba_prompt_kernel(q_ref, k_ref, vt_ref, kbar_ref, o_ref, sel_ref, qb_ref, acc_ref):
    i = pl.program_id(2)
    blk = MOBA_BLOCK
    hp = MOBA_HEADS_PER_STEP
    nb = kbar_ref.shape[1]
    past = lax.broadcasted_iota(jnp.int32, (nb, blk), 0) < i
    kpos = lax.broadcasted_iota(jnp.int32, (blk, blk), 0)
    qpos = lax.broadcasted_iota(jnp.int32, (blk, blk), 1)
    heads = [slice(hh * HD_B, (hh + 1) * HD_B) for hh in range(hp)]

    ones = jnp.ones((BF16_ROWS, blk), BF16)

    def pv(n, sl, pr):
        return _dot(jnp.concatenate([vt_ref[n, sl, :], ones], axis=0), pr)

    for hh, sl in enumerate(heads):
        qb_ref[hh] = (q_ref[:, sl] * LOG2E).astype(BF16)
    s_l = [_dot_nt(k_ref[i, :, sl], qb_ref[hh]) for hh, sl in enumerate(heads)]
    sc_l = [_dot_nt(kbar_ref[0, :, sl], q_ref[:, sl], HI) for sl in heads]
    ms, pr_l = [], []
    for hh in range(hp):
        s = jnp.where(kpos <= qpos, s_l[hh], NEG_INF)
        m = jnp.max(s, axis=0, keepdims=True)
        ms.append(m)
        pr_l.append(jnp.exp2(s - m).astype(BF16))
    for hh, sl in enumerate(heads):
        acc_ref[hh] = pv(i, sl, pr_l[hh])
        sel_ref[hh] = _select_topk(sc_l[hh], past, 0)

    def body(n, ms):
        s_l = [_dot_nt(k_ref[n, :, sl], qb_ref[hh]) for hh, sl in enumerate(heads)]
        ms_new, pr_l, alpha_l = [], [], []
        for hh in range(hp):
            s = jnp.where(sel_ref[hh, pl.ds(n, 1), :] > 0.0, s_l[hh], NEG_INF)
            m_new = jnp.maximum(ms[hh], jnp.max(s, axis=0, keepdims=True))
            alpha_l.append(jnp.exp2(ms[hh] - m_new))
            ms_new.append(m_new)
            pr_l.append(jnp.exp2(s - m_new).astype(BF16))
        for hh, sl in enumerate(heads):
            acc_ref[hh] = alpha_l[hh] * acc_ref[hh] + pv(n, sl, pr_l[hh])
        return tuple(ms_new)

    lax.fori_loop(0, i, body, tuple(ms))
    for hh, sl in enumerate(heads):
        o_ref[:, sl] = (acc_ref[hh, 0:HD_B, :] / acc_ref[hh, HD_B:HD_B + 1, :]).T.astype(o_ref.dtype)


def moba_prompt(qn, knb, vt, kbar, bsz, seq):
    nb = seq // MOBA_BLOCK
    blk = MOBA_BLOCK
    hp = MOBA_HEADS_PER_STEP
    wid = hp * HD_B
    return pl.pallas_call(
        _moba_prompt_kernel, grid=(bsz, H_B // hp, nb),
        in_specs=[pl.BlockSpec((blk, wid), lambda b, h, i: (b * nb + i, h)),
                  pl.BlockSpec((nb, blk, wid), lambda b, h, i: (b, 0, h)),
                  pl.BlockSpec((nb, wid, blk), lambda b, h, i: (b, h, 0)),
                  pl.BlockSpec((1, nb, wid), lambda b, h, i: (b, 0, h))],
        out_specs=pl.BlockSpec((blk, wid), lambda b, h, i: (b * nb + i, h)),
        out_shape=jax.ShapeDtypeStruct((bsz * seq, W_BR), _branch_dtype(blk)),
        scratch_shapes=[pltpu.VMEM((hp, nb, blk), F32), pltpu.VMEM((hp, blk, HD_B), BF16),
                        pltpu.VMEM((hp, HD_B + BF16_ROWS, blk), F32)],
        compiler_params=_params("arbitrary", "arbitrary", "arbitrary"), name="moba_prompt")(qn, knb, vt, kbar)


DECODE_PAGES_PER_STEP = 16


def _moba_decode_kernel(pt_ref, q_ref, qf_ref, bias_ref, obias_ref, kown_ref, vown_ref, *rest, pps, seq):
    del pt_ref
    k_refs = rest[:pps]
    v_refs = rest[pps:2 * pps]
    o_ref = rest[2 * pps]
    m_s, l_s, o_s, ks_s = rest[2 * pps + 1:]
    n = pl.program_id(1)
    q = q_ref[0]
    ppb = MOBA_BLOCK // PAGE_SIZE
    npg = ks_s.shape[0] // H_B
    lane = lax.broadcasted_iota(jnp.int32, m_s.shape, 1)

    def partial_softmax(s):
        m = jnp.max(s, axis=-1, keepdims=True)
        pr = jnp.exp(s - m)
        return m, jnp.sum(pr, axis=-1, keepdims=True), pr

    @pl.when(n == 0)
    def _init():
        m_s[...] = jnp.zeros(m_s.shape, F32)
        l_s[...] = jnp.zeros(l_s.shape, F32)

    m_all = m_s[...]
    l_all = l_s[...]
    s_l = [_dot_nt(q, k_refs[j][0, 0].astype(BF16)) for j in range(pps)]
    pr_l = []
    for j in range(pps):
        m, l, pr = partial_softmax(s_l[j] + bias_ref[...])
        pg = n * pps + j
        m_all = jnp.where(lane == pg, m, m_all)
        l_all = jnp.where(lane == pg, l, l_all)
        pr_l.append(pr.astype(BF16))
    for j in range(pps):
        pg = n * pps + j
        o_s[pg] = _dot(pr_l[j], v_refs[j][0, 0].astype(BF16))
        ks_s[pl.ds(pg * H_B, H_B), :] = jnp.sum(k_refs[j][0, 0].reshape(PAGE_SIZE, H_B, HD_B), axis=0)
    m_s[...] = m_all
    l_s[...] = l_all

    @pl.when(n == pl.num_programs(1) - 1)
    def _combine():
        nb = npg // ppb
        m_o, l_o, pr_o = partial_softmax(_dot_nt(q, kown_ref[0].astype(BF16)) + obias_ref[...])
        o_s[npg] = _dot(pr_o.astype(BF16), vown_ref[0].astype(BF16))
        m_sl = jnp.where(lane == npg, m_o, m_all)
        l_sl = jnp.where(lane == npg, l_o, l_all)
        scp = _dot_nt(qf_ref[0], ks_s[...], HI)
        ch = lax.broadcasted_iota(jnp.int32, scp.shape, 1) % H_B
        rh = lax.broadcasted_iota(jnp.int32, scp.shape, 0) // seq
        scp = jnp.where(ch == rh, scp, 0.0)
        cols_per_blk = ppb * H_B
        gather = (lax.broadcasted_iota(jnp.int32, (npg * H_B, nb), 0) // cols_per_blk
                  == lax.broadcasted_iota(jnp.int32, (npg * H_B, nb), 1)).astype(F32)
        scb = _dot(scp, gather, HI) * (1.0 / MOBA_BLOCK)
        sel = _select_topk(scb, None, 1)
        slot = lax.broadcasted_iota(jnp.int32, (nb, m_s.shape[1]), 1)
        expand = (slot // ppb == lax.broadcasted_iota(jnp.int32, (nb, m_s.shape[1]), 0)) & (slot < npg)
        sel_slot = jnp.where(lane == npg, 1.0, _dot(sel, expand.astype(F32), HI))
        m_sel = jnp.where(sel_slot > 0.0, m_sl, NEG_INF)
        w = sel_slot * jnp.exp(m_sel - jnp.max(m_sel, axis=-1, keepdims=True))
        w = w / jnp.sum(w * l_sl, axis=-1, keepdims=True)
        acc = w[:, 0:1] * o_s[0]
        for pg in range(1, npg + 1):
            acc = acc + w[:, pg:pg + 1] * o_s[pg]
        o_ref[0] = acc


def moba_decode(qn, kn, v, cache_k, cache_v, page_table, layer, bsz, seq):
    ppb = MOBA_BLOCK // PAGE_SIZE
    pps = DECODE_PAGES_PER_STEP
    n_pages = page_table.shape[1]
    nrow = H_B * seq
    assert nrow % SUBLANES == 0 and n_pages // ppb >= MOBA_TOPK and n_pages % ppb == 0
    assert n_pages % pps == 0 and n_pages < LANES
    rows = PAGE_SIZE * H_B
    depth, n_phys = cache_k.shape[:2]
    ck = cache_k.reshape(depth, n_phys, rows, HD_B)
    cv = cache_v.reshape(depth, n_phys, rows, HD_B)
    qc = jnp.transpose(qn.reshape(bsz, seq, H_B, HD_B), (0, 2, 1, 3)).reshape(bsz, nrow, HD_B)
    c_h, c_q = jnp.arange(nrow) // seq, jnp.arange(nrow) % seq
    key_h = jnp.arange(rows) % H_B
    bias = jnp.where(c_h[:, None] == key_h[None, :], 0.0, NEG_INF).astype(F32)
    own = jnp.arange(seq * H_B)
    obias = jnp.where((c_h[:, None] == own[None, :] % H_B) & (own[None, :] // H_B <= c_q[:, None]),
                      0.0, NEG_INF).astype(F32)
    kown = kn.reshape(bsz, seq * H_B, HD_B)
    vown = v.reshape(bsz, seq * H_B, HD_B)

    page = lambda j: pl.BlockSpec((1, 1, rows, HD_B), lambda b, n, pt, j=j: (layer, pt[b, n * pps + j], 0, 0))
    seq3 = lambda shape: pl.BlockSpec((1,) + shape, lambda b, n, pt: (b, 0, 0))
    const2 = lambda shape: pl.BlockSpec(shape, lambda b, n, pt: (0, 0))
    grid_spec = pltpu.PrefetchScalarGridSpec(
        num_scalar_prefetch=1, grid=(bsz, n_pages // pps),
        in_specs=[seq3((nrow, HD_B)), seq3((nrow, HD_B)), const2((nrow, rows)), const2((nrow, seq * H_B)),
                  seq3((seq * H_B, HD_B)), seq3((seq * H_B, HD_B))]
                 + [page(j) for j in range(pps)] + [page(j) for j in range(pps)],
        out_specs=seq3((nrow, HD_B)),
        scratch_shapes=[pltpu.VMEM((nrow, LANES), F32), pltpu.VMEM((nrow, LANES), F32),
                        pltpu.VMEM((n_pages + 1, nrow, HD_B), F32), pltpu.VMEM((n_pages * H_B, HD_B), F32)])
    out = pl.pallas_call(
        functools.partial(_moba_decode_kernel, pps=pps, seq=seq), grid_spec=grid_spec,
        out_shape=jax.ShapeDtypeStruct((bsz, nrow, HD_B), F32),
        compiler_params=_params("arbitrary", "arbitrary"), name="moba_decode")(
            page_table, qc.astype(BF16), qc, bias, obias, kown, vown, *([ck] * pps), *([cv] * pps))
    return jnp.transpose(out.reshape(bsz, H_B, seq, HD_B), (0, 2, 1, 3)).reshape(bsz * seq, W_BR)


def _mixer(x, layer, lw, states, st, bsz, seq, attend):
    (ln_g, w_t, conv_a, a_log, dt_bias, norm_a, qnorm_b, knorm_b, b_if, norm_c, wa, wb, wc, wo) = lw
    conv_buf, s_a, c_c, n_c, m_c = states
    h = rms_cast(x, ln_g)
    p = in_proj(h, w_t, layer)
    ps = small_proj(h, w_t, layer)
    oa, conv_new, s_new = delta_branch(p, ps, conv_a, conv_buf, a_log, dt_bias, s_a, st, norm_a, bsz, seq)
    ob, kn, vb = attend(p, qnorm_b, knorm_b)
    oc, c_new, n_new, m_new = mlstm_branch(p, ps, b_if, c_c, n_c, m_c, st, norm_c, bsz, seq)
    y = out_proj(x, merge(oa, ob, oc, p, wa, wb, wc), wo)
    kv_shape = (bsz, seq, H_B, HD_B)
    return y, (kn.reshape(kv_shape), vb.reshape(kv_shape), conv_new, s_new, c_new, n_new, m_new)


def _pad_states(conv, delta, c, n, m):
    conv = jnp.pad(conv, ((0, 0), (0, 0), (SUBLANES - (CONV_W - 1), 0), (0, 0)))
    m = jnp.pad(m, ((0, 0), (0, 0), (0, LANES - H_C)))[:, :, None, :]
    return conv, delta, c, n, m


def kernel(x_prompt, x_sample, cache_k, cache_v, page_table, state_conv_a, state_delta_a, state_mlstm_c,
           state_mlstm_n, state_mlstm_m, ln_g, w_in, conv_a, a_log, dt_bias, norm_a, qnorm_b, knorm_b,
           b_if, norm_c, w_br_a, w_br_b, w_br_c, w_out):
    bp, sp, d = x_prompt.shape
    bs, ss, _ = x_sample.shape
    depth = w_in.shape[0]
    yp = x_prompt.reshape(bp * sp, d)
    ys = x_sample.reshape(bs * ss, d)
    new_p = [[] for _ in range(7)]
    new_s = [[] for _ in range(7)]
    w_t = jnp.swapaxes(w_in, 1, 2)
    zeros = lambda *s: jnp.zeros(s, F32)
    states_p = _pad_states(zeros(1, bp, CONV_W - 1, 3 * W_BR), zeros(1, bp, H_A, DK_A, DK_A),
                           zeros(1, bp, H_C, DK_C, DK_C), zeros(1, bp, H_C, DK_C), zeros(1, bp, H_C))
    states_s = _pad_states(state_conv_a, state_delta_a, state_mlstm_c, state_mlstm_n, state_mlstm_m)
    for l in range(depth):
        lw = (ln_g[l], w_t, conv_a[l], a_log[l], dt_bias[l], norm_a[l], qnorm_b[l],
              knorm_b[l], b_if[l], norm_c[l], w_br_a[l].astype(BF16), w_br_b[l].astype(BF16),
              w_br_c[l].astype(BF16), w_out[l].astype(BF16))

        def attend_prompt(p, qg, kg):
            qn, kn, knb, vt, kbar = qk_norm(p, qg, kg, with_kv=True)
            ob = moba_prompt(qn, knb, vt, kbar.reshape(bp, sp // MOBA_BLOCK, W_BR), bp, sp)
            return ob, kn, p[:, COL_VB * W_BR:(COL_VB + 1) * W_BR]

        def attend_sample(p, qg, kg, l=l):
            qn, kn = qk_norm(p, qg, kg, with_kv=False)
            vb = p[:, COL_VB * W_BR:(COL_VB + 1) * W_BR]
            return moba_decode(qn, kn, vb, cache_k, cache_v, page_table, l, bs, ss), kn, vb

        yp, st_p = _mixer(yp, l, lw, states_p, 0, bp, sp, attend_prompt)
        ys, st_s = _mixer(ys, l, lw, states_s, l, bs, ss, attend_sample)
        for i in range(7):
            new_p[i].append(st_p[i])
            new_s[i].append(st_s[i])
    outs_p = [jnp.stack(t) for t in new_p]
    outs_s = [jnp.stack(t) for t in new_s]
    return (yp.reshape(bp, sp, d), ys.reshape(bs, ss, d), *outs_p, *outs_s)
```

```python
import functools

import jax
import jax.numpy as jnp
from jax import lax
from jax.experimental import pallas as pl
from jax.experimental.pallas import tpu as pltpu

F32 = jnp.float32
BF16 = jnp.bfloat16
HI = lax.Precision.HIGHEST
EPS = 1e-6
NEG_INF = float("-inf")

H_A, DK_A = 8, 128
H_B, HD_B = 8, 128
H_C, DK_C = 4, 256
CONV_W = 4
MOBA_BLOCK = 256
MOBA_TOPK = 3
PAGE_SIZE = 128
CHUNK = 64
W_BR = 1024
D_MODEL = 2048

LANES = 128
SUBLANES = 8
BF16_ROWS = 16
LOG2E = 1.4426950408889634
VMEM_LIMIT = 48 * 1024 * 1024

COL_QA, COL_KA, COL_VA, COL_ZA = 0, 1, 2, 3
COL_QB, COL_KB, COL_VB, COL_ZB = 4, 5, 6, 7
COL_QC, COL_KC, COL_VC, COL_ZC, COL_OC = 8, 9, 10, 11, 12
COL_GATE = 13
N_BIG = 19 * W_BR
OFF_SMALL_A = 4 * W_BR
OFF_SMALL_C = OFF_SMALL_A + 2 * H_A + 9 * W_BR
OFF_GATE = OFF_SMALL_C + 2 * H_C
WIN_A, WIN_C = OFF_SMALL_A // LANES, OFF_SMALL_C // LANES
LANE_BETA, LANE_ALPHA = 0, H_A
LANE_I = OFF_SMALL_C % LANES
LANE_F = LANE_I + H_C
SHIFT_A, SHIFT_BC, SHIFT_G = 0, 2 * H_A, 2 * H_A + 2 * H_C
FIRST_BC_BLOCK, FIRST_G_BLOCK = COL_QB, COL_GATE


def _params(*sem):
    return pltpu.CompilerParams(dimension_semantics=sem, vmem_limit_bytes=VMEM_LIMIT)


def _sigmoid(x):
    return 1.0 / (1.0 + jnp.exp(-x))


def _silu(x):
    return x * _sigmoid(x)


def _softplus(x):
    return jnp.maximum(x, 0.0) + jnp.log(1.0 + jnp.exp(-jnp.abs(x)))


def _dot(a, b, precision=None):
    return jnp.dot(a, b, precision=precision, preferred_element_type=F32)


def _dot_nt(a, b, precision=None):
    return lax.dot_general(a, b, (((1,), (1,)), ((), ())), precision=precision,
                           preferred_element_type=F32)


def _dot_tn(a, b, precision=None):
    return lax.dot_general(a, b, (((0,), (0,)), ((), ())), precision=precision,
                           preferred_element_type=F32)


def _cumsum_rows(x):
    rid = lax.broadcasted_iota(jnp.int32, x.shape, 0)
    d = 1
    while d < x.shape[0]:
        x = x + jnp.where(rid >= d, pltpu.roll(x, d, 0), 0.0)
        d *= 2
    return x


def _chunk_len(seq):
    return min(CHUNK, -(-seq // BF16_ROWS) * BF16_ROWS)


def _branch_dtype(block_rows):
    return BF16 if block_rows % BF16_ROWS == 0 else F32


def _pick_tile(n, candidates):
    for c in candidates:
        if n % c == 0:
            return c
    return n


def _rms_kernel(x_ref, g_ref, o_ref):
    x = x_ref[...]
    y = x * lax.rsqrt(jnp.mean(x * x, axis=-1, keepdims=True) + EPS)
    o_ref[...] = (y * g_ref[...]).astype(o_ref.dtype)


def rms_cast(x, g):
    t, d = x.shape
    tm = _pick_tile(t, (512, 256, 128))
    return pl.pallas_call(
        _rms_kernel, grid=(t // tm,),
        in_specs=[pl.BlockSpec((tm, d), lambda i: (i, 0)), pl.BlockSpec((1, d), lambda i: (0, 0))],
        out_specs=pl.BlockSpec((tm, d), lambda i: (i, 0)),
        out_shape=jax.ShapeDtypeStruct((t, d), BF16),
        compiler_params=_params("arbitrary"), name="rms_cast")(x, g.reshape(1, d))


def _in_proj_kernel(a_ref, w_ref, o_ref, wb_s):
    @pl.when(pl.program_id(1) == 0)
    def _cast():
        wb_s[...] = w_ref[...].astype(BF16)

    o_ref[...] = _dot_nt(a_ref[...], wb_s[...])


def in_proj(h, w_t, layer):
    t, k = h.shape
    tm = _pick_tile(t, (1024, 768, 512, 256, 128))
    tn = W_BR

    def w_rows(j, i):
        shift = jnp.where(j < FIRST_BC_BLOCK, SHIFT_A // SUBLANES,
                          jnp.where(j < FIRST_G_BLOCK, SHIFT_BC // SUBLANES, SHIFT_G // SUBLANES))
        return layer, (j * (tn // SUBLANES) + shift) * SUBLANES, 0

    n_i = t // tm
    row = lambda j, i: jnp.where(j % 2 == 0, i, n_i - 1 - i)
    return pl.pallas_call(
        _in_proj_kernel, grid=(N_BIG // tn, n_i),
        in_specs=[pl.BlockSpec((tm, k), lambda j, i: (row(j, i), 0)),
                  pl.BlockSpec((None, pl.Element(tn), pl.Element(k)), w_rows)],
        out_specs=pl.BlockSpec((tm, tn), lambda j, i: (row(j, i), j)),
        out_shape=jax.ShapeDtypeStruct((t, N_BIG), F32),
        scratch_shapes=[pltpu.VMEM((tn, k), BF16)],
        compiler_params=_params("arbitrary", "arbitrary"), name="in_proj")(h, w_t)


def _small_proj_kernel(a_ref, wa_ref, wc_ref, o_ref):
    a = a_ref[...]
    o_ref[:, 0:LANES] = _dot_nt(a, wa_ref[0].astype(BF16))
    o_ref[:, LANES:2 * LANES] = _dot_nt(a, wc_ref[0].astype(BF16))


def small_proj(h, w_t, layer):
    t, k = h.shape
    tm = _pick_tile(t, (1024, 512, 256, 128))
    win = lambda c: pl.BlockSpec((1, LANES, k), lambda i, c=c: (layer, c, 0))
    return pl.pallas_call(
        _small_proj_kernel, grid=(t // tm,),
        in_specs=[pl.BlockSpec((tm, k), lambda i: (i, 0)), win(WIN_A), win(WIN_C)],
        out_specs=pl.BlockSpec((tm, 2 * LANES), lambda i: (i, 0)),
        out_shape=jax.ShapeDtypeStruct((t, 2 * LANES), F32),
        compiler_params=_params("arbitrary"), name="small_proj")(h, w_t, w_t)


def _merge_kernel(oa_ref, ob_ref, oc_ref, zb_ref, ga_ref, gb_ref, gc_ref, wa_ref, wb_ref, wc_ref, o_ref):
    ob = ob_ref[...] * _silu(zb_ref[...])
    m = _sigmoid(ga_ref[...]) * _dot(oa_ref[...].astype(BF16), wa_ref[...])
    m += _sigmoid(gb_ref[...]) * _dot(ob.astype(BF16), wb_ref[...])
    m += _sigmoid(gc_ref[...]) * _dot(oc_ref[...].astype(BF16), wc_ref[...])
    o_ref[...] = m.astype(o_ref.dtype)


def merge(oa, ob, oc, p, wa, wb, wc):
    t = oa.shape[0]
    tm = _pick_tile(t, (512, 256, 128))
    tn = W_BR
    row = lambda j, i: (i, 0)
    return pl.pallas_call(
        _merge_kernel, grid=(D_MODEL // tn, t // tm),
        in_specs=[pl.BlockSpec((tm, W_BR), row), pl.BlockSpec((tm, W_BR), row), pl.BlockSpec((tm, W_BR), row),
                  pl.BlockSpec((tm, W_BR), lambda j, i: (i, COL_ZB)),
                  pl.BlockSpec((tm, tn), lambda j, i: (i, COL_GATE + j)),
                  pl.BlockSpec((tm, tn), lambda j, i: (i, COL_GATE + 2 + j)),
                  pl.BlockSpec((tm, tn), lambda j, i: (i, COL_GATE + 4 + j)),
                  pl.BlockSpec((W_BR, tn), lambda j, i: (0, j)),
                  pl.BlockSpec((W_BR, tn), lambda j, i: (0, j)),
                  pl.BlockSpec((W_BR, tn), lambda j, i: (0, j))],
        out_specs=pl.BlockSpec((tm, tn), lambda j, i: (i, j)),
        out_shape=jax.ShapeDtypeStruct((t, D_MODEL), BF16),
        compiler_params=_params("arbitrary", "arbitrary"), name="merge")(oa, ob, oc, p, p, p, p, wa, wb, wc)


def _outproj_kernel(x_ref, m_ref, w_ref, o_ref):
    o_ref[...] = x_ref[...] + _dot(m_ref[...], w_ref[...])


def out_proj(x, m, w):
    t, d = x.shape
    tm = _pick_tile(t, (512, 256, 128))
    tn = 1024
    return pl.pallas_call(
        _outproj_kernel, grid=(d // tn, t // tm),
        in_specs=[pl.BlockSpec((tm, tn), lambda j, i: (i, j)), pl.BlockSpec((tm, d), lambda j, i: (i, 0)),
                  pl.BlockSpec((d, tn), lambda j, i: (0, j))],
        out_specs=pl.BlockSpec((tm, tn), lambda j, i: (i, j)),
        out_shape=jax.ShapeDtypeStruct((t, d), F32),
        compiler_params=_params("arbitrary", "arbitrary"), name="out_proj")(x, m, w)


def _bdot(a, b):
    return _dot(a.astype(BF16), b.astype(BF16))


DELTA_CHUNKS_PER_STEP = 4


def _delta_kernel(q_ref, k_ref, v_ref, z_ref, sm_ref, cw_ref, cb_ref, al_ref, dtb_ref, s0_ref, ng_ref,
                  o_ref, cn_ref, sn_ref, xp_ref, s_ref, *, c_in, chunk):
    c = pl.program_id(1)
    C = chunk
    hd = DK_A

    @pl.when(c == 0)
    def _init():
        xp_ref[...] = jnp.zeros(xp_ref.shape, F32)
        xp_ref[0:SUBLANES, :] = cb_ref[0]
        s_ref[...] = s0_ref[0]

    xp_ref[SUBLANES:SUBLANES + c_in, 0:W_BR] = q_ref[...]
    xp_ref[SUBLANES:SUBLANES + c_in, W_BR:2 * W_BR] = k_ref[...]
    xp_ref[SUBLANES:SUBLANES + c_in, 2 * W_BR:3 * W_BR] = v_ref[...]

    nch = max(c_in // C, 1)
    padded = c_in < C
    valid = lax.broadcasted_iota(jnp.int32, (C, 1), 0) < c_in
    mask = (lambda x, fill: jnp.where(valid, x, fill)) if padded else (lambda x, fill: x)
    ri = lax.broadcasted_iota(jnp.int32, (C, C), 0)
    ci = lax.broadcasted_iota(jnp.int32, (C, C), 1)
    causal = ci <= ri
    strict = ci < ri

    beta_c, gcum_c, gcum_t_c, e_g_c, e_end_c, e_last_c = [], [], [], [], [], []
    for kc in range(nch):
        if padded:
            sm = jnp.concatenate([sm_ref[...], jnp.zeros((C - c_in, LANES), F32)], axis=0)
        else:
            sm = sm_ref[kc * C:(kc + 1) * C, :]
        g = mask(-jnp.exp(al_ref[...]) * _softplus(sm + dtb_ref[...]), 0.0)
        gcum = _cumsum_rows(g)
        g_last = gcum[C - 1:C, :]
        beta_c.append(_sigmoid(sm))
        gcum_c.append(gcum)
        gcum_t_c.append(gcum.T)
        e_g_c.append(jnp.exp(gcum))
        e_end_c.append(jnp.exp(g_last - gcum))
        e_last_c.append(jnp.exp(g_last))

    def conv(kc, col):
        sl = slice(col, col + hd)
        r0 = 5 + kc * C
        y = xp_ref[r0:r0 + C, sl] * cw_ref[0:1, sl]
        for j in range(1, CONV_W):
            y = y + xp_ref[r0 + j:r0 + j + C, sl] * cw_ref[j:j + 1, sl]
        return mask(_silu(y), 0.0)

    heads = range(H_A)
    pairs = [(kc, h) for kc in range(nch) for h in heads]
    npair = range(len(pairs))
    col = lambda vec, kc, lane: vec[kc][:, lane:lane + 1]
    q_l = [conv(kc, h * hd) for kc, h in pairs]
    k_l = [conv(kc, W_BR + h * hd) for kc, h in pairs]
    v_l = [conv(kc, 2 * W_BR + h * hd) for kc, h in pairs]
    q_l = [q * lax.rsqrt(jnp.sum(q * q, axis=-1, keepdims=True) + EPS) * (DK_A ** -0.5) for q in q_l]
    k_l = [k * lax.rsqrt(jnp.sum(k * k, axis=-1, keepdims=True) + EPS) for k in k_l]
    decay_l = [jnp.exp(jnp.where(causal, col(gcum_c, kc, LANE_ALPHA + h)
                                 - gcum_t_c[kc][LANE_ALPHA + h:LANE_ALPHA + h + 1, :], NEG_INF))
               for kc, h in pairs]
    b_l = [col(beta_c, kc, LANE_BETA + h) for kc, h in pairs]
    eg_l = [col(e_g_c, kc, LANE_ALPHA + h) for kc, h in pairs]
    kb_l = [k.astype(BF16) for k in k_l]
    qk_kk = [_dot_nt(jnp.concatenate([q_l[i].astype(BF16), kb_l[i]], axis=0), kb_l[i]) for i in npair]
    a_l = [jnp.where(strict, b_l[i] * decay_l[i] * qk_kk[i][C:2 * C], 0.0) for i in npair]
    r_l = [-a for a in a_l]
    pw_l = a_l
    n = 2
    while n < C:
        pw_l = [_bdot(pw, pw) for pw in pw_l]
        r_l = [r_l[i] + pw_l[i] + _bdot(r_l[i], pw_l[i]) for i in npair]
        n *= 2
    rhs_l = [jnp.concatenate([b_l[i] * v_l[i], (b_l[i] * eg_l[i]) * k_l[i]], axis=1) for i in npair]
    sol_l = [rhs_l[i] + _bdot(r_l[i], rhs_l[i]) for i in npair]
    kq_l = [jnp.concatenate([sol_l[i][:, hd:2 * hd], q_l[i] * eg_l[i]], axis=0).astype(BF16) for i in npair]
    p_l = [(decay_l[i] * qk_kk[i][0:C]).astype(BF16) for i in npair]
    kend_l = [(k_l[i] * col(e_end_c, kc, LANE_ALPHA + h)).astype(BF16) for i, (kc, h) in enumerate(pairs)]

    s_l = [s_ref[h] for h in heads]
    o_l = []
    for kc in range(nch):
        idx = [kc * H_A + h for h in heads]
        kq_s = [_dot(kq_l[i], s_l[h].astype(BF16)) for h, i in enumerate(idx)]
        wb_l = [(sol_l[i][:, 0:hd] - kq_s[h][0:C]).astype(BF16) for h, i in enumerate(idx)]
        o_l += [kq_s[h][C:2 * C] + _dot(p_l[i], wb_l[h]) for h, i in enumerate(idx)]
        s_l = [col(e_last_c, kc, LANE_ALPHA + h) * s_l[h] + _dot_tn(kend_l[i], wb_l[h])
               for h, i in enumerate(idx)]
    for h in heads:
        s_ref[h] = s_l[h]
    for i, (kc, h) in enumerate(pairs):
        o = o_l[i]
        o = o * lax.rsqrt(jnp.mean(o * o, axis=-1, keepdims=True) + EPS) * ng_ref[...]
        rows = slice(0, c_in) if padded else slice(kc * C, (kc + 1) * C)
        o = o[0:min(c_in, C)] * _silu(z_ref[rows, h * hd:(h + 1) * hd])
        o_ref[rows, h * hd:(h + 1) * hd] = o.astype(o_ref.dtype)

    tail = xp_ref[c_in:c_in + SUBLANES, :]
    xp_ref[0:SUBLANES, :] = tail

    @pl.when(c == pl.num_programs(1) - 1)
    def _fin():
        cn_ref[0] = tail
        sn_ref[0] = s_ref[...]


def delta_branch(p, ps, conv_w, conv_buf, a_log, dt_bias, s0, st, norm_g, bsz, seq):
    chunk = _chunk_len(seq)
    c_in = min(seq, DELTA_CHUNKS_PER_STEP * chunk)
    assert seq % c_in == 0 and (c_in % chunk == 0 or c_in < chunk)
    n_chunks = seq // c_in
    lane_vec = lambda v: jnp.zeros((1, LANES), F32).at[0, LANE_ALPHA:LANE_ALPHA + H_A].set(v)
    rb = lambda col: pl.BlockSpec((c_in, W_BR), lambda b, c, col=col: (b * n_chunks + c, col))
    const2 = lambda b, c: (0, 0)
    o, cn, sn = pl.pallas_call(
        functools.partial(_delta_kernel, c_in=c_in, chunk=chunk), grid=(bsz, n_chunks),
        in_specs=[rb(COL_QA), rb(COL_KA), rb(COL_VA), rb(COL_ZA),
                  pl.BlockSpec((c_in, LANES), lambda b, c: (b * n_chunks + c, 0)),
                  pl.BlockSpec((CONV_W, 3 * W_BR), const2),
                  pl.BlockSpec((None, 1, SUBLANES, 3 * W_BR), lambda b, c: (st, b, 0, 0)),
                  pl.BlockSpec((1, LANES), const2), pl.BlockSpec((1, LANES), const2),
                  pl.BlockSpec((None, 1, H_A, DK_A, DK_A), lambda b, c: (st, b, 0, 0, 0)),
                  pl.BlockSpec((1, DK_A), const2)],
        out_specs=[pl.BlockSpec((c_in, W_BR), lambda b, c: (b * n_chunks + c, 0)),
                   pl.BlockSpec((1, SUBLANES, 3 * W_BR), lambda b, c: (b, 0, 0)),
                   pl.BlockSpec((1, H_A, DK_A, DK_A), lambda b, c: (b, 0, 0, 0))],
        out_shape=[jax.ShapeDtypeStruct((bsz * seq, W_BR), _branch_dtype(c_in)),
                   jax.ShapeDtypeStruct((bsz, SUBLANES, 3 * W_BR), F32),
                   jax.ShapeDtypeStruct((bsz, H_A, DK_A, DK_A), F32)],
        scratch_shapes=[pltpu.VMEM((SUBLANES + max(c_in, chunk), 3 * W_BR), F32),
                        pltpu.VMEM((H_A, DK_A, DK_A), F32)],
        compiler_params=_params("arbitrary", "arbitrary"), name="delta")(
            p, p, p, p, ps, conv_w, conv_buf, lane_vec(a_log), lane_vec(dt_bias), s0, norm_g.reshape(1, DK_A))
    return o, cn[:, SUBLANES - (CONV_W - 1):], sn


MLSTM_CHUNKS_PER_STEP = 4


def _mlstm_kernel(q_ref, k_ref, v_ref, z_ref, og_ref, sm_ref, bif_ref, c0_ref, n0_ref, m0_ref, ng_ref,
                  o_ref, cn_ref, nn_ref, mn_ref, c_s, n_s, m_s, *, c_in, chunk):
    c = pl.program_id(1)
    C = chunk
    hd = DK_C

    @pl.when(c == 0)
    def _init():
        c_s[...] = c0_ref[0]
        n_s[...] = n0_ref[0]
        m_s[...] = m0_ref[0]

    nch = max(c_in // C, 1)
    padded = c_in < C

    def rows_of(ref, kc, sl):
        if padded:
            return jnp.concatenate([ref[:, sl], jnp.zeros((C - c_in, sl.stop - sl.start), F32)], axis=0)
        return ref[kc * C:(kc + 1) * C, sl]

    valid = lax.broadcasted_iota(jnp.int32, (C, 1), 0) < c_in
    ri = lax.broadcasted_iota(jnp.int32, (C, C), 0)
    ci = lax.broadcasted_iota(jnp.int32, (C, C), 1)
    causal = ci <= ri
    m_old = m_s[...]
    lane = lax.broadcasted_iota(jnp.int32, m_old.shape, 1)
    heads = range(H_C)
    hsl = [slice(h * hd, (h + 1) * hd) for h in heads]
    pairs = [(kc, h) for kc in range(nch) for h in heads]

    i_c, i_t_c, b_c, b_t_c = [], [], [], []
    for kc in range(nch):
        pre = rows_of(sm_ref, kc, slice(0, LANES)) + bif_ref[...]
        i_pre = jnp.where(valid, pre, NEG_INF) if padded else pre
        log_f = -_softplus(-pre)
        if padded:
            log_f = jnp.where(valid, log_f, 0.0)
        bcum = _dot(causal.astype(F32), log_f, HI)
        i_c.append(i_pre)
        i_t_c.append(i_pre.T)
        b_c.append(bcum)
        b_t_c.append(bcum.T)

    q_l = [rows_of(q_ref, kc, hsl[h]) for kc, h in pairs]
    ks_l = [rows_of(k_ref, kc, hsl[h]) * (DK_C ** -0.5) for kc, h in pairs]
    vb_l = [rows_of(v_ref, kc, hsl[h]).astype(BF16) for kc, h in pairs]
    qb_l = [q.astype(BF16) for q in q_l]
    qk_l = [_dot_nt(qb_l[i], ks_l[i].astype(BF16)) for i in range(len(pairs))]
    bcol_l = [b_c[kc][:, LANE_F + h:LANE_F + h + 1] for kc, h in pairs]
    icol_l = [i_c[kc][:, LANE_I + h:LANE_I + h + 1] for kc, h in pairs]
    intra_l = [jnp.where(causal, bcol_l[i] - b_t_c[kc][LANE_F + h:LANE_F + h + 1, :]
                         + i_t_c[kc][LANE_I + h:LANE_I + h + 1, :], NEG_INF) for i, (kc, h) in enumerate(pairs)]
    imax_l = [jnp.max(x, axis=-1, keepdims=True) for x in intra_l]
    m_prev = [m_old[:, h:h + 1] for h in heads]
    mt_l, winter_l, wc_l, wj_l = [], [], [], []
    for i, (kc, h) in enumerate(pairs):
        inter = bcol_l[i] + m_prev[h]
        m_t = jnp.maximum(inter, imax_l[i])
        m_new = m_t[C - 1:C, :]
        b_last = bcol_l[i][C - 1:C, :]
        mt_l.append(m_t)
        winter_l.append(jnp.exp(inter - m_t))
        wc_l.append(jnp.exp(b_last + m_prev[h] - m_new))
        wj_l.append(jnp.exp(b_last - bcol_l[i] + icol_l[i] - m_new))
        m_prev[h] = m_new
    m_new_vec = m_old
    for h in heads:
        m_new_vec = jnp.where(lane == h, m_prev[h], m_new_vec)
    m_s[...] = m_new_vec

    s_l = [qk_l[i] * jnp.exp(intra_l[i] - mt_l[i]) for i in range(len(pairs))]
    sv_l = [_dot(s_l[i].astype(BF16), vb_l[i]) for i in range(len(pairs))]
    ssum_l = [jnp.sum(s, axis=-1, keepdims=True) for s in s_l]
    kw_l = [wj_l[i] * ks_l[i] for i in range(len(pairs))]
    upd_l = [_dot_tn(kw_l[i].astype(BF16), vb_l[i]) for i in range(len(pairs))]
    nsum_l = [jnp.sum(kw, axis=0, keepdims=True) for kw in kw_l]

    cm_l = [c_s[h] for h in heads]
    n_l = [n_s[h:h + 1, :] for h in heads]
    for i, (kc, h) in enumerate(pairs):
        num = winter_l[i] * _dot(qb_l[i], cm_l[h].astype(BF16)) + sv_l[i]
        den = winter_l[i] * jnp.sum(q_l[i] * n_l[h], axis=-1, keepdims=True) + ssum_l[i]
        hh = num / jnp.maximum(jnp.abs(den), jnp.exp(-mt_l[i]))
        cm_l[h] = wc_l[i] * cm_l[h] + upd_l[i]
        n_l[h] = wc_l[i] * n_l[h] + nsum_l[i]
        rows = slice(0, c_in) if padded else slice(kc * C, (kc + 1) * C)
        hc = _sigmoid(og_ref[rows, hsl[h]]) * hh[0:min(c_in, C)]
        hc = hc * lax.rsqrt(jnp.mean(hc * hc, axis=-1, keepdims=True) + EPS) * ng_ref[...]
        o_ref[rows, hsl[h]] = (hc * _silu(z_ref[rows, hsl[h]])).astype(o_ref.dtype)
    for h in heads:
        c_s[h] = cm_l[h]
        n_s[h:h + 1, :] = n_l[h]

    @pl.when(c == pl.num_programs(1) - 1)
    def _fin():
        cn_ref[0] = c_s[...]
        nn_ref[0] = n_s[...]
        mn_ref[0] = m_new_vec


def mlstm_branch(p, ps, b_if, c0, n0, m0p, st, norm_g, bsz, seq):
    chunk = _chunk_len(seq)
    c_in = min(seq, MLSTM_CHUNKS_PER_STEP * chunk)
    assert seq % c_in == 0 and (c_in % chunk == 0 or c_in < chunk)
    n_chunks = seq // c_in
    bif = jnp.zeros((1, LANES), F32).at[0, LANE_I:LANE_I + 2 * H_C].set(b_if)
    rb = lambda col: pl.BlockSpec((c_in, W_BR), lambda b, c, col=col: (b * n_chunks + c, col))
    const2 = lambda b, c: (0, 0)
    st4 = pl.BlockSpec((1, H_C, DK_C, DK_C), lambda b, c: (b, 0, 0, 0))
    st3 = pl.BlockSpec((1, H_C, DK_C), lambda b, c: (b, 0, 0))
    stm = pl.BlockSpec((1, 1, LANES), lambda b, c: (b, 0, 0))
    in4 = pl.BlockSpec((None, 1, H_C, DK_C, DK_C), lambda b, c: (st, b, 0, 0, 0))
    in3 = pl.BlockSpec((None, 1, H_C, DK_C), lambda b, c: (st, b, 0, 0))
    inm = pl.BlockSpec((None, 1, 1, LANES), lambda b, c: (st, b, 0, 0))
    o, cn, nn, mn = pl.pallas_call(
        functools.partial(_mlstm_kernel, c_in=c_in, chunk=chunk), grid=(bsz, n_chunks),
        in_specs=[rb(COL_QC), rb(COL_KC), rb(COL_VC), rb(COL_ZC), rb(COL_OC),
                  pl.BlockSpec((c_in, LANES), lambda b, c: (b * n_chunks + c, 1)),
                  pl.BlockSpec((1, LANES), const2), in4, in3, inm, pl.BlockSpec((1, DK_C), const2)],
        out_specs=[pl.BlockSpec((c_in, W_BR), lambda b, c: (b * n_chunks + c, 0)), st4, st3, stm],
        out_shape=[jax.ShapeDtypeStruct((bsz * seq, W_BR), _branch_dtype(c_in)),
                   jax.ShapeDtypeStruct((bsz, H_C, DK_C, DK_C), F32),
                   jax.ShapeDtypeStruct((bsz, H_C, DK_C), F32),
                   jax.ShapeDtypeStruct((bsz, 1, LANES), F32)],
        scratch_shapes=[pltpu.VMEM((H_C, DK_C, DK_C), F32), pltpu.VMEM((H_C, DK_C), F32),
                        pltpu.VMEM((1, LANES), F32)],
        compiler_params=_params("arbitrary", "arbitrary"), name="mlstm")(
            p, p, p, p, p, ps, bif, c0, n0, m0p, norm_g.reshape(1, DK_C))
    return o, cn, nn, mn[:, 0, :H_C]


def _head_rms(x, g):
    return x * lax.rsqrt(jnp.mean(x * x, axis=-1, keepdims=True) + EPS) * g


def _qknorm_kernel(q_ref, k_ref, qg_ref, kg_ref, qn_ref, kn_ref):
    for h in range(H_B):
        sl = slice(h * HD_B, (h + 1) * HD_B)
        qn_ref[:, sl] = _head_rms(q_ref[:, sl], qg_ref[...]) * (HD_B ** -0.5)
        kn_ref[:, sl] = _head_rms(k_ref[:, sl], kg_ref[...])


def _qknorm_kv_kernel(q_ref, k_ref, v_ref, qg_ref, kg_ref, qn_ref, kn_ref, knb_ref, vt_ref, kbar_ref):
    for h in range(H_B):
        sl = slice(h * HD_B, (h + 1) * HD_B)
        qn_ref[:, sl] = _head_rms(q_ref[:, sl], qg_ref[...]) * (HD_B ** -0.5)
        kn = _head_rms(k_ref[:, sl], kg_ref[...])
        kn_ref[:, sl] = kn
        knb_ref[0, :, sl] = kn.astype(BF16)
        kbar_ref[0, :, sl] = jnp.mean(kn, axis=0, keepdims=True)
    eye = (lax.broadcasted_iota(jnp.int32, (HD_B, HD_B), 0)
           == lax.broadcasted_iota(jnp.int32, (HD_B, HD_B), 1)).astype(BF16)
    for h in range(H_B):
        sl = slice(h * HD_B, (h + 1) * HD_B)
        vt_ref[0, sl, :] = _dot_nt(eye, v_ref[:, sl].astype(BF16)).astype(BF16)


def qk_norm(p, qg, kg, with_kv):
    t = p.shape[0]
    qg = qg.reshape(1, HD_B)
    kg = kg.reshape(1, HD_B)
    gs = pl.BlockSpec((1, HD_B), lambda i: (0, 0))
    if not with_kv:
        tm = _pick_tile(t, (256, 128))
        return pl.pallas_call(
            _qknorm_kernel, grid=(t // tm,),
            in_specs=[pl.BlockSpec((tm, W_BR), lambda i: (i, COL_QB)),
                      pl.BlockSpec((tm, W_BR), lambda i: (i, COL_KB)), gs, gs],
            out_specs=[pl.BlockSpec((tm, W_BR), lambda i: (i, 0))] * 2,
            out_shape=[jax.ShapeDtypeStruct((t, W_BR), F32)] * 2,
            compiler_params=_params("arbitrary"), name="qk_norm")(p, p, qg, kg)
    tm = MOBA_BLOCK
    nb = t // tm
    return pl.pallas_call(
        _qknorm_kv_kernel, grid=(nb,),
        in_specs=[pl.BlockSpec((tm, W_BR), lambda i: (i, COL_QB)),
                  pl.BlockSpec((tm, W_BR), lambda i: (i, COL_KB)),
                  pl.BlockSpec((tm, W_BR), lambda i: (i, COL_VB)), gs, gs],
        out_specs=[pl.BlockSpec((tm, W_BR), lambda i: (i, 0)), pl.BlockSpec((tm, W_BR), lambda i: (i, 0)),
                   pl.BlockSpec((1, tm, W_BR), lambda i: (i, 0, 0)),
                   pl.BlockSpec((1, W_BR, tm), lambda i: (i, 0, 0)),
                   pl.BlockSpec((1, 1, W_BR), lambda i: (i, 0, 0))],
        out_shape=[jax.ShapeDtypeStruct((t, W_BR), F32), jax.ShapeDtypeStruct((t, W_BR), F32),
                   jax.ShapeDtypeStruct((nb, tm, W_BR), BF16), jax.ShapeDtypeStruct((nb, W_BR, tm), BF16),
                   jax.ShapeDtypeStruct((nb, 1, W_BR), F32)],
        compiler_params=_params("arbitrary"), name="qk_norm_kv")(p, p, p, qg, kg)


def _select_topk(sc, valid, axis):
    nb = sc.shape[axis]
    idx = lax.broadcasted_iota(jnp.int32, sc.shape, axis).astype(F32)
    if valid is not None:
        sc = jnp.where(valid, sc, NEG_INF)
    sel = jnp.zeros(sc.shape, F32)
    for _ in range(MOBA_TOPK):
        mx = jnp.max(sc, axis=axis, keepdims=True)
        first = jnp.min(jnp.where(sc == mx, idx, float(nb)), axis=axis, keepdims=True)
        hit = idx == first
        sel = jnp.where(hit & (mx > NEG_INF), 1.0, sel)
        sc = jnp.where(hit, NEG_INF, sc)
    return sel


MOBA_HEADS_PER_STEP = 8


def _moba_prompt_kernel(q_ref, k_ref, vt_ref, kbar_ref, o_ref, sel_ref, qb_ref, acc_ref):
    i = pl.program_id(2)
    blk = MOBA_BLOCK
    hp = MOBA_HEADS_PER_STEP
    nb = kbar_ref.shape[1]
    past = lax.broadcasted_iota(jnp.int32, (nb, blk), 0) < i
    kpos = lax.broadcasted_iota(jnp.int32, (blk, blk), 0)
    qpos = lax.broadcasted_iota(jnp.int32, (blk, blk), 1)
    heads = [slice(hh * HD_B, (hh + 1) * HD_B) for hh in range(hp)]

    ones = jnp.ones((BF16_ROWS, blk), BF16)

    def pv(n, sl, pr):
        return _dot(jnp.concatenate([vt_ref[n, sl, :], ones], axis=0), pr)

    for hh, sl in enumerate(heads):
        qb_ref[hh] = (q_ref[:, sl] * LOG2E).astype(BF16)
    s_l = [_dot_nt(k_ref[i, :, sl], qb_ref[hh]) for hh, sl in enumerate(heads)]
    sc_l = [_dot_nt(kbar_ref[0, :, sl], q_ref[:, sl], HI) for sl in heads]
    ms, pr_l = [], []
    for hh in range(hp):
        s = jnp.where(kpos <= qpos, s_l[hh], NEG_INF)
        m = jnp.max(s, axis=0, keepdims=True)
        ms.append(m)
        pr_l.append(jnp.exp2(s - m).astype(BF16))
    for hh, sl in enumerate(heads):
        acc_ref[hh] = pv(i, sl, pr_l[hh])
        sel_ref[hh] = _select_topk(sc_l[hh], past, 0)

    def body(n, ms):
        s_l = [_dot_nt(k_ref[n, :, sl], qb_ref[hh]) for hh, sl in enumerate(heads)]
        ms_new, pr_l, alpha_l = [], [], []
        for hh in range(hp):
            s = jnp.where(sel_ref[hh, pl.ds(n, 1), :] > 0.0, s_l[hh], NEG_INF)
            m_new = jnp.maximum(ms[hh], jnp.max(s, axis=0, keepdims=True))
            alpha_l.append(jnp.exp2(ms[hh] - m_new))
            ms_new.append(m_new)
            pr_l.append(jnp.exp2(s - m_new).astype(BF16))
        for hh, sl in enumerate(heads):
            acc_ref[hh] = alpha_l[hh] * acc_ref[hh] + pv(n, sl, pr_l[hh])
        return tuple(ms_new)

    lax.fori_loop(0, i, body, tuple(ms))
    for hh, sl in enumerate(heads):
        o_ref[:, sl] = (acc_ref[hh, 0:HD_B, :] / acc_ref[hh, HD_B:HD_B + 1, :]).T.astype(o_ref.dtype)


def moba_prompt(qn, knb, vt, kbar, bsz, seq):
    nb = seq // MOBA_BLOCK
    blk = MOBA_BLOCK
    hp = MOBA_HEADS_PER_STEP
    wid = hp * HD_B
    return pl.pallas_call(
        _moba_prompt_kernel, grid=(bsz, H_B // hp, nb),
        in_specs=[pl.BlockSpec((blk, wid), lambda b, h, i: (b * nb + i, h)),
                  pl.BlockSpec((nb, blk, wid), lambda b, h, i: (b, 0, h)),
                  pl.BlockSpec((nb, wid, blk), lambda b, h, i: (b, h, 0)),
                  pl.BlockSpec((1, nb, wid), lambda b, h, i: (b, 0, h))],
        out_specs=pl.BlockSpec((blk, wid), lambda b, h, i: (b * nb + i, h)),
        out_shape=jax.ShapeDtypeStruct((bsz * seq, W_BR), _branch_dtype(blk)),
        scratch_shapes=[pltpu.VMEM((hp, nb, blk), F32), pltpu.VMEM((hp, blk, HD_B), BF16),
                        pltpu.VMEM((hp, HD_B + BF16_ROWS, blk), F32)],
        compiler_params=_params("arbitrary", "arbitrary", "arbitrary"), name="moba_prompt")(qn, knb, vt, kbar)


DECODE_PAGES_PER_STEP = 16


def _moba_decode_kernel(pt_ref, q_ref, qf_ref, bias_ref, obias_ref, kown_ref, vown_ref, *rest, pps, seq):
    del pt_ref
    k_refs = rest[:pps]
    v_refs = rest[pps:2 * pps]
    o_ref = rest[2 * pps]
    m_s, l_s, o_s, ks_s = rest[2 * pps + 1:]
    n = pl.program_id(1)
    q = q_ref[0]
    ppb = MOBA_BLOCK // PAGE_SIZE
    npg = ks_s.shape[0] // H_B
    lane = lax.broadcasted_iota(jnp.int32, m_s.shape, 1)

    def partial_softmax(s):
        m = jnp.max(s, axis=-1, keepdims=True)
        pr = jnp.exp(s - m)
        return m, jnp.sum(pr, axis=-1, keepdims=True), pr

    @pl.when(n == 0)
    def _init():
        m_s[...] = jnp.zeros(m_s.shape, F32)
        l_s[...] = jnp.zeros(l_s.shape, F32)

    m_all = m_s[...]
    l_all = l_s[...]
    s_l = [_dot_nt(q, k_refs[j][0, 0].astype(BF16)) for j in range(pps)]
    pr_l = []
    for j in range(pps):
        m, l, pr = partial_softmax(s_l[j] + bias_ref[...])
        pg = n * pps + j
        m_all = jnp.where(lane == pg, m, m_all)
        l_all = jnp.where(lane == pg, l, l_all)
        pr_l.append(pr.astype(BF16))
    for j in range(pps):
        pg = n * pps + j
        o_s[pg] = _dot(pr_l[j], v_refs[j][0, 0].astype(BF16))
        ks_s[pl.ds(pg * H_B, H_B), :] = jnp.sum(k_refs[j][0, 0].reshape(PAGE_SIZE, H_B, HD_B), axis=0)
    m_s[...] = m_all
    l_s[...] = l_all

    @pl.when(n == pl.num_programs(1) - 1)
    def _combine():
        nb = npg // ppb
        m_o, l_o, pr_o = partial_softmax(_dot_nt(q, kown_ref[0].astype(BF16)) + obias_ref[...])
        o_s[npg] = _dot(pr_o.astype(BF16), vown_ref[0].astype(BF16))
        m_sl = jnp.where(lane == npg, m_o, m_all)
        l_sl = jnp.where(lane == npg, l_o, l_all)
        scp = _dot_nt(qf_ref[0], ks_s[...], HI)
        ch = lax.broadcasted_iota(jnp.int32, scp.shape, 1) % H_B
        rh = lax.broadcasted_iota(jnp.int32, scp.shape, 0) // seq
        scp = jnp.where(ch == rh, scp, 0.0)
        cols_per_blk = ppb * H_B
        gather = (lax.broadcasted_iota(jnp.int32, (npg * H_B, nb), 0) // cols_per_blk
                  == lax.broadcasted_iota(jnp.int32, (npg * H_B, nb), 1)).astype(F32)
        scb = _dot(scp, gather, HI) * (1.0 / MOBA_BLOCK)
        sel = _select_topk(scb, None, 1)
        slot = lax.broadcasted_iota(jnp.int32, (nb, m_s.shape[1]), 1)
        expand = (slot // ppb == lax.broadcasted_iota(jnp.int32, (nb, m_s.shape[1]), 0)) & (slot < npg)
        sel_slot = jnp.where(lane == npg, 1.0, _dot(sel, expand.astype(F32), HI))
        m_sel = jnp.where(sel_slot > 0.0, m_sl, NEG_INF)
        w = sel_slot * jnp.exp(m_sel - jnp.max(m_sel, axis=-1, keepdims=True))
        w = w / jnp.sum(w * l_sl, axis=-1, keepdims=True)
        acc = w[:, 0:1] * o_s[0]
        for pg in range(1, npg + 1):
            acc = acc + w[:, pg:pg + 1] * o_s[pg]
        o_ref[0] = acc


def moba_decode(qn, kn, v, cache_k, cache_v, page_table, layer, bsz, seq):
    ppb = MOBA_BLOCK // PAGE_SIZE
    pps = DECODE_PAGES_PER_STEP
    n_pages = page_table.shape[1]
    nrow = H_B * seq
    assert nrow % SUBLANES == 0 and n_pages // ppb >= MOBA_TOPK and n_pages % ppb == 0
    assert n_pages % pps == 0 and n_pages < LANES
    rows = PAGE_SIZE * H_B
    depth, n_phys = cache_k.shape[:2]
    ck = cache_k.reshape(depth, n_phys, rows, HD_B)
    cv = cache_v.reshape(depth, n_phys, rows, HD_B)
    qc = jnp.transpose(qn.reshape(bsz, seq, H_B, HD_B), (0, 2, 1, 3)).reshape(bsz, nrow, HD_B)
    c_h, c_q = jnp.arange(nrow) // seq, jnp.arange(nrow) % seq
    key_h = jnp.arange(rows) % H_B
    bias = jnp.where(c_h[:, None] == key_h[None, :], 0.0, NEG_INF).astype(F32)
    own = jnp.arange(seq * H_B)
    obias = jnp.where((c_h[:, None] == own[None, :] % H_B) & (own[None, :] // H_B <= c_q[:, None]),
                      0.0, NEG_INF).astype(F32)
    kown = kn.reshape(bsz, seq * H_B, HD_B)
    vown = v.reshape(bsz, seq * H_B, HD_B)

    page = lambda j: pl.BlockSpec((1, 1, rows, HD_B), lambda b, n, pt, j=j: (layer, pt[b, n * pps + j], 0, 0))
    seq3 = lambda shape: pl.BlockSpec((1,) + shape, lambda b, n, pt: (b, 0, 0))
    const2 = lambda shape: pl.BlockSpec(shape, lambda b, n, pt: (0, 0))
    grid_spec = pltpu.PrefetchScalarGridSpec(
        num_scalar_prefetch=1, grid=(bsz, n_pages // pps),
        in_specs=[seq3((nrow, HD_B)), seq3((nrow, HD_B)), const2((nrow, rows)), const2((nrow, seq * H_B)),
                  seq3((seq * H_B, HD_B)), seq3((seq * H_B, HD_B))]
                 + [page(j) for j in range(pps)] + [page(j) for j in range(pps)],
        out_specs=seq3((nrow, HD_B)),
        scratch_shapes=[pltpu.VMEM((nrow, LANES), F32), pltpu.VMEM((nrow, LANES), F32),
                        pltpu.VMEM((n_pages + 1, nrow, HD_B), F32), pltpu.VMEM((n_pages * H_B, HD_B), F32)])
    out = pl.pallas_call(
        functools.partial(_moba_decode_kernel, pps=pps, seq=seq), grid_spec=grid_spec,
        out_shape=jax.ShapeDtypeStruct((bsz, nrow, HD_B), F32),
        compiler_params=_params("arbitrary", "arbitrary"), name="moba_decode")(
            page_table, qc.astype(BF16), qc, bias, obias, kown, vown, *([ck] * pps), *([cv] * pps))
    return jnp.transpose(out.reshape(bsz, H_B, seq, HD_B), (0, 2, 1, 3)).reshape(bsz * seq, W_BR)


def _mixer(x, layer, lw, states, st, bsz, seq, attend):
    (ln_g, w_t, conv_a, a_log, dt_bias, norm_a, qnorm_b, knorm_b, b_if, norm_c, wa, wb, wc, wo) = lw
    conv_buf, s_a, c_c, n_c, m_c = states
    h = rms_cast(x, ln_g)
    p = in_proj(h, w_t, layer)
    ps = small_proj(h, w_t, layer)
    oa, conv_new, s_new = delta_branch(p, ps, conv_a, conv_buf, a_log, dt_bias, s_a, st, norm_a, bsz, seq)
    ob, kn, vb = attend(p, qnorm_b, knorm_b)
    oc, c_new, n_new, m_new = mlstm_branch(p, ps, b_if, c_c, n_c, m_c, st, norm_c, bsz, seq)
    y = out_proj(x, merge(oa, ob, oc, p, wa, wb, wc), wo)
    kv_shape = (bsz, seq, H_B, HD_B)
    return y, (kn.reshape(kv_shape), vb.reshape(kv_shape), conv_new, s_new, c_new, n_new, m_new)


def _pad_states(conv, delta, c, n, m):
    conv = jnp.pad(conv, ((0, 0), (0, 0), (SUBLANES - (CONV_W - 1), 0), (0, 0)))
    m = jnp.pad(m, ((0, 0), (0, 0), (0, LANES - H_C)))[:, :, None, :]
    return conv, delta, c, n, m


def kernel(x_prompt, x_sample, cache_k, cache_v, page_table, state_conv_a, state_delta_a, state_mlstm_c,
           state_mlstm_n, state_mlstm_m, ln_g, w_in, conv_a, a_log, dt_bias, norm_a, qnorm_b, knorm_b,
           b_if, norm_c, w_br_a, w_br_b, w_br_c, w_out):
    bp, sp, d = x_prompt.shape
    bs, ss, _ = x_sample.shape
    depth = w_in.shape[0]
    yp = x_prompt.reshape(bp * sp, d)
    ys = x_sample.reshape(bs * ss, d)
    new_p = [[] for _ in range(7)]
    new_s = [[] for _ in range(7)]
    w_t = jnp.swapaxes(w_in, 1, 2)
    zeros = lambda *s: jnp.zeros(s, F32)
    states_p = _pad_states(zeros(1, bp, CONV_W - 1, 3 * W_BR), zeros(1, bp, H_A, DK_A, DK_A),
                           zeros(1, bp, H_C, DK_C, DK_C), zeros(1, bp, H_C, DK_C), zeros(1, bp, H_C))
    states_s = _pad_states(state_conv_a, state_delta_a, state_mlstm_c, state_mlstm_n, state_mlstm_m)
    for l in range(depth):
        lw = (ln_g[l], w_t, conv_a[l], a_log[l], dt_bias[l], norm_a[l], qnorm_b[l],
              knorm_b[l], b_if[l], norm_c[l], w_br_a[l].astype(BF16), w_br_b[l].astype(BF16),
              w_br_c[l].astype(BF16), w_out[l].astype(BF16))

        def attend_prompt(p, qg, kg):
            qn, kn, knb, vt, kbar = qk_norm(p, qg, kg, with_kv=True)
            ob = moba_prompt(qn, knb, vt, kbar.reshape(bp, sp // MOBA_BLOCK, W_BR), bp, sp)
            return ob, kn, p[:, COL_VB * W_BR:(COL_VB + 1) * W_BR]

        def attend_sample(p, qg, kg, l=l):
            qn, kn = qk_norm(p, qg, kg, with_kv=False)
            vb = p[:, COL_VB * W_BR:(COL_VB + 1) * W_BR]
            return moba_decode(qn, kn, vb, cache_k, cache_v, page_table, l, bs, ss), kn, vb

        yp, st_p = _mixer(yp, l, lw, states_p, 0, bp, sp, attend_prompt)
        ys, st_s = _mixer(ys, l, lw, states_s, l, bs, ss, attend_sample)
        for i in range(7):
            new_p[i].append(st_p[i])
            new_s[i].append(st_s[i])
    outs_p = [jnp.stack(t) for t in new_p]
    outs_s = [jnp.stack(t) for t in new_s]
    return (yp.reshape(bp, sp, d), ys.reshape(bs, ss, d), *outs_p, *outs_s)
```

```python
import functools

import jax
import jax.numpy as jnp
from jax import lax
from jax.experimental import pallas as pl
from jax.experimental.pallas import tpu as pltpu

F32 = jnp.float32
BF16 = jnp.bfloat16
HI = lax.Precision.HIGHEST
EPS = 1e-6
NEG_INF = float("-inf")

H_A, DK_A = 8, 128
H_B, HD_B = 8, 128
H_C, DK_C = 4, 256
CONV_W = 4
MOBA_BLOCK = 256
MOBA_TOPK = 3
PAGE_SIZE = 128
CHUNK = 64
W_BR = 1024
D_MODEL = 2048

LANES = 128
SUBLANES = 8
BF16_ROWS = 16
LOG2E = 1.4426950408889634
VMEM_LIMIT = 48 * 1024 * 1024

COL_QA, COL_KA, COL_VA, COL_ZA = 0, 1, 2, 3
COL_QB, COL_KB, COL_VB, COL_ZB = 4, 5, 6, 7
COL_QC, COL_KC, COL_VC, COL_ZC, COL_OC = 8, 9, 10, 11, 12
COL_GATE = 13
N_BIG = 19 * W_BR
OFF_SMALL_A = 4 * W_BR
OFF_SMALL_C = OFF_SMALL_A + 2 * H_A + 9 * W_BR
OFF_GATE = OFF_SMALL_C + 2 * H_C
WIN_A, WIN_C = OFF_SMALL_A // LANES, OFF_SMALL_C // LANES
LANE_BETA, LANE_ALPHA = 0, H_A
LANE_I = OFF_SMALL_C % LANES
LANE_F = LANE_I + H_C
SHIFT_A, SHIFT_BC, SHIFT_G = 0, 2 * H_A, 2 * H_A + 2 * H_C
FIRST_BC_BLOCK, FIRST_G_BLOCK = COL_QB, COL_GATE


def _params(*sem):
    return pltpu.CompilerParams(dimension_semantics=sem, vmem_limit_bytes=VMEM_LIMIT)


def _sigmoid(x):
    return 1.0 / (1.0 + jnp.exp(-x))


def _silu(x):
    return x * _sigmoid(x)


def _softplus(x):
    return jnp.maximum(x, 0.0) + jnp.log(1.0 + jnp.exp(-jnp.abs(x)))


def _dot(a, b, precision=None):
    return jnp.dot(a, b, precision=precision, preferred_element_type=F32)


def _dot_nt(a, b, precision=None):
    return lax.dot_general(a, b, (((1,), (1,)), ((), ())), precision=precision,
                           preferred_element_type=F32)


def _dot_tn(a, b, precision=None):
    return lax.dot_general(a, b, (((0,), (0,)), ((), ())), precision=precision,
                           preferred_element_type=F32)


def _cumsum_rows(x):
    rid = lax.broadcasted_iota(jnp.int32, x.shape, 0)
    d = 1
    while d < x.shape[0]:
        x = x + jnp.where(rid >= d, pltpu.roll(x, d, 0), 0.0)
        d *= 2
    return x


def _chunk_len(seq):
    return min(CHUNK, -(-seq // BF16_ROWS) * BF16_ROWS)


def _branch_dtype(block_rows):
    return BF16 if block_rows % BF16_ROWS == 0 else F32


def _pick_tile(n, candidates):
    for c in candidates:
        if n % c == 0:
            return c
    return n


def _rms_kernel(x_ref, g_ref, o_ref):
    x = x_ref[...]
    y = x * lax.rsqrt(jnp.mean(x * x, axis=-1, keepdims=True) + EPS)
    o_ref[...] = (y * g_ref[...]).astype(o_ref.dtype)


def rms_cast(x, g):
    t, d = x.shape
    tm = _pick_tile(t, (512, 256, 128))
    return pl.pallas_call(
        _rms_kernel, grid=(t // tm,),
        in_specs=[pl.BlockSpec((tm, d), lambda i: (i, 0)), pl.BlockSpec((1, d), lambda i: (0, 0))],
        out_specs=pl.BlockSpec((tm, d), lambda i: (i, 0)),
        out_shape=jax.ShapeDtypeStruct((t, d), BF16),
        compiler_params=_params("arbitrary"), name="rms_cast")(x, g.reshape(1, d))


def _in_proj_kernel(a_ref, a2_ref, w_ref, o_ref, o2_ref, wb_s):
    @pl.when(pl.program_id(1) == 0)
    def _first():
        wb_s[...] = w_ref[...].astype(BF16)
        o2_ref[...] = _dot_nt(a2_ref[...], wb_s[...])

    o_ref[...] = _dot_nt(a_ref[...], wb_s[...])


def in_proj(h, h2, w_t, layer):
    t, k = h.shape
    t2 = h2.shape[0]
    tm = _pick_tile(t, (1024, 768, 512, 256, 128))
    tn = W_BR

    def w_rows(j, i):
        shift = jnp.where(j < FIRST_BC_BLOCK, SHIFT_A // SUBLANES,
                          jnp.where(j < FIRST_G_BLOCK, SHIFT_BC // SUBLANES, SHIFT_G // SUBLANES))
        return layer, (j * (tn // SUBLANES) + shift) * SUBLANES, 0

    n_i = t // tm
    row = lambda j, i: jnp.where(j % 2 == 0, i, n_i - 1 - i)
    return pl.pallas_call(
        _in_proj_kernel, grid=(N_BIG // tn, n_i),
        in_specs=[pl.BlockSpec((tm, k), lambda j, i: (row(j, i), 0)),
                  pl.BlockSpec((t2, k), lambda j, i: (0, 0)),
                  pl.BlockSpec((None, pl.Element(tn), pl.Element(k)), w_rows)],
        out_specs=[pl.BlockSpec((tm, tn), lambda j, i: (row(j, i), j)),
                   pl.BlockSpec((t2, tn), lambda j, i: (0, j))],
        out_shape=[jax.ShapeDtypeStruct((t, N_BIG), F32), jax.ShapeDtypeStruct((t2, N_BIG), F32)],
        scratch_shapes=[pltpu.VMEM((tn, k), BF16)],
        compiler_params=_params("arbitrary", "arbitrary"), name="in_proj")(h, h2, w_t)


def _small_proj_kernel(a_ref, wa_ref, wc_ref, o_ref):
    a = a_ref[...]
    o_ref[:, 0:LANES] = _dot_nt(a, wa_ref[0].astype(BF16))
    o_ref[:, LANES:2 * LANES] = _dot_nt(a, wc_ref[0].astype(BF16))


def small_proj(h, w_t, layer):
    t, k = h.shape
    tm = _pick_tile(t, (1024, 512, 256, 128))
    win = lambda c: pl.BlockSpec((1, LANES, k), lambda i, c=c: (layer, c, 0))
    return pl.pallas_call(
        _small_proj_kernel, grid=(t // tm,),
        in_specs=[pl.BlockSpec((tm, k), lambda i: (i, 0)), win(WIN_A), win(WIN_C)],
        out_specs=pl.BlockSpec((tm, 2 * LANES), lambda i: (i, 0)),
        out_shape=jax.ShapeDtypeStruct((t, 2 * LANES), F32),
        compiler_params=_params("arbitrary"), name="small_proj")(h, w_t, w_t)


def _merge_kernel(oa_ref, ob_ref, oc_ref, zb_ref, ga_ref, gb_ref, gc_ref, wa_ref, wb_ref, wc_ref, o_ref):
    ob = ob_ref[...] * _silu(zb_ref[...])
    m = _sigmoid(ga_ref[...]) * _dot(oa_ref[...].astype(BF16), wa_ref[...])
    m += _sigmoid(gb_ref[...]) * _dot(ob.astype(BF16), wb_ref[...])
    m += _sigmoid(gc_ref[...]) * _dot(oc_ref[...].astype(BF16), wc_ref[...])
    o_ref[...] = m.astype(o_ref.dtype)


def merge(oa, ob, oc, p, wa, wb, wc):
    t = oa.shape[0]
    tm = _pick_tile(t, (512, 256, 128))
    tn = W_BR
    row = lambda j, i: (i, 0)
    return pl.pallas_call(
        _merge_kernel, grid=(D_MODEL // tn, t // tm),
        in_specs=[pl.BlockSpec((tm, W_BR), row), pl.BlockSpec((tm, W_BR), row), pl.BlockSpec((tm, W_BR), row),
                  pl.BlockSpec((tm, W_BR), lambda j, i: (i, COL_ZB)),
                  pl.BlockSpec((tm, tn), lambda j, i: (i, COL_GATE + j)),
                  pl.BlockSpec((tm, tn), lambda j, i: (i, COL_GATE + 2 + j)),
                  pl.BlockSpec((tm, tn), lambda j, i: (i, COL_GATE + 4 + j)),
                  pl.BlockSpec((W_BR, tn), lambda j, i: (0, j)),
                  pl.BlockSpec((W_BR, tn), lambda j, i: (0, j)),
                  pl.BlockSpec((W_BR, tn), lambda j, i: (0, j))],
        out_specs=pl.BlockSpec((tm, tn), lambda j, i: (i, j)),
        out_shape=jax.ShapeDtypeStruct((t, D_MODEL), BF16),
        compiler_params=_params("arbitrary", "arbitrary"), name="merge")(oa, ob, oc, p, p, p, p, wa, wb, wc)


def _outproj_kernel(x_ref, m_ref, w_ref, o_ref):
    o_ref[...] = x_ref[...] + _dot(m_ref[...], w_ref[...])


def out_proj(x, m, w):
    t, d = x.shape
    tm = _pick_tile(t, (512, 256, 128))
    tn = 1024
    return pl.pallas_call(
        _outproj_kernel, grid=(d // tn, t // tm),
        in_specs=[pl.BlockSpec((tm, tn), lambda j, i: (i, j)), pl.BlockSpec((tm, d), lambda j, i: (i, 0)),
                  pl.BlockSpec((d, tn), lambda j, i: (0, j))],
        out_specs=pl.BlockSpec((tm, tn), lambda j, i: (i, j)),
        out_shape=jax.ShapeDtypeStruct((t, d), F32),
        compiler_params=_params("arbitrary", "arbitrary"), name="out_proj")(x, m, w)


def _bdot(a, b):
    return _dot(a.astype(BF16), b.astype(BF16))


DELTA_CHUNKS_PER_STEP = 4


def _delta_kernel(q_ref, k_ref, v_ref, z_ref, sm_ref, cw_ref, cb_ref, al_ref, dtb_ref, s0_ref, ng_ref,
                  o_ref, cn_ref, sn_ref, xp_ref, s_ref, *, c_in, chunk):
    c = pl.program_id(1)
    C = chunk
    hd = DK_A

    @pl.when(c == 0)
    def _init():
        xp_ref[...] = jnp.zeros(xp_ref.shape, F32)
        xp_ref[0:SUBLANES, :] = cb_ref[0]
        s_ref[...] = s0_ref[0]

    xp_ref[SUBLANES:SUBLANES + c_in, 0:W_BR] = q_ref[...]
    xp_ref[SUBLANES:SUBLANES + c_in, W_BR:2 * W_BR] = k_ref[...]
    xp_ref[SUBLANES:SUBLANES + c_in, 2 * W_BR:3 * W_BR] = v_ref[...]

    nch = max(c_in // C, 1)
    padded = c_in < C
    valid = lax.broadcasted_iota(jnp.int32, (C, 1), 0) < c_in
    mask = (lambda x, fill: jnp.where(valid, x, fill)) if padded else (lambda x, fill: x)
    ri = lax.broadcasted_iota(jnp.int32, (C, C), 0)
    ci = lax.broadcasted_iota(jnp.int32, (C, C), 1)
    causal = ci <= ri
    strict = ci < ri

    beta_c, gcum_c, gcum_t_c, e_g_c, e_end_c, e_last_c = [], [], [], [], [], []
    for kc in range(nch):
        if padded:
            sm = jnp.concatenate([sm_ref[...], jnp.zeros((C - c_in, LANES), F32)], axis=0)
        else:
            sm = sm_ref[kc * C:(kc + 1) * C, :]
        g = mask(-jnp.exp(al_ref[...]) * _softplus(sm + dtb_ref[...]), 0.0)
        gcum = _cumsum_rows(g)
        g_last = gcum[C - 1:C, :]
        beta_c.append(_sigmoid(sm))
        gcum_c.append(gcum)
        gcum_t_c.append(gcum.T)
        e_g_c.append(jnp.exp(gcum))
        e_end_c.append(jnp.exp(g_last - gcum))
        e_last_c.append(jnp.exp(g_last))

    def conv(kc, col):
        sl = slice(col, col + hd)
        r0 = 5 + kc * C
        y = xp_ref[r0:r0 + C, sl] * cw_ref[0:1, sl]
        for j in range(1, CONV_W):
            y = y + xp_ref[r0 + j:r0 + j + C, sl] * cw_ref[j:j + 1, sl]
        return mask(_silu(y), 0.0)

    heads = range(H_A)
    pairs = [(kc, h) for kc in range(nch) for h in heads]
    npair = range(len(pairs))
    col = lambda vec, kc, lane: vec[kc][:, lane:lane + 1]
    q_l = [conv(kc, h * hd) for kc, h in pairs]
    k_l = [conv(kc, W_BR + h * hd) for kc, h in pairs]
    v_l = [conv(kc, 2 * W_BR + h * hd) for kc, h in pairs]
    q_l = [q * lax.rsqrt(jnp.sum(q * q, axis=-1, keepdims=True) + EPS) * (DK_A ** -0.5) for q in q_l]
    k_l = [k * lax.rsqrt(jnp.sum(k * k, axis=-1, keepdims=True) + EPS) for k in k_l]
    decay_l = [jnp.exp(jnp.where(causal, col(gcum_c, kc, LANE_ALPHA + h)
                                 - gcum_t_c[kc][LANE_ALPHA + h:LANE_ALPHA + h + 1, :], NEG_INF))
               for kc, h in pairs]
    b_l = [col(beta_c, kc, LANE_BETA + h) for kc, h in pairs]
    eg_l = [col(e_g_c, kc, LANE_ALPHA + h) for kc, h in pairs]
    kb_l = [k.astype(BF16) for k in k_l]
    qk_kk = [_dot_nt(jnp.concatenate([q_l[i].astype(BF16), kb_l[i]], axis=0), kb_l[i]) for i in npair]
    a_l = [jnp.where(strict, b_l[i] * decay_l[i] * qk_kk[i][C:2 * C], 0.0) for i in npair]
    r_l = [-a for a in a_l]
    pw_l = a_l
    n = 2
    while n < C:
        pw_l = [_bdot(pw, pw) for pw in pw_l]
        r_l = [r_l[i] + pw_l[i] + _bdot(r_l[i], pw_l[i]) for i in npair]
        n *= 2
    rhs_l = [jnp.concatenate([b_l[i] * v_l[i], (b_l[i] * eg_l[i]) * k_l[i]], axis=1) for i in npair]
    sol_l = [rhs_l[i] + _bdot(r_l[i], rhs_l[i]) for i in npair]
    kq_l = [jnp.concatenate([sol_l[i][:, hd:2 * hd], q_l[i] * eg_l[i]], axis=0).astype(BF16) for i in npair]
    p_l = [(decay_l[i] * qk_kk[i][0:C]).astype(BF16) for i in npair]
    kend_l = [(k_l[i] * col(e_end_c, kc, LANE_ALPHA + h)).astype(BF16) for i, (kc, h) in enumerate(pairs)]

    s_l = [s_ref[h] for h in heads]
    o_l = []
    for kc in range(nch):
        idx = [kc * H_A + h for h in heads]
        kq_s = [_dot(kq_l[i], s_l[h].astype(BF16)) for h, i in enumerate(idx)]
        wb_l = [(sol_l[i][:, 0:hd] - kq_s[h][0:C]).astype(BF16) for h, i in enumerate(idx)]
        o_l += [kq_s[h][C:2 * C] + _dot(p_l[i], wb_l[h]) for h, i in enumerate(idx)]
        s_l = [col(e_last_c, kc, LANE_ALPHA + h) * s_l[h] + _dot_tn(kend_l[i], wb_l[h])
               for h, i in enumerate(idx)]
    for h in heads:
        s_ref[h] = s_l[h]
    for i, (kc, h) in enumerate(pairs):
        o = o_l[i]
        o = o * lax.rsqrt(jnp.mean(o * o, axis=-1, keepdims=True) + EPS) * ng_ref[...]
        rows = slice(0, c_in) if padded else slice(kc * C, (kc + 1) * C)
        o = o[0:min(c_in, C)] * _silu(z_ref[rows, h * hd:(h + 1) * hd])
        o_ref[rows, h * hd:(h + 1) * hd] = o.astype(o_ref.dtype)

    tail = xp_ref[c_in:c_in + SUBLANES, :]
    xp_ref[0:SUBLANES, :] = tail

    @pl.when(c == pl.num_programs(1) - 1)
    def _fin():
        cn_ref[0] = tail
        sn_ref[0] = s_ref[...]


def delta_branch(p, ps, conv_w, conv_buf, a_log, dt_bias, s0, st, norm_g, bsz, seq):
    chunk = _chunk_len(seq)
    c_in = min(seq, DELTA_CHUNKS_PER_STEP * chunk)
    assert seq % c_in == 0 and (c_in % chunk == 0 or c_in < chunk)
    n_chunks = seq // c_in
    lane_vec = lambda v: jnp.zeros((1, LANES), F32).at[0, LANE_ALPHA:LANE_ALPHA + H_A].set(v)
    rb = lambda col: pl.BlockSpec((c_in, W_BR), lambda b, c, col=col: (b * n_chunks + c, col))
    const2 = lambda b, c: (0, 0)
    o, cn, sn = pl.pallas_call(
        functools.partial(_delta_kernel, c_in=c_in, chunk=chunk), grid=(bsz, n_chunks),
        in_specs=[rb(COL_QA), rb(COL_KA), rb(COL_VA), rb(COL_ZA),
                  pl.BlockSpec((c_in, LANES), lambda b, c: (b * n_chunks + c, 0)),
                  pl.BlockSpec((CONV_W, 3 * W_BR), const2),
                  pl.BlockSpec((None, 1, SUBLANES, 3 * W_BR), lambda b, c: (st, b, 0, 0)),
                  pl.BlockSpec((1, LANES), const2), pl.BlockSpec((1, LANES), const2),
                  pl.BlockSpec((None, 1, H_A, DK_A, DK_A), lambda b, c: (st, b, 0, 0, 0)),
                  pl.BlockSpec((1, DK_A), const2)],
        out_specs=[pl.BlockSpec((c_in, W_BR), lambda b, c: (b * n_chunks + c, 0)),
                   pl.BlockSpec((1, SUBLANES, 3 * W_BR), lambda b, c: (b, 0, 0)),
                   pl.BlockSpec((1, H_A, DK_A, DK_A), lambda b, c: (b, 0, 0, 0))],
        out_shape=[jax.ShapeDtypeStruct((bsz * seq, W_BR), _branch_dtype(c_in)),
                   jax.ShapeDtypeStruct((bsz, SUBLANES, 3 * W_BR), F32),
                   jax.ShapeDtypeStruct((bsz, H_A, DK_A, DK_A), F32)],
        scratch_shapes=[pltpu.VMEM((SUBLANES + max(c_in, chunk), 3 * W_BR), F32),
                        pltpu.VMEM((H_A, DK_A, DK_A), F32)],
        compiler_params=_params("arbitrary", "arbitrary"), name="delta")(
            p, p, p, p, ps, conv_w, conv_buf, lane_vec(a_log), lane_vec(dt_bias), s0, norm_g.reshape(1, DK_A))
    return o, cn[:, SUBLANES - (CONV_W - 1):], sn


MLSTM_CHUNKS_PER_STEP = 4


def _mlstm_kernel(q_ref, k_ref, v_ref, z_ref, og_ref, sm_ref, bif_ref, c0_ref, n0_ref, m0_ref, ng_ref,
                  o_ref, cn_ref, nn_ref, mn_ref, c_s, n_s, m_s, *, c_in, chunk):
    c = pl.program_id(1)
    C = chunk
    hd = DK_C

    @pl.when(c == 0)
    def _init():
        c_s[...] = c0_ref[0]
        n_s[...] = n0_ref[0]
        m_s[...] = m0_ref[0]

    nch = max(c_in // C, 1)
    padded = c_in < C

    def rows_of(ref, kc, sl):
        if padded:
            return jnp.concatenate([ref[:, sl], jnp.zeros((C - c_in, sl.stop - sl.start), F32)], axis=0)
        return ref[kc * C:(kc + 1) * C, sl]

    valid = lax.broadcasted_iota(jnp.int32, (C, 1), 0) < c_in
    ri = lax.broadcasted_iota(jnp.int32, (C, C), 0)
    ci = lax.broadcasted_iota(jnp.int32, (C, C), 1)
    causal = ci <= ri
    m_old = m_s[...]
    lane = lax.broadcasted_iota(jnp.int32, m_old.shape, 1)
    heads = range(H_C)
    hsl = [slice(h * hd, (h + 1) * hd) for h in heads]
    pairs = [(kc, h) for kc in range(nch) for h in heads]

    i_c, i_t_c, b_c, b_t_c = [], [], [], []
    for kc in range(nch):
        pre = rows_of(sm_ref, kc, slice(0, LANES)) + bif_ref[...]
        i_pre = jnp.where(valid, pre, NEG_INF) if padded else pre
        log_f = -_softplus(-pre)
        if padded:
            log_f = jnp.where(valid, log_f, 0.0)
        bcum = _dot(causal.astype(F32), log_f, HI)
        i_c.append(i_pre)
        i_t_c.append(i_pre.T)
        b_c.append(bcum)
        b_t_c.append(bcum.T)

    q_l = [rows_of(q_ref, kc, hsl[h]) for kc, h in pairs]
    ks_l = [rows_of(k_ref, kc, hsl[h]) * (DK_C ** -0.5) for kc, h in pairs]
    vb_l = [rows_of(v_ref, kc, hsl[h]).astype(BF16) for kc, h in pairs]
    qb_l = [q.astype(BF16) for q in q_l]
    qk_l = [_dot_nt(qb_l[i], ks_l[i].astype(BF16)) for i in range(len(pairs))]
    bcol_l = [b_c[kc][:, LANE_F + h:LANE_F + h + 1] for kc, h in pairs]
    icol_l = [i_c[kc][:, LANE_I + h:LANE_I + h + 1] for kc, h in pairs]
    intra_l = [jnp.where(causal, bcol_l[i] - b_t_c[kc][LANE_F + h:LANE_F + h + 1, :]
                         + i_t_c[kc][LANE_I + h:LANE_I + h + 1, :], NEG_INF) for i, (kc, h) in enumerate(pairs)]
    imax_l = [jnp.max(x, axis=-1, keepdims=True) for x in intra_l]
    m_prev = [m_old[:, h:h + 1] for h in heads]
    mt_l, winter_l, wc_l, wj_l = [], [], [], []
    for i, (kc, h) in enumerate(pairs):
        inter = bcol_l[i] + m_prev[h]
        m_t = jnp.maximum(inter, imax_l[i])
        m_new = m_t[C - 1:C, :]
        b_last = bcol_l[i][C - 1:C, :]
        mt_l.append(m_t)
        winter_l.append(jnp.exp(inter - m_t))
        wc_l.append(jnp.exp(b_last + m_prev[h] - m_new))
        wj_l.append(jnp.exp(b_last - bcol_l[i] + icol_l[i] - m_new))
        m_prev[h] = m_new
    m_new_vec = m_old
    for h in heads:
        m_new_vec = jnp.where(lane == h, m_prev[h], m_new_vec)
    m_s[...] = m_new_vec

    s_l = [qk_l[i] * jnp.exp(intra_l[i] - mt_l[i]) for i in range(len(pairs))]
    sv_l = [_dot(s_l[i].astype(BF16), vb_l[i]) for i in range(len(pairs))]
    ssum_l = [jnp.sum(s, axis=-1, keepdims=True) for s in s_l]
    kw_l = [wj_l[i] * ks_l[i] for i in range(len(pairs))]
    upd_l = [_dot_tn(kw_l[i].astype(BF16), vb_l[i]) for i in range(len(pairs))]
    nsum_l = [jnp.sum(kw, axis=0, keepdims=True) for kw in kw_l]

    cm_l = [c_s[h] for h in heads]
    n_l = [n_s[h:h + 1, :] for h in heads]
    for i, (kc, h) in enumerate(pairs):
        num = winter_l[i] * _dot(qb_l[i], cm_l[h].astype(BF16)) + sv_l[i]
        den = winter_l[i] * jnp.sum(q_l[i] * n_l[h], axis=-1, keepdims=True) + ssum_l[i]
        hh = num / jnp.maximum(jnp.abs(den), jnp.exp(-mt_l[i]))
        cm_l[h] = wc_l[i] * cm_l[h] + upd_l[i]
        n_l[h] = wc_l[i] * n_l[h] + nsum_l[i]
        rows = slice(0, c_in) if padded else slice(kc * C, (kc + 1) * C)
        hc = _sigmoid(og_ref[rows, hsl[h]]) * hh[0:min(c_in, C)]
        hc = hc * lax.rsqrt(jnp.mean(hc * hc, axis=-1, keepdims=True) + EPS) * ng_ref[...]
        o_ref[rows, hsl[h]] = (hc * _silu(z_ref[rows, hsl[h]])).astype(o_ref.dtype)
    for h in heads:
        c_s[h] = cm_l[h]
        n_s[h:h + 1, :] = n_l[h]

    @pl.when(c == pl.num_programs(1) - 1)
    def _fin():
        cn_ref[0] = c_s[...]
        nn_ref[0] = n_s[...]
        mn_ref[0] = m_new_vec


def mlstm_branch(p, ps, b_if, c0, n0, m0p, st, norm_g, bsz, seq):
    chunk = _chunk_len(seq)
    c_in = min(seq, MLSTM_CHUNKS_PER_STEP * chunk)
    assert seq % c_in == 0 and (c_in % chunk == 0 or c_in < chunk)
    n_chunks = seq // c_in
    bif = jnp.zeros((1, LANES), F32).at[0, LANE_I:LANE_I + 2 * H_C].set(b_if)
    rb = lambda col: pl.BlockSpec((c_in, W_BR), lambda b, c, col=col: (b * n_chunks + c, col))
    const2 = lambda b, c: (0, 0)
    st4 = pl.BlockSpec((1, H_C, DK_C, DK_C), lambda b, c: (b, 0, 0, 0))
    st3 = pl.BlockSpec((1, H_C, DK_C), lambda b, c: (b, 0, 0))
    stm = pl.BlockSpec((1, 1, LANES), lambda b, c: (b, 0, 0))
    in4 = pl.BlockSpec((None, 1, H_C, DK_C, DK_C), lambda b, c: (st, b, 0, 0, 0))
    in3 = pl.BlockSpec((None, 1, H_C, DK_C), lambda b, c: (st, b, 0, 0))
    inm = pl.BlockSpec((None, 1, 1, LANES), lambda b, c: (st, b, 0, 0))
    o, cn, nn, mn = pl.pallas_call(
        functools.partial(_mlstm_kernel, c_in=c_in, chunk=chunk), grid=(bsz, n_chunks),
        in_specs=[rb(COL_QC), rb(COL_KC), rb(COL_VC), rb(COL_ZC), rb(COL_OC),
                  pl.BlockSpec((c_in, LANES), lambda b, c: (b * n_chunks + c, 1)),
                  pl.BlockSpec((1, LANES), const2), in4, in3, inm, pl.BlockSpec((1, DK_C), const2)],
        out_specs=[pl.BlockSpec((c_in, W_BR), lambda b, c: (b * n_chunks + c, 0)), st4, st3, stm],
        out_shape=[jax.ShapeDtypeStruct((bsz * seq, W_BR), _branch_dtype(c_in)),
                   jax.ShapeDtypeStruct((bsz, H_C, DK_C, DK_C), F32),
                   jax.ShapeDtypeStruct((bsz, H_C, DK_C), F32),
                   jax.ShapeDtypeStruct((bsz, 1, LANES), F32)],
        scratch_shapes=[pltpu.VMEM((H_C, DK_C, DK_C), F32), pltpu.VMEM((H_C, DK_C), F32),
                        pltpu.VMEM((1, LANES), F32)],
        compiler_params=_params("arbitrary", "arbitrary"), name="mlstm")(
            p, p, p, p, p, ps, bif, c0, n0, m0p, norm_g.reshape(1, DK_C))
    return o, cn, nn, mn[:, 0, :H_C]


def _head_rms(x, g):
    return x * lax.rsqrt(jnp.mean(x * x, axis=-1, keepdims=True) + EPS) * g


def _qknorm_kernel(q_ref, k_ref, qg_ref, kg_ref, qn_ref, kn_ref):
    for h in range(H_B):
        sl = slice(h * HD_B, (h + 1) * HD_B)
        qn_ref[:, sl] = _head_rms(q_ref[:, sl], qg_ref[...]) * (HD_B ** -0.5)
        kn_ref[:, sl] = _head_rms(k_ref[:, sl], kg_ref[...])


def _qknorm_kv_kernel(q_ref, k_ref, v_ref, qg_ref, kg_ref, qn_ref, kn_ref, knb_ref, vt_ref, kbar_ref):
    for h in range(H_B):
        sl = slice(h * HD_B, (h + 1) * HD_B)
        qn_ref[:, sl] = _head_rms(q_ref[:, sl], qg_ref[...]) * (HD_B ** -0.5)
        kn = _head_rms(k_ref[:, sl], kg_ref[...])
        kn_ref[:, sl] = kn
        knb_ref[0, :, sl] = kn.astype(BF16)
        kbar_ref[0, :, sl] = jnp.mean(kn, axis=0, keepdims=True)
    eye = (lax.broadcasted_iota(jnp.int32, (HD_B, HD_B), 0)
           == lax.broadcasted_iota(jnp.int32, (HD_B, HD_B), 1)).astype(BF16)
    for h in range(H_B):
        sl = slice(h * HD_B, (h + 1) * HD_B)
        vt_ref[0, sl, :] = _dot_nt(eye, v_ref[:, sl].astype(BF16)).astype(BF16)


def qk_norm(p, qg, kg, with_kv):
    t = p.shape[0]
    qg = qg.reshape(1, HD_B)
    kg = kg.reshape(1, HD_B)
    gs = pl.BlockSpec((1, HD_B), lambda i: (0, 0))
    if not with_kv:
        tm = _pick_tile(t, (256, 128))
        return pl.pallas_call(
            _qknorm_kernel, grid=(t // tm,),
            in_specs=[pl.BlockSpec((tm, W_BR), lambda i: (i, COL_QB)),
                      pl.BlockSpec((tm, W_BR), lambda i: (i, COL_KB)), gs, gs],
            out_specs=[pl.BlockSpec((tm, W_BR), lambda i: (i, 0))] * 2,
            out_shape=[jax.ShapeDtypeStruct((t, W_BR), F32)] * 2,
            compiler_params=_params("arbitrary"), name="qk_norm")(p, p, qg, kg)
    tm = MOBA_BLOCK
    nb = t // tm
    return pl.pallas_call(
        _qknorm_kv_kernel, grid=(nb,),
        in_specs=[pl.BlockSpec((tm, W_BR), lambda i: (i, COL_QB)),
                  pl.BlockSpec((tm, W_BR), lambda i: (i, COL_KB)),
                  pl.BlockSpec((tm, W_BR), lambda i: (i, COL_VB)), gs, gs],
        out_specs=[pl.BlockSpec((tm, W_BR), lambda i: (i, 0)), pl.BlockSpec((tm, W_BR), lambda i: (i, 0)),
                   pl.BlockSpec((1, tm, W_BR), lambda i: (i, 0, 0)),
                   pl.BlockSpec((1, W_BR, tm), lambda i: (i, 0, 0)),
                   pl.BlockSpec((1, 1, W_BR), lambda i: (i, 0, 0))],
        out_shape=[jax.ShapeDtypeStruct((t, W_BR), F32), jax.ShapeDtypeStruct((t, W_BR), F32),
                   jax.ShapeDtypeStruct((nb, tm, W_BR), BF16), jax.ShapeDtypeStruct((nb, W_BR, tm), BF16),
                   jax.ShapeDtypeStruct((nb, 1, W_BR), F32)],
        compiler_params=_params("arbitrary"), name="qk_norm_kv")(p, p, p, qg, kg)


def _select_topk(sc, valid, axis):
    nb = sc.shape[axis]
    idx = lax.broadcasted_iota(jnp.int32, sc.shape, axis).astype(F32)
    if valid is not None:
        sc = jnp.where(valid, sc, NEG_INF)
    sel = jnp.zeros(sc.shape, F32)
    for _ in range(MOBA_TOPK):
        mx = jnp.max(sc, axis=axis, keepdims=True)
        first = jnp.min(jnp.where(sc == mx, idx, float(nb)), axis=axis, keepdims=True)
        hit = idx == first
        sel = jnp.where(hit & (mx > NEG_INF), 1.0, sel)
        sc = jnp.where(hit, NEG_INF, sc)
    return sel


MOBA_HEADS_PER_STEP = 8


def _moba_prompt_kernel(q_ref, k_ref, vt_ref, kbar_ref, o_ref, sel_ref, qb_ref, acc_ref):
    i = pl.program_id(2)
    blk = MOBA_BLOCK
    hp = MOBA_HEADS_PER_STEP
    nb = kbar_ref.shape[1]
    past = lax.broadcasted_iota(jnp.int32, (nb, blk), 0) < i
    kpos = lax.broadcasted_iota(jnp.int32, (blk, blk), 0)
    qpos = lax.broadcasted_iota(jnp.int32, (blk, blk), 1)
    heads = [slice(hh * HD_B, (hh + 1) * HD_B) for hh in range(hp)]

    ones = jnp.ones((BF16_ROWS, blk), BF16)

    def pv(n, sl, pr):
        return _dot(jnp.concatenate([vt_ref[n, sl, :], ones], axis=0), pr)

    for hh, sl in enumerate(heads):
        qb_ref[hh] = (q_ref[:, sl] * LOG2E).astype(BF16)
    s_l = [_dot_nt(k_ref[i, :, sl], qb_ref[hh]) for hh, sl in enumerate(heads)]
    sc_l = [_dot_nt(kbar_ref[0, :, sl], q_ref[:, sl], HI) for sl in heads]
    ms, pr_l = [], []
    for hh in range(hp):
        s = jnp.where(kpos <= qpos, s_l[hh], NEG_INF)
        m = jnp.max(s, axis=0, keepdims=True)
        ms.append(m)
        pr_l.append(jnp.exp2(s - m).astype(BF16))
    for hh, sl in enumerate(heads):
        acc_ref[hh] = pv(i, sl, pr_l[hh])
        sel_ref[hh] = _select_topk(sc_l[hh], past, 0)

    def past_blocks(ns, ms):
        s_l = [[_dot_nt(k_ref[n, :, sl], qb_ref[hh]) for n in ns] for hh, sl in enumerate(heads)]
        ones_n = jnp.ones((BF16_ROWS, len(ns) * blk), BF16)
        ms_new, pr_l, alpha_l = [], [], []
        for hh in range(hp):
            ss = [jnp.where(sel_ref[hh, pl.ds(n, 1), :] > 0.0, s, NEG_INF) for n, s in zip(ns, s_l[hh])]
            m_new = ms[hh]
            for s in ss:
                m_new = jnp.maximum(m_new, jnp.max(s, axis=0, keepdims=True))
            alpha_l.append(jnp.exp2(ms[hh] - m_new))
            ms_new.append(m_new)
            pr_l.append(jnp.concatenate([jnp.exp2(s - m_new).astype(BF16) for s in ss], axis=0))
        for hh, sl in enumerate(heads):
            lhs = jnp.concatenate([jnp.concatenate([vt_ref[n, sl, :] for n in ns], axis=1), ones_n], axis=0)
            acc_ref[hh] = alpha_l[hh] * acc_ref[hh] + _dot(lhs, pr_l[hh])
        return tuple(ms_new)

    odd = i % 2
    ms = lax.cond(odd == 1, lambda ms: past_blocks([0], ms), lambda ms: ms, tuple(ms))
    lax.fori_loop(0, i // 2, lambda t, ms: past_blocks([2 * t + odd, 2 * t + odd + 1], ms), ms)
    for hh, sl in enumerate(heads):
        o_ref[:, sl] = (acc_ref[hh, 0:HD_B, :] / acc_ref[hh, HD_B:HD_B + 1, :]).T.astype(o_ref.dtype)


def moba_prompt(qn, knb, vt, kbar, bsz, seq):
    nb = seq // MOBA_BLOCK
    blk = MOBA_BLOCK
    hp = MOBA_HEADS_PER_STEP
    wid = hp * HD_B
    return pl.pallas_call(
        _moba_prompt_kernel, grid=(bsz, H_B // hp, nb),
        in_specs=[pl.BlockSpec((blk, wid), lambda b, h, i: (b * nb + i, h)),
                  pl.BlockSpec((nb, blk, wid), lambda b, h, i: (b, 0, h)),
                  pl.BlockSpec((nb, wid, blk), lambda b, h, i: (b, h, 0)),
                  pl.BlockSpec((1, nb, wid), lambda b, h, i: (b, 0, h))],
        out_specs=pl.BlockSpec((blk, wid), lambda b, h, i: (b * nb + i, h)),
        out_shape=jax.ShapeDtypeStruct((bsz * seq, W_BR), _branch_dtype(blk)),
        scratch_shapes=[pltpu.VMEM((hp, nb, blk), F32), pltpu.VMEM((hp, blk, HD_B), BF16),
                        pltpu.VMEM((hp, HD_B + BF16_ROWS, blk), F32)],
        compiler_params=_params("arbitrary", "arbitrary", "arbitrary"), name="moba_prompt")(qn, knb, vt, kbar)


DECODE_PAGES_PER_STEP = 16


def _moba_decode_kernel(pt_ref, q_ref, qf_ref, bias_ref, obias_ref, kown_ref, vown_ref, *rest, pps, seq):
    del pt_ref
    k_refs = rest[:pps]
    v_refs = rest[pps:2 * pps]
    o_ref = rest[2 * pps]
    m_s, l_s, o_s, ks_s = rest[2 * pps + 1:]
    n = pl.program_id(1)
    q = q_ref[0]
    ppb = MOBA_BLOCK // PAGE_SIZE
    npg = ks_s.shape[0] // H_B
    lane = lax.broadcasted_iota(jnp.int32, m_s.shape, 1)

    def partial_softmax(s):
        m = jnp.max(s, axis=-1, keepdims=True)
        pr = jnp.exp(s - m)
        return m, jnp.sum(pr, axis=-1, keepdims=True), pr

    @pl.when(n == 0)
    def _init():
        m_s[...] = jnp.zeros(m_s.shape, F32)
        l_s[...] = jnp.zeros(l_s.shape, F32)

    m_all = m_s[...]
    l_all = l_s[...]
    s_l = [_dot_nt(q, k_refs[j][0, 0].astype(BF16)) for j in range(pps)]
    pr_l = []
    for j in range(pps):
        m, l, pr = partial_softmax(s_l[j] + bias_ref[...])
        pg = n * pps + j
        m_all = jnp.where(lane == pg, m, m_all)
        l_all = jnp.where(lane == pg, l, l_all)
        pr_l.append(pr.astype(BF16))
    for j in range(pps):
        pg = n * pps + j
        o_s[pg] = _dot(pr_l[j], v_refs[j][0, 0].astype(BF16))
        ks_s[pl.ds(pg * H_B, H_B), :] = jnp.sum(k_refs[j][0, 0].reshape(PAGE_SIZE, H_B, HD_B), axis=0)
    m_s[...] = m_all
    l_s[...] = l_all

    @pl.when(n == pl.num_programs(1) - 1)
    def _combine():
        nb = npg // ppb
        m_o, l_o, pr_o = partial_softmax(_dot_nt(q, kown_ref[0].astype(BF16)) + obias_ref[...])
        o_s[npg] = _dot(pr_o.astype(BF16), vown_ref[0].astype(BF16))
        m_sl = jnp.where(lane == npg, m_o, m_all)
        l_sl = jnp.where(lane == npg, l_o, l_all)
        scp = _dot_nt(qf_ref[0], ks_s[...], HI)
        ch = lax.broadcasted_iota(jnp.int32, scp.shape, 1) % H_B
        rh = lax.broadcasted_iota(jnp.int32, scp.shape, 0) // seq
        scp = jnp.where(ch == rh, scp, 0.0)
        cols_per_blk = ppb * H_B
        gather = (lax.broadcasted_iota(jnp.int32, (npg * H_B, nb), 0) // cols_per_blk
                  == lax.broadcasted_iota(jnp.int32, (npg * H_B, nb), 1)).astype(F32)
        scb = _dot(scp, gather, HI) * (1.0 / MOBA_BLOCK)
        sel = _select_topk(scb, None, 1)
        slot = lax.broadcasted_iota(jnp.int32, (nb, m_s.shape[1]), 1)
        expand = (slot // ppb == lax.broadcasted_iota(jnp.int32, (nb, m_s.shape[1]), 0)) & (slot < npg)
        sel_slot = jnp.where(lane == npg, 1.0, _dot(sel, expand.astype(F32), HI))
        m_sel = jnp.where(sel_slot > 0.0, m_sl, NEG_INF)
        w = sel_slot * jnp.exp(m_sel - jnp.max(m_sel, axis=-1, keepdims=True))
        w = w / jnp.sum(w * l_sl, axis=-1, keepdims=True)
        acc = w[:, 0:1] * o_s[0]
        for pg in range(1, npg + 1):
            acc = acc + w[:, pg:pg + 1] * o_s[pg]
        o_ref[0] = acc


def moba_decode(qn, kn, v, cache_k, cache_v, page_table, layer, bsz, seq):
    ppb = MOBA_BLOCK // PAGE_SIZE
    pps = DECODE_PAGES_PER_STEP
    n_pages = page_table.shape[1]
    nrow = H_B * seq
    assert nrow % SUBLANES == 0 and n_pages // ppb >= MOBA_TOPK and n_pages % ppb == 0
    assert n_pages % pps == 0 and n_pages < LANES
    rows = PAGE_SIZE * H_B
    depth, n_phys = cache_k.shape[:2]
    ck = cache_k.reshape(depth, n_phys, rows, HD_B)
    cv = cache_v.reshape(depth, n_phys, rows, HD_B)
    qc = jnp.transpose(qn.reshape(bsz, seq, H_B, HD_B), (0, 2, 1, 3)).reshape(bsz, nrow, HD_B)
    c_h, c_q = jnp.arange(nrow) // seq, jnp.arange(nrow) % seq
    key_h = jnp.arange(rows) % H_B
    bias = jnp.where(c_h[:, None] == key_h[None, :], 0.0, NEG_INF).astype(F32)
    own = jnp.arange(seq * H_B)
    obias = jnp.where((c_h[:, None] == own[None, :] % H_B) & (own[None, :] // H_B <= c_q[:, None]),
                      0.0, NEG_INF).astype(F32)
    kown = kn.reshape(bsz, seq * H_B, HD_B)
    vown = v.reshape(bsz, seq * H_B, HD_B)

    page = lambda j: pl.BlockSpec((1, 1, rows, HD_B), lambda b, n, pt, j=j: (layer, pt[b, n * pps + j], 0, 0))
    seq3 = lambda shape: pl.BlockSpec((1,) + shape, lambda b, n, pt: (b, 0, 0))
    const2 = lambda shape: pl.BlockSpec(shape, lambda b, n, pt: (0, 0))
    grid_spec = pltpu.PrefetchScalarGridSpec(
        num_scalar_prefetch=1, grid=(bsz, n_pages // pps),
        in_specs=[seq3((nrow, HD_B)), seq3((nrow, HD_B)), const2((nrow, rows)), const2((nrow, seq * H_B)),
                  seq3((seq * H_B, HD_B)), seq3((seq * H_B, HD_B))]
                 + [page(j) for j in range(pps)] + [page(j) for j in range(pps)],
        out_specs=seq3((nrow, HD_B)),
        scratch_shapes=[pltpu.VMEM((nrow, LANES), F32), pltpu.VMEM((nrow, LANES), F32),
                        pltpu.VMEM((n_pages + 1, nrow, HD_B), F32), pltpu.VMEM((n_pages * H_B, HD_B), F32)])
    out = pl.pallas_call(
        functools.partial(_moba_decode_kernel, pps=pps, seq=seq), grid_spec=grid_spec,
        out_shape=jax.ShapeDtypeStruct((bsz, nrow, HD_B), F32),
        compiler_params=_params("arbitrary", "arbitrary"), name="moba_decode")(
            page_table, qc.astype(BF16), qc, bias, obias, kown, vown, *([ck] * pps), *([cv] * pps))
    return jnp.transpose(out.reshape(bsz, H_B, seq, HD_B), (0, 2, 1, 3)).reshape(bsz * seq, W_BR)


def _mixer(x, h, p, layer, lw, states, st, bsz, seq, attend):
    (w_t, conv_a, a_log, dt_bias, norm_a, qnorm_b, knorm_b, b_if, norm_c, wa, wb, wc, wo) = lw
    conv_buf, s_a, c_c, n_c, m_c = states
    ps = small_proj(h, w_t, layer)
    oa, conv_new, s_new = delta_branch(p, ps, conv_a, conv_buf, a_log, dt_bias, s_a, st, norm_a, bsz, seq)
    ob, kn, vb = attend(p, qnorm_b, knorm_b)
    oc, c_new, n_new, m_new = mlstm_branch(p, ps, b_if, c_c, n_c, m_c, st, norm_c, bsz, seq)
    y = out_proj(x, merge(oa, ob, oc, p, wa, wb, wc), wo)
    kv_shape = (bsz, seq, H_B, HD_B)
    return y, (kn.reshape(kv_shape), vb.reshape(kv_shape), conv_new, s_new, c_new, n_new, m_new)


def _pad_states(conv, delta, c, n, m):
    conv = jnp.pad(conv, ((0, 0), (0, 0), (SUBLANES - (CONV_W - 1), 0), (0, 0)))
    m = jnp.pad(m, ((0, 0), (0, 0), (0, LANES - H_C)))[:, :, None, :]
    return conv, delta, c, n, m


def kernel(x_prompt, x_sample, cache_k, cache_v, page_table, state_conv_a, state_delta_a, state_mlstm_c,
           state_mlstm_n, state_mlstm_m, ln_g, w_in, conv_a, a_log, dt_bias, norm_a, qnorm_b, knorm_b,
           b_if, norm_c, w_br_a, w_br_b, w_br_c, w_out):
    bp, sp, d = x_prompt.shape
    bs, ss, _ = x_sample.shape
    depth = w_in.shape[0]
    yp = x_prompt.reshape(bp * sp, d)
    ys = x_sample.reshape(bs * ss, d)
    new_p = [[] for _ in range(7)]
    new_s = [[] for _ in range(7)]
    w_t = jnp.swapaxes(w_in, 1, 2)
    zeros = lambda *s: jnp.zeros(s, F32)
    states_p = _pad_states(zeros(1, bp, CONV_W - 1, 3 * W_BR), zeros(1, bp, H_A, DK_A, DK_A),
                           zeros(1, bp, H_C, DK_C, DK_C), zeros(1, bp, H_C, DK_C), zeros(1, bp, H_C))
    states_s = _pad_states(state_conv_a, state_delta_a, state_mlstm_c, state_mlstm_n, state_mlstm_m)
    for l in range(depth):
        lw = (w_t, conv_a[l], a_log[l], dt_bias[l], norm_a[l], qnorm_b[l],
              knorm_b[l], b_if[l], norm_c[l], w_br_a[l].astype(BF16), w_br_b[l].astype(BF16),
              w_br_c[l].astype(BF16), w_out[l].astype(BF16))

        def attend_prompt(p, qg, kg):
            qn, kn, knb, vt, kbar = qk_norm(p, qg, kg, with_kv=True)
            ob = moba_prompt(qn, knb, vt, kbar.reshape(bp, sp // MOBA_BLOCK, W_BR), bp, sp)
            return ob, kn, p[:, COL_VB * W_BR:(COL_VB + 1) * W_BR]

        def attend_sample(p, qg, kg, l=l):
            qn, kn = qk_norm(p, qg, kg, with_kv=False)
            vb = p[:, COL_VB * W_BR:(COL_VB + 1) * W_BR]
            return moba_decode(qn, kn, vb, cache_k, cache_v, page_table, l, bs, ss), kn, vb

        hp = rms_cast(yp, ln_g[l])
        hs = rms_cast(ys, ln_g[l])
        pp, psm = in_proj(hp, hs, w_t, l)
        yp, st_p = _mixer(yp, hp, pp, l, lw, states_p, 0, bp, sp, attend_prompt)
        ys, st_s = _mixer(ys, hs, psm, l, lw, states_s, l, bs, ss, attend_sample)
        for i in range(7):
            new_p[i].append(st_p[i])
            new_s[i].append(st_s[i])
    outs_p = [jnp.stack(t) for t in new_p]
    outs_s = [jnp.stack(t) for t in new_s]
    return (yp.reshape(bp, sp, d), ys.reshape(bs, ss, d), *outs_p, *outs_s)
```

```python
import functools

import jax
import jax.numpy as jnp
from jax import lax
from jax.experimental import pallas as pl
from jax.experimental.pallas import tpu as pltpu

F32 = jnp.float32
BF16 = jnp.bfloat16
HI = lax.Precision.HIGHEST
EPS = 1e-6
NEG_INF = float("-inf")

H_A, DK_A = 8, 128
H_B, HD_B = 8, 128
H_C, DK_C = 4, 256
CONV_W = 4
MOBA_BLOCK = 256
MOBA_TOPK = 3
PAGE_SIZE = 128
CHUNK = 64
W_BR = 1024
D_MODEL = 2048

LANES = 128
SUBLANES = 8
BF16_ROWS = 16
LOG2E = 1.4426950408889634
VMEM_LIMIT = 48 * 1024 * 1024

COL_QA, COL_KA, COL_VA, COL_ZA = 0, 1, 2, 3
COL_QB, COL_KB, COL_VB, COL_ZB = 4, 5, 6, 7
COL_QC, COL_KC, COL_VC, COL_ZC, COL_OC = 8, 9, 10, 11, 12
COL_GATE = 13
N_BIG = 19 * W_BR
OFF_SMALL_A = 4 * W_BR
OFF_SMALL_C = OFF_SMALL_A + 2 * H_A + 9 * W_BR
OFF_GATE = OFF_SMALL_C + 2 * H_C
WIN_A, WIN_C = OFF_SMALL_A // LANES, OFF_SMALL_C // LANES
LANE_BETA, LANE_ALPHA = 0, H_A
LANE_I = OFF_SMALL_C % LANES
LANE_F = LANE_I + H_C
SHIFT_A, SHIFT_BC, SHIFT_G = 0, 2 * H_A, 2 * H_A + 2 * H_C
FIRST_BC_BLOCK, FIRST_G_BLOCK = COL_QB, COL_GATE


def _params(*sem):
    return pltpu.CompilerParams(dimension_semantics=sem, vmem_limit_bytes=VMEM_LIMIT)


def _sigmoid(x):
    return 1.0 / (1.0 + jnp.exp(-x))


def _silu(x):
    return x * _sigmoid(x)


def _softplus(x):
    return jnp.maximum(x, 0.0) + jnp.log(1.0 + jnp.exp(-jnp.abs(x)))


def _dot(a, b, precision=None):
    return jnp.dot(a, b, precision=precision, preferred_element_type=F32)


def _dot_nt(a, b, precision=None):
    return lax.dot_general(a, b, (((1,), (1,)), ((), ())), precision=precision,
                           preferred_element_type=F32)


def _dot_tn(a, b, precision=None):
    return lax.dot_general(a, b, (((0,), (0,)), ((), ())), precision=precision,
                           preferred_element_type=F32)


def _cumsum_rows(x):
    rid = lax.broadcasted_iota(jnp.int32, x.shape, 0)
    d = 1
    while d < x.shape[0]:
        x = x + jnp.where(rid >= d, pltpu.roll(x, d, 0), 0.0)
        d *= 2
    return x


def _chunk_len(seq):
    return min(CHUNK, -(-seq // BF16_ROWS) * BF16_ROWS)


def _branch_dtype(block_rows):
    return BF16 if block_rows % BF16_ROWS == 0 else F32


def _pick_tile(n, candidates):
    for c in candidates:
        if n % c == 0:
            return c
    return n


def _rms_kernel(x_ref, g_ref, wa_ref, wc_ref, o_ref, ps_ref):
    x = x_ref[...]
    y = x * lax.rsqrt(jnp.mean(x * x, axis=-1, keepdims=True) + EPS)
    h = (y * g_ref[...]).astype(o_ref.dtype)
    o_ref[...] = h
    ps_ref[:, 0:LANES] = _dot_nt(h, wa_ref[0].astype(BF16))
    ps_ref[:, LANES:2 * LANES] = _dot_nt(h, wc_ref[0].astype(BF16))


def rms_cast(x, g, w_t, layer):
    t, d = x.shape
    tm = _pick_tile(t, (512, 256, 128))
    win = lambda c: pl.BlockSpec((1, LANES, d), lambda i, c=c: (layer, c, 0))
    return pl.pallas_call(
        _rms_kernel, grid=(t // tm,),
        in_specs=[pl.BlockSpec((tm, d), lambda i: (i, 0)), pl.BlockSpec((1, d), lambda i: (0, 0)),
                  win(WIN_A), win(WIN_C)],
        out_specs=[pl.BlockSpec((tm, d), lambda i: (i, 0)), pl.BlockSpec((tm, 2 * LANES), lambda i: (i, 0))],
        out_shape=[jax.ShapeDtypeStruct((t, d), BF16), jax.ShapeDtypeStruct((t, 2 * LANES), F32)],
        compiler_params=_params("arbitrary"), name="rms_cast")(x, g.reshape(1, d), w_t, w_t)


def _in_proj_kernel(a_ref, a2_ref, w_ref, o_ref, o2_ref, wb_s):
    @pl.when(pl.program_id(1) == 0)
    def _first():
        wb_s[...] = w_ref[...].astype(BF16)
        o2_ref[...] = _dot_nt(a2_ref[...], wb_s[...])

    o_ref[...] = _dot_nt(a_ref[...], wb_s[...])


def in_proj(h, h2, w_t, layer):
    t, k = h.shape
    t2 = h2.shape[0]
    tm = _pick_tile(t, (1024, 768, 512, 256, 128))
    tn = W_BR

    def w_rows(j, i):
        shift = jnp.where(j < FIRST_BC_BLOCK, SHIFT_A // SUBLANES,
                          jnp.where(j < FIRST_G_BLOCK, SHIFT_BC // SUBLANES, SHIFT_G // SUBLANES))
        return layer, (j * (tn // SUBLANES) + shift) * SUBLANES, 0

    n_i = t // tm
    row = lambda j, i: jnp.where(j % 2 == 0, i, n_i - 1 - i)
    return pl.pallas_call(
        _in_proj_kernel, grid=(N_BIG // tn, n_i),
        in_specs=[pl.BlockSpec((tm, k), lambda j, i: (row(j, i), 0)),
                  pl.BlockSpec((t2, k), lambda j, i: (0, 0)),
                  pl.BlockSpec((None, pl.Element(tn), pl.Element(k)), w_rows)],
        out_specs=[pl.BlockSpec((tm, tn), lambda j, i: (row(j, i), j)),
                   pl.BlockSpec((t2, tn), lambda j, i: (0, j))],
        out_shape=[jax.ShapeDtypeStruct((t, N_BIG), F32), jax.ShapeDtypeStruct((t2, N_BIG), F32)],
        scratch_shapes=[pltpu.VMEM((tn, k), BF16)],
        compiler_params=_params("arbitrary", "arbitrary"), name="in_proj")(h, h2, w_t)


def _merge_kernel(oa_ref, ob_ref, oc_ref, zb_ref, ga_ref, gb_ref, gc_ref, wa_ref, wb_ref, wc_ref, o_ref):
    ob = ob_ref[...] * _silu(zb_ref[...])
    m = _sigmoid(ga_ref[...]) * _dot(oa_ref[...].astype(BF16), wa_ref[...])
    m += _sigmoid(gb_ref[...]) * _dot(ob.astype(BF16), wb_ref[...])
    m += _sigmoid(gc_ref[...]) * _dot(oc_ref[...].astype(BF16), wc_ref[...])
    o_ref[...] = m.astype(o_ref.dtype)


def merge(oa, ob, oc, p, wa, wb, wc):
    t = oa.shape[0]
    tm = _pick_tile(t, (512, 256, 128))
    tn = W_BR
    row = lambda j, i: (i, 0)
    return pl.pallas_call(
        _merge_kernel, grid=(D_MODEL // tn, t // tm),
        in_specs=[pl.BlockSpec((tm, W_BR), row), pl.BlockSpec((tm, W_BR), row), pl.BlockSpec((tm, W_BR), row),
                  pl.BlockSpec((tm, W_BR), lambda j, i: (i, COL_ZB)),
                  pl.BlockSpec((tm, tn), lambda j, i: (i, COL_GATE + j)),
                  pl.BlockSpec((tm, tn), lambda j, i: (i, COL_GATE + 2 + j)),
                  pl.BlockSpec((tm, tn), lambda j, i: (i, COL_GATE + 4 + j)),
                  pl.BlockSpec((W_BR, tn), lambda j, i: (0, j)),
                  pl.BlockSpec((W_BR, tn), lambda j, i: (0, j)),
                  pl.BlockSpec((W_BR, tn), lambda j, i: (0, j))],
        out_specs=pl.BlockSpec((tm, tn), lambda j, i: (i, j)),
        out_shape=jax.ShapeDtypeStruct((t, D_MODEL), BF16),
        compiler_params=_params("arbitrary", "arbitrary"), name="merge")(oa, ob, oc, p, p, p, p, wa, wb, wc)


def _outproj_kernel(x_ref, m_ref, w_ref, o_ref):
    o_ref[...] = x_ref[...] + _dot(m_ref[...], w_ref[...])


def out_proj(x, m, w):
    t, d = x.shape
    tm = _pick_tile(t, (512, 256, 128))
    tn = 1024
    return pl.pallas_call(
        _outproj_kernel, grid=(d // tn, t // tm),
        in_specs=[pl.BlockSpec((tm, tn), lambda j, i: (i, j)), pl.BlockSpec((tm, d), lambda j, i: (i, 0)),
                  pl.BlockSpec((d, tn), lambda j, i: (0, j))],
        out_specs=pl.BlockSpec((tm, tn), lambda j, i: (i, j)),
        out_shape=jax.ShapeDtypeStruct((t, d), F32),
        compiler_params=_params("arbitrary", "arbitrary"), name="out_proj")(x, m, w)


def _bdot(a, b):
    return _dot(a.astype(BF16), b.astype(BF16))


DELTA_CHUNKS_PER_STEP = 4


def _delta_kernel(q_ref, k_ref, v_ref, z_ref, sm_ref, cw_ref, cb_ref, al_ref, dtb_ref, s0_ref, ng_ref,
                  o_ref, cn_ref, sn_ref, xp_ref, s_ref, *, c_in, chunk):
    c = pl.program_id(1)
    C = chunk
    hd = DK_A

    @pl.when(c == 0)
    def _init():
        xp_ref[...] = jnp.zeros(xp_ref.shape, F32)
        xp_ref[0:SUBLANES, :] = cb_ref[0]
        s_ref[...] = s0_ref[0]

    xp_ref[SUBLANES:SUBLANES + c_in, 0:W_BR] = q_ref[...]
    xp_ref[SUBLANES:SUBLANES + c_in, W_BR:2 * W_BR] = k_ref[...]
    xp_ref[SUBLANES:SUBLANES + c_in, 2 * W_BR:3 * W_BR] = v_ref[...]

    nch = max(c_in // C, 1)
    padded = c_in < C
    valid = lax.broadcasted_iota(jnp.int32, (C, 1), 0) < c_in
    mask = (lambda x, fill: jnp.where(valid, x, fill)) if padded else (lambda x, fill: x)
    ri = lax.broadcasted_iota(jnp.int32, (C, C), 0)
    ci = lax.broadcasted_iota(jnp.int32, (C, C), 1)
    causal = ci <= ri
    strict = ci < ri

    beta_c, gcum_c, gcum_t_c, e_g_c, e_end_c, e_last_c = [], [], [], [], [], []
    for kc in range(nch):
        if padded:
            sm = jnp.concatenate([sm_ref[...], jnp.zeros((C - c_in, LANES), F32)], axis=0)
        else:
            sm = sm_ref[kc * C:(kc + 1) * C, :]
        g = mask(-jnp.exp(al_ref[...]) * _softplus(sm + dtb_ref[...]), 0.0)
        gcum = _cumsum_rows(g)
        g_last = gcum[C - 1:C, :]
        beta_c.append(_sigmoid(sm))
        gcum_c.append(gcum)
        gcum_t_c.append(gcum.T)
        e_g_c.append(jnp.exp(gcum))
        e_end_c.append(jnp.exp(g_last - gcum))
        e_last_c.append(jnp.exp(g_last))

    def conv(kc, col):
        sl = slice(col, col + hd)
        r0 = 5 + kc * C
        y = xp_ref[r0:r0 + C, sl] * cw_ref[0:1, sl]
        for j in range(1, CONV_W):
            y = y + xp_ref[r0 + j:r0 + j + C, sl] * cw_ref[j:j + 1, sl]
        return mask(_silu(y), 0.0)

    heads = range(H_A)
    pairs = [(kc, h) for kc in range(nch) for h in heads]
    npair = range(len(pairs))
    col = lambda vec, kc, lane: vec[kc][:, lane:lane + 1]
    q_l = [conv(kc, h * hd) for kc, h in pairs]
    k_l = [conv(kc, W_BR + h * hd) for kc, h in pairs]
    v_l = [conv(kc, 2 * W_BR + h * hd) for kc, h in pairs]
    q_l = [q * lax.rsqrt(jnp.sum(q * q, axis=-1, keepdims=True) + EPS) * (DK_A ** -0.5) for q in q_l]
    k_l = [k * lax.rsqrt(jnp.sum(k * k, axis=-1, keepdims=True) + EPS) for k in k_l]
    decay_l = [jnp.exp(jnp.where(causal, col(gcum_c, kc, LANE_ALPHA + h)
                                 - gcum_t_c[kc][LANE_ALPHA + h:LANE_ALPHA + h + 1, :], NEG_INF))
               for kc, h in pairs]
    b_l = [col(beta_c, kc, LANE_BETA + h) for kc, h in pairs]
    eg_l = [col(e_g_c, kc, LANE_ALPHA + h) for kc, h in pairs]
    kb_l = [k.astype(BF16) for k in k_l]
    qk_kk = [_dot_nt(jnp.concatenate([q_l[i].astype(BF16), kb_l[i]], axis=0), kb_l[i]) for i in npair]
    a_l = [jnp.where(strict, b_l[i] * decay_l[i] * qk_kk[i][C:2 * C], 0.0) for i in npair]
    r_l = [-a for a in a_l]
    pw_l = a_l
    n = 2
    while n < C:
        pw_l = [_bdot(pw, pw) for pw in pw_l]
        r_l = [r_l[i] + pw_l[i] + _bdot(r_l[i], pw_l[i]) for i in npair]
        n *= 2
    rhs_l = [jnp.concatenate([b_l[i] * v_l[i], (b_l[i] * eg_l[i]) * k_l[i]], axis=1) for i in npair]
    sol_l = [rhs_l[i] + _bdot(r_l[i], rhs_l[i]) for i in npair]
    kq_l = [jnp.concatenate([sol_l[i][:, hd:2 * hd], q_l[i] * eg_l[i]], axis=0).astype(BF16) for i in npair]
    p_l = [(decay_l[i] * qk_kk[i][0:C]).astype(BF16) for i in npair]
    kend_l = [(k_l[i] * col(e_end_c, kc, LANE_ALPHA + h)).astype(BF16) for i, (kc, h) in enumerate(pairs)]

    s_l = [s_ref[h] for h in heads]
    o_l = []
    for kc in range(nch):
        idx = [kc * H_A + h for h in heads]
        kq_s = [_dot(kq_l[i], s_l[h].astype(BF16)) for h, i in enumerate(idx)]
        wb_l = [(sol_l[i][:, 0:hd] - kq_s[h][0:C]).astype(BF16) for h, i in enumerate(idx)]
        o_l += [kq_s[h][C:2 * C] + _dot(p_l[i], wb_l[h]) for h, i in enumerate(idx)]
        s_l = [col(e_last_c, kc, LANE_ALPHA + h) * s_l[h] + _dot_tn(kend_l[i], wb_l[h])
               for h, i in enumerate(idx)]
    for h in heads:
        s_ref[h] = s_l[h]
    for i, (kc, h) in enumerate(pairs):
        o = o_l[i]
        o = o * lax.rsqrt(jnp.mean(o * o, axis=-1, keepdims=True) + EPS) * ng_ref[...]
        rows = slice(0, c_in) if padded else slice(kc * C, (kc + 1) * C)
        o = o[0:min(c_in, C)] * _silu(z_ref[rows, h * hd:(h + 1) * hd])
        o_ref[rows, h * hd:(h + 1) * hd] = o.astype(o_ref.dtype)

    tail = xp_ref[c_in:c_in + SUBLANES, :]
    xp_ref[0:SUBLANES, :] = tail

    @pl.when(c == pl.num_programs(1) - 1)
    def _fin():
        cn_ref[0] = tail
        sn_ref[0] = s_ref[...]


def delta_branch(p, ps, conv_w, conv_buf, a_log, dt_bias, s0, st, norm_g, bsz, seq):
    chunk = _chunk_len(seq)
    c_in = min(seq, DELTA_CHUNKS_PER_STEP * chunk)
    assert seq % c_in == 0 and (c_in % chunk == 0 or c_in < chunk)
    n_chunks = seq // c_in
    lane_vec = lambda v: jnp.zeros((1, LANES), F32).at[0, LANE_ALPHA:LANE_ALPHA + H_A].set(v)
    rb = lambda col: pl.BlockSpec((c_in, W_BR), lambda b, c, col=col: (b * n_chunks + c, col))
    const2 = lambda b, c: (0, 0)
    o, cn, sn = pl.pallas_call(
        functools.partial(_delta_kernel, c_in=c_in, chunk=chunk), grid=(bsz, n_chunks),
        in_specs=[rb(COL_QA), rb(COL_KA), rb(COL_VA), rb(COL_ZA),
                  pl.BlockSpec((c_in, LANES), lambda b, c: (b * n_chunks + c, 0)),
                  pl.BlockSpec((CONV_W, 3 * W_BR), const2),
                  pl.BlockSpec((None, 1, SUBLANES, 3 * W_BR), lambda b, c: (st, b, 0, 0)),
                  pl.BlockSpec((1, LANES), const2), pl.BlockSpec((1, LANES), const2),
                  pl.BlockSpec((None, 1, H_A, DK_A, DK_A), lambda b, c: (st, b, 0, 0, 0)),
                  pl.BlockSpec((1, DK_A), const2)],
        out_specs=[pl.BlockSpec((c_in, W_BR), lambda b, c: (b * n_chunks + c, 0)),
                   pl.BlockSpec((1, SUBLANES, 3 * W_BR), lambda b, c: (b, 0, 0)),
                   pl.BlockSpec((1, H_A, DK_A, DK_A), lambda b, c: (b, 0, 0, 0))],
        out_shape=[jax.ShapeDtypeStruct((bsz * seq, W_BR), _branch_dtype(c_in)),
                   jax.ShapeDtypeStruct((bsz, SUBLANES, 3 * W_BR), F32),
                   jax.ShapeDtypeStruct((bsz, H_A, DK_A, DK_A), F32)],
        scratch_shapes=[pltpu.VMEM((SUBLANES + max(c_in, chunk), 3 * W_BR), F32),
                        pltpu.VMEM((H_A, DK_A, DK_A), F32)],
        compiler_params=_params("arbitrary", "arbitrary"), name="delta")(
            p, p, p, p, ps, conv_w, conv_buf, lane_vec(a_log), lane_vec(dt_bias), s0, norm_g.reshape(1, DK_A))
    return o, cn[:, SUBLANES - (CONV_W - 1):], sn


MLSTM_CHUNKS_PER_STEP = 4


def _mlstm_kernel(q_ref, k_ref, v_ref, z_ref, og_ref, sm_ref, bif_ref, c0_ref, n0_ref, m0_ref, ng_ref,
                  o_ref, cn_ref, nn_ref, mn_ref, c_s, n_s, m_s, *, c_in, chunk):
    c = pl.program_id(1)
    C = chunk
    hd = DK_C

    @pl.when(c == 0)
    def _init():
        c_s[...] = c0_ref[0]
        n_s[...] = n0_ref[0]
        m_s[...] = m0_ref[0]

    nch = max(c_in // C, 1)
    padded = c_in < C

    def rows_of(ref, kc, sl):
        if padded:
            return jnp.concatenate([ref[:, sl], jnp.zeros((C - c_in, sl.stop - sl.start), F32)], axis=0)
        return ref[kc * C:(kc + 1) * C, sl]

    valid = lax.broadcasted_iota(jnp.int32, (C, 1), 0) < c_in
    ri = lax.broadcasted_iota(jnp.int32, (C, C), 0)
    ci = lax.broadcasted_iota(jnp.int32, (C, C), 1)
    causal = ci <= ri
    m_old = m_s[...]
    lane = lax.broadcasted_iota(jnp.int32, m_old.shape, 1)
    heads = range(H_C)
    hsl = [slice(h * hd, (h + 1) * hd) for h in heads]
    pairs = [(kc, h) for kc in range(nch) for h in heads]

    i_c, i_t_c, b_c, b_t_c = [], [], [], []
    for kc in range(nch):
        pre = rows_of(sm_ref, kc, slice(0, LANES)) + bif_ref[...]
        i_pre = jnp.where(valid, pre, NEG_INF) if padded else pre
        log_f = -_softplus(-pre)
        if padded:
            log_f = jnp.where(valid, log_f, 0.0)
        bcum = _dot(causal.astype(F32), log_f, HI)
        i_c.append(i_pre)
        i_t_c.append(i_pre.T)
        b_c.append(bcum)
        b_t_c.append(bcum.T)

    q_l = [rows_of(q_ref, kc, hsl[h]) for kc, h in pairs]
    ks_l = [rows_of(k_ref, kc, hsl[h]) * (DK_C ** -0.5) for kc, h in pairs]
    vb_l = [rows_of(v_ref, kc, hsl[h]).astype(BF16) for kc, h in pairs]
    qb_l = [q.astype(BF16) for q in q_l]
    qk_l = [_dot_nt(qb_l[i], ks_l[i].astype(BF16)) for i in range(len(pairs))]
    bcol_l = [b_c[kc][:, LANE_F + h:LANE_F + h + 1] for kc, h in pairs]
    icol_l = [i_c[kc][:, LANE_I + h:LANE_I + h + 1] for kc, h in pairs]
    intra_l = [jnp.where(causal, bcol_l[i] - b_t_c[kc][LANE_F + h:LANE_F + h + 1, :]
                         + i_t_c[kc][LANE_I + h:LANE_I + h + 1, :], NEG_INF) for i, (kc, h) in enumerate(pairs)]
    imax_l = [jnp.max(x, axis=-1, keepdims=True) for x in intra_l]
    m_prev = [m_old[:, h:h + 1] for h in heads]
    mt_l, winter_l, wc_l, wj_l = [], [], [], []
    for i, (kc, h) in enumerate(pairs):
        inter = bcol_l[i] + m_prev[h]
        m_t = jnp.maximum(inter, imax_l[i])
        m_new = m_t[C - 1:C, :]
        b_last = bcol_l[i][C - 1:C, :]
        mt_l.append(m_t)
        winter_l.append(jnp.exp(inter - m_t))
        wc_l.append(jnp.exp(b_last + m_prev[h] - m_new))
        wj_l.append(jnp.exp(b_last - bcol_l[i] + icol_l[i] - m_new))
        m_prev[h] = m_new
    m_new_vec = m_old
    for h in heads:
        m_new_vec = jnp.where(lane == h, m_prev[h], m_new_vec)
    m_s[...] = m_new_vec

    s_l = [qk_l[i] * jnp.exp(intra_l[i] - mt_l[i]) for i in range(len(pairs))]
    sv_l = [_dot(s_l[i].astype(BF16), vb_l[i]) for i in range(len(pairs))]
    ssum_l = [jnp.sum(s, axis=-1, keepdims=True) for s in s_l]
    kw_l = [wj_l[i] * ks_l[i] for i in range(len(pairs))]
    upd_l = [_dot_tn(kw_l[i].astype(BF16), vb_l[i]) for i in range(len(pairs))]
    nsum_l = [jnp.sum(kw, axis=0, keepdims=True) for kw in kw_l]

    cm_l = [c_s[h] for h in heads]
    n_l = [n_s[h:h + 1, :] for h in heads]
    for i, (kc, h) in enumerate(pairs):
        num = winter_l[i] * _dot(qb_l[i], cm_l[h].astype(BF16)) + sv_l[i]
        den = winter_l[i] * jnp.sum(q_l[i] * n_l[h], axis=-1, keepdims=True) + ssum_l[i]
        hh = num / jnp.maximum(jnp.abs(den), jnp.exp(-mt_l[i]))
        cm_l[h] = wc_l[i] * cm_l[h] + upd_l[i]
        n_l[h] = wc_l[i] * n_l[h] + nsum_l[i]
        rows = slice(0, c_in) if padded else slice(kc * C, (kc + 1) * C)
        hc = _sigmoid(og_ref[rows, hsl[h]]) * hh[0:min(c_in, C)]
        hc = hc * lax.rsqrt(jnp.mean(hc * hc, axis=-1, keepdims=True) + EPS) * ng_ref[...]
        o_ref[rows, hsl[h]] = (hc * _silu(z_ref[rows, hsl[h]])).astype(o_ref.dtype)
    for h in heads:
        c_s[h] = cm_l[h]
        n_s[h:h + 1, :] = n_l[h]

    @pl.when(c == pl.num_programs(1) - 1)
    def _fin():
        cn_ref[0] = c_s[...]
        nn_ref[0] = n_s[...]
        mn_ref[0] = m_new_vec


def mlstm_branch(p, ps, b_if, c0, n0, m0p, st, norm_g, bsz, seq):
    chunk = _chunk_len(seq)
    c_in = min(seq, MLSTM_CHUNKS_PER_STEP * chunk)
    assert seq % c_in == 0 and (c_in % chunk == 0 or c_in < chunk)
    n_chunks = seq // c_in
    bif = jnp.zeros((1, LANES), F32).at[0, LANE_I:LANE_I + 2 * H_C].set(b_if)
    rb = lambda col: pl.BlockSpec((c_in, W_BR), lambda b, c, col=col: (b * n_chunks + c, col))
    const2 = lambda b, c: (0, 0)
    st4 = pl.BlockSpec((1, H_C, DK_C, DK_C), lambda b, c: (b, 0, 0, 0))
    st3 = pl.BlockSpec((1, H_C, DK_C), lambda b, c: (b, 0, 0))
    stm = pl.BlockSpec((1, 1, LANES), lambda b, c: (b, 0, 0))
    in4 = pl.BlockSpec((None, 1, H_C, DK_C, DK_C), lambda b, c: (st, b, 0, 0, 0))
    in3 = pl.BlockSpec((None, 1, H_C, DK_C), lambda b, c: (st, b, 0, 0))
    inm = pl.BlockSpec((None, 1, 1, LANES), lambda b, c: (st, b, 0, 0))
    o, cn, nn, mn = pl.pallas_call(
        functools.partial(_mlstm_kernel, c_in=c_in, chunk=chunk), grid=(bsz, n_chunks),
        in_specs=[rb(COL_QC), rb(COL_KC), rb(COL_VC), rb(COL_ZC), rb(COL_OC),
                  pl.BlockSpec((c_in, LANES), lambda b, c: (b * n_chunks + c, 1)),
                  pl.BlockSpec((1, LANES), const2), in4, in3, inm, pl.BlockSpec((1, DK_C), const2)],
        out_specs=[pl.BlockSpec((c_in, W_BR), lambda b, c: (b * n_chunks + c, 0)), st4, st3, stm],
        out_shape=[jax.ShapeDtypeStruct((bsz * seq, W_BR), _branch_dtype(c_in)),
                   jax.ShapeDtypeStruct((bsz, H_C, DK_C, DK_C), F32),
                   jax.ShapeDtypeStruct((bsz, H_C, DK_C), F32),
                   jax.ShapeDtypeStruct((bsz, 1, LANES), F32)],
        scratch_shapes=[pltpu.VMEM((H_C, DK_C, DK_C), F32), pltpu.VMEM((H_C, DK_C), F32),
                        pltpu.VMEM((1, LANES), F32)],
        compiler_params=_params("arbitrary", "arbitrary"), name="mlstm")(
            p, p, p, p, p, ps, bif, c0, n0, m0p, norm_g.reshape(1, DK_C))
    return o, cn, nn, mn[:, 0, :H_C]


def _head_rms(x, g):
    return x * lax.rsqrt(jnp.mean(x * x, axis=-1, keepdims=True) + EPS) * g


def _qknorm_kernel(q_ref, k_ref, qg_ref, kg_ref, qn_ref, kn_ref):
    for h in range(H_B):
        sl = slice(h * HD_B, (h + 1) * HD_B)
        qn_ref[:, sl] = _head_rms(q_ref[:, sl], qg_ref[...]) * (HD_B ** -0.5)
        kn_ref[:, sl] = _head_rms(k_ref[:, sl], kg_ref[...])


def _qknorm_kv_kernel(q_ref, k_ref, v_ref, qg_ref, kg_ref, qn_ref, kn_ref, knb_ref, vt_ref, kbar_ref):
    for h in range(H_B):
        sl = slice(h * HD_B, (h + 1) * HD_B)
        qn_ref[:, sl] = _head_rms(q_ref[:, sl], qg_ref[...]) * (HD_B ** -0.5)
        kn = _head_rms(k_ref[:, sl], kg_ref[...])
        kn_ref[:, sl] = kn
        knb_ref[0, :, sl] = kn.astype(BF16)
        kbar_ref[0, :, sl] = jnp.mean(kn, axis=0, keepdims=True)
    eye = (lax.broadcasted_iota(jnp.int32, (HD_B, HD_B), 0)
           == lax.broadcasted_iota(jnp.int32, (HD_B, HD_B), 1)).astype(BF16)
    for h in range(H_B):
        sl = slice(h * HD_B, (h + 1) * HD_B)
        vt_ref[0, sl, :] = _dot_nt(eye, v_ref[:, sl].astype(BF16)).astype(BF16)


def qk_norm(p, qg, kg, with_kv):
    t = p.shape[0]
    qg = qg.reshape(1, HD_B)
    kg = kg.reshape(1, HD_B)
    gs = pl.BlockSpec((1, HD_B), lambda i: (0, 0))
    if not with_kv:
        tm = _pick_tile(t, (256, 128))
        return pl.pallas_call(
            _qknorm_kernel, grid=(t // tm,),
            in_specs=[pl.BlockSpec((tm, W_BR), lambda i: (i, COL_QB)),
                      pl.BlockSpec((tm, W_BR), lambda i: (i, COL_KB)), gs, gs],
            out_specs=[pl.BlockSpec((tm, W_BR), lambda i: (i, 0))] * 2,
            out_shape=[jax.ShapeDtypeStruct((t, W_BR), F32)] * 2,
            compiler_params=_params("arbitrary"), name="qk_norm")(p, p, qg, kg)
    tm = MOBA_BLOCK
    nb = t // tm
    return pl.pallas_call(
        _qknorm_kv_kernel, grid=(nb,),
        in_specs=[pl.BlockSpec((tm, W_BR), lambda i: (i, COL_QB)),
                  pl.BlockSpec((tm, W_BR), lambda i: (i, COL_KB)),
                  pl.BlockSpec((tm, W_BR), lambda i: (i, COL_VB)), gs, gs],
        out_specs=[pl.BlockSpec((tm, W_BR), lambda i: (i, 0)), pl.BlockSpec((tm, W_BR), lambda i: (i, 0)),
                   pl.BlockSpec((1, tm, W_BR), lambda i: (i, 0, 0)),
                   pl.BlockSpec((1, W_BR, tm), lambda i: (i, 0, 0)),
                   pl.BlockSpec((1, 1, W_BR), lambda i: (i, 0, 0))],
        out_shape=[jax.ShapeDtypeStruct((t, W_BR), F32), jax.ShapeDtypeStruct((t, W_BR), F32),
                   jax.ShapeDtypeStruct((nb, tm, W_BR), BF16), jax.ShapeDtypeStruct((nb, W_BR, tm), BF16),
                   jax.ShapeDtypeStruct((nb, 1, W_BR), F32)],
        compiler_params=_params("arbitrary"), name="qk_norm_kv")(p, p, p, qg, kg)


def _select_topk(sc, valid, axis):
    nb = sc.shape[axis]
    idx = lax.broadcasted_iota(jnp.int32, sc.shape, axis).astype(F32)
    if valid is not None:
        sc = jnp.where(valid, sc, NEG_INF)
    sel = jnp.zeros(sc.shape, F32)
    for _ in range(MOBA_TOPK):
        mx = jnp.max(sc, axis=axis, keepdims=True)
        first = jnp.min(jnp.where(sc == mx, idx, float(nb)), axis=axis, keepdims=True)
        hit = idx == first
        sel = jnp.where(hit & (mx > NEG_INF), 1.0, sel)
        sc = jnp.where(hit, NEG_INF, sc)
    return sel


MOBA_HEADS_PER_STEP = 8


def _moba_prompt_kernel(q_ref, k_ref, vt_ref, kbar_ref, o_ref, sel_ref, qb_ref, acc_ref):
    i = pl.program_id(2)
    blk = MOBA_BLOCK
    hp = MOBA_HEADS_PER_STEP
    nb = kbar_ref.shape[1]
    past = lax.broadcasted_iota(jnp.int32, (nb, blk), 0) < i
    kpos = lax.broadcasted_iota(jnp.int32, (blk, blk), 0)
    qpos = lax.broadcasted_iota(jnp.int32, (blk, blk), 1)
    heads = [slice(hh * HD_B, (hh + 1) * HD_B) for hh in range(hp)]

    ones = jnp.ones((BF16_ROWS, blk), BF16)

    def pv(n, sl, pr):
        return _dot(jnp.concatenate([vt_ref[n, sl, :], ones], axis=0), pr)

    for hh, sl in enumerate(heads):
        qb_ref[hh] = (q_ref[:, sl] * LOG2E).astype(BF16)
    s_l = [_dot_nt(k_ref[i, :, sl], qb_ref[hh]) for hh, sl in enumerate(heads)]
    sc_l = [_dot_nt(kbar_ref[0, :, sl], q_ref[:, sl], HI) for sl in heads]
    ms, pr_l = [], []
    for hh in range(hp):
        s = jnp.where(kpos <= qpos, s_l[hh], NEG_INF)
        m = jnp.max(s, axis=0, keepdims=True)
        ms.append(m)
        pr_l.append(jnp.exp2(s - m).astype(BF16))
    for hh, sl in enumerate(heads):
        acc_ref[hh] = pv(i, sl, pr_l[hh])
        sel_ref[hh] = _select_topk(sc_l[hh], past, 0)

    def past_blocks(ns, ms):
        s_l = [[_dot_nt(k_ref[n, :, sl], qb_ref[hh]) for n in ns] for hh, sl in enumerate(heads)]
        ones_n = jnp.ones((BF16_ROWS, len(ns) * blk), BF16)
        ms_new, pr_l, alpha_l = [], [], []
        for hh in range(hp):
            ss = [jnp.where(sel_ref[hh, pl.ds(n, 1), :] > 0.0, s, NEG_INF) for n, s in zip(ns, s_l[hh])]
            m_new = ms[hh]
            for s in ss:
                m_new = jnp.maximum(m_new, jnp.max(s, axis=0, keepdims=True))
            alpha_l.append(jnp.exp2(ms[hh] - m_new))
            ms_new.append(m_new)
            pr_l.append(jnp.concatenate([jnp.exp2(s - m_new).astype(BF16) for s in ss], axis=0))
        for hh, sl in enumerate(heads):
            lhs = jnp.concatenate([jnp.concatenate([vt_ref[n, sl, :] for n in ns], axis=1), ones_n], axis=0)
            acc_ref[hh] = alpha_l[hh] * acc_ref[hh] + _dot(lhs, pr_l[hh])
        return tuple(ms_new)

    odd = i % 2
    ms = lax.cond(odd == 1, lambda ms: past_blocks([0], ms), lambda ms: ms, tuple(ms))
    lax.fori_loop(0, i // 2, lambda t, ms: past_blocks([2 * t + odd, 2 * t + odd + 1], ms), ms)
    for hh, sl in enumerate(heads):
        o_ref[:, sl] = (acc_ref[hh, 0:HD_B, :] / acc_ref[hh, HD_B:HD_B + 1, :]).T.astype(o_ref.dtype)


def moba_prompt(qn, knb, vt, kbar, bsz, seq):
    nb = seq // MOBA_BLOCK
    blk = MOBA_BLOCK
    hp = MOBA_HEADS_PER_STEP
    wid = hp * HD_B
    return pl.pallas_call(
        _moba_prompt_kernel, grid=(bsz, H_B // hp, nb),
        in_specs=[pl.BlockSpec((blk, wid), lambda b, h, i: (b * nb + i, h)),
                  pl.BlockSpec((nb, blk, wid), lambda b, h, i: (b, 0, h)),
                  pl.BlockSpec((nb, wid, blk), lambda b, h, i: (b, h, 0)),
                  pl.BlockSpec((1, nb, wid), lambda b, h, i: (b, 0, h))],
        out_specs=pl.BlockSpec((blk, wid), lambda b, h, i: (b * nb + i, h)),
        out_shape=jax.ShapeDtypeStruct((bsz * seq, W_BR), _branch_dtype(blk)),
        scratch_shapes=[pltpu.VMEM((hp, nb, blk), F32), pltpu.VMEM((hp, blk, HD_B), BF16),
                        pltpu.VMEM((hp, HD_B + BF16_ROWS, blk), F32)],
        compiler_params=_params("arbitrary", "arbitrary", "arbitrary"), name="moba_prompt")(qn, knb, vt, kbar)


DECODE_PAGES_PER_STEP = 16


def _moba_decode_kernel(pt_ref, q_ref, qf_ref, bias_ref, obias_ref, kown_ref, vown_ref, *rest, pps, seq):
    del pt_ref
    k_refs = rest[:pps]
    v_refs = rest[pps:2 * pps]
    o_ref = rest[2 * pps]
    m_s, l_s, o_s, ks_s = rest[2 * pps + 1:]
    n = pl.program_id(1)
    q = q_ref[0]
    ppb = MOBA_BLOCK // PAGE_SIZE
    npg = ks_s.shape[0] // H_B
    lane = lax.broadcasted_iota(jnp.int32, m_s.shape, 1)

    def partial_softmax(s):
        m = jnp.max(s, axis=-1, keepdims=True)
        pr = jnp.exp(s - m)
        return m, jnp.sum(pr, axis=-1, keepdims=True), pr

    @pl.when(n == 0)
    def _init():
        m_s[...] = jnp.zeros(m_s.shape, F32)
        l_s[...] = jnp.zeros(l_s.shape, F32)

    m_all = m_s[...]
    l_all = l_s[...]
    s_l = [_dot_nt(q, k_refs[j][0, 0].astype(BF16)) for j in range(pps)]
    pr_l = []
    for j in range(pps):
        m, l, pr = partial_softmax(s_l[j] + bias_ref[...])
        pg = n * pps + j
        m_all = jnp.where(lane == pg, m, m_all)
        l_all = jnp.where(lane == pg, l, l_all)
        pr_l.append(pr.astype(BF16))
    for j in range(pps):
        pg = n * pps + j
        o_s[pg] = _dot(pr_l[j], v_refs[j][0, 0].astype(BF16))
        ks_s[pl.ds(pg * H_B, H_B), :] = jnp.sum(k_refs[j][0, 0].reshape(PAGE_SIZE, H_B, HD_B), axis=0)
    m_s[...] = m_all
    l_s[...] = l_all

    @pl.when(n == pl.num_programs(1) - 1)
    def _combine():
        nb = npg // ppb
        m_o, l_o, pr_o = partial_softmax(_dot_nt(q, kown_ref[0].astype(BF16)) + obias_ref[...])
        o_s[npg] = _dot(pr_o.astype(BF16), vown_ref[0].astype(BF16))
        m_sl = jnp.where(lane == npg, m_o, m_all)
        l_sl = jnp.where(lane == npg, l_o, l_all)
        scp = _dot_nt(qf_ref[0], ks_s[...], HI)
        ch = lax.broadcasted_iota(jnp.int32, scp.shape, 1) % H_B
        rh = lax.broadcasted_iota(jnp.int32, scp.shape, 0) // seq
        scp = jnp.where(ch == rh, scp, 0.0)
        cols_per_blk = ppb * H_B
        gather = (lax.broadcasted_iota(jnp.int32, (npg * H_B, nb), 0) // cols_per_blk
                  == lax.broadcasted_iota(jnp.int32, (npg * H_B, nb), 1)).astype(F32)
        scb = _dot(scp, gather, HI) * (1.0 / MOBA_BLOCK)
        sel = _select_topk(scb, None, 1)
        slot = lax.broadcasted_iota(jnp.int32, (nb, m_s.shape[1]), 1)
        expand = (slot // ppb == lax.broadcasted_iota(jnp.int32, (nb, m_s.shape[1]), 0)) & (slot < npg)
        sel_slot = jnp.where(lane == npg, 1.0, _dot(sel.astype(BF16), expand.astype(BF16)))
        m_sel = jnp.where(sel_slot > 0.0, m_sl, NEG_INF)
        w = sel_slot * jnp.exp(m_sel - jnp.max(m_sel, axis=-1, keepdims=True))
        w = w / jnp.sum(w * l_sl, axis=-1, keepdims=True)
        acc = w[:, 0:1] * o_s[0]
        for pg in range(1, npg + 1):
            acc = acc + w[:, pg:pg + 1] * o_s[pg]
        o_ref[0] = acc


def moba_decode(qn, kn, v, cache_k, cache_v, page_table, layer, bsz, seq):
    ppb = MOBA_BLOCK // PAGE_SIZE
    pps = DECODE_PAGES_PER_STEP
    n_pages = page_table.shape[1]
    nrow = H_B * seq
    assert nrow % SUBLANES == 0 and n_pages // ppb >= MOBA_TOPK and n_pages % ppb == 0
    assert n_pages % pps == 0 and n_pages < LANES
    rows = PAGE_SIZE * H_B
    depth, n_phys = cache_k.shape[:2]
    ck = cache_k.reshape(depth, n_phys, rows, HD_B)
    cv = cache_v.reshape(depth, n_phys, rows, HD_B)
    qc = jnp.transpose(qn.reshape(bsz, seq, H_B, HD_B), (0, 2, 1, 3)).reshape(bsz, nrow, HD_B)
    c_h, c_q = jnp.arange(nrow) // seq, jnp.arange(nrow) % seq
    key_h = jnp.arange(rows) % H_B
    bias = jnp.where(c_h[:, None] == key_h[None, :], 0.0, NEG_INF).astype(F32)
    own = jnp.arange(seq * H_B)
    obias = jnp.where((c_h[:, None] == own[None, :] % H_B) & (own[None, :] // H_B <= c_q[:, None]),
                      0.0, NEG_INF).astype(F32)
    kown = kn.reshape(bsz, seq * H_B, HD_B)
    vown = v.reshape(bsz, seq * H_B, HD_B)

    page = lambda j: pl.BlockSpec((1, 1, rows, HD_B), lambda b, n, pt, j=j: (layer, pt[b, n * pps + j], 0, 0))
    seq3 = lambda shape: pl.BlockSpec((1,) + shape, lambda b, n, pt: (b, 0, 0))
    const2 = lambda shape: pl.BlockSpec(shape, lambda b, n, pt: (0, 0))
    grid_spec = pltpu.PrefetchScalarGridSpec(
        num_scalar_prefetch=1, grid=(bsz, n_pages // pps),
        in_specs=[seq3((nrow, HD_B)), seq3((nrow, HD_B)), const2((nrow, rows)), const2((nrow, seq * H_B)),
                  seq3((seq * H_B, HD_B)), seq3((seq * H_B, HD_B))]
                 + [page(j) for j in range(pps)] + [page(j) for j in range(pps)],
        out_specs=seq3((nrow, HD_B)),
        scratch_shapes=[pltpu.VMEM((nrow, LANES), F32), pltpu.VMEM((nrow, LANES), F32),
                        pltpu.VMEM((n_pages + 1, nrow, HD_B), F32), pltpu.VMEM((n_pages * H_B, HD_B), F32)])
    out = pl.pallas_call(
        functools.partial(_moba_decode_kernel, pps=pps, seq=seq), grid_spec=grid_spec,
        out_shape=jax.ShapeDtypeStruct((bsz, nrow, HD_B), F32),
        compiler_params=_params("arbitrary", "arbitrary"), name="moba_decode")(
            page_table, qc.astype(BF16), qc, bias, obias, kown, vown, *([ck] * pps), *([cv] * pps))
    return jnp.transpose(out.reshape(bsz, H_B, seq, HD_B), (0, 2, 1, 3)).reshape(bsz * seq, W_BR)


def _mixer(x, p, ps, lw, states, st, bsz, seq, attend):
    (conv_a, a_log, dt_bias, norm_a, qnorm_b, knorm_b, b_if, norm_c, wa, wb, wc, wo) = lw
    conv_buf, s_a, c_c, n_c, m_c = states
    oa, conv_new, s_new = delta_branch(p, ps, conv_a, conv_buf, a_log, dt_bias, s_a, st, norm_a, bsz, seq)
    ob, kn, vb = attend(p, qnorm_b, knorm_b)
    oc, c_new, n_new, m_new = mlstm_branch(p, ps, b_if, c_c, n_c, m_c, st, norm_c, bsz, seq)
    y = out_proj(x, merge(oa, ob, oc, p, wa, wb, wc), wo)
    kv_shape = (bsz, seq, H_B, HD_B)
    return y, (kn.reshape(kv_shape), vb.reshape(kv_shape), conv_new, s_new, c_new, n_new, m_new)


def _pad_states(conv, delta, c, n, m):
    conv = jnp.pad(conv, ((0, 0), (0, 0), (SUBLANES - (CONV_W - 1), 0), (0, 0)))
    m = jnp.pad(m, ((0, 0), (0, 0), (0, LANES - H_C)))[:, :, None, :]
    return conv, delta, c, n, m


def kernel(x_prompt, x_sample, cache_k, cache_v, page_table, state_conv_a, state_delta_a, state_mlstm_c,
           state_mlstm_n, state_mlstm_m, ln_g, w_in, conv_a, a_log, dt_bias, norm_a, qnorm_b, knorm_b,
           b_if, norm_c, w_br_a, w_br_b, w_br_c, w_out):
    bp, sp, d = x_prompt.shape
    bs, ss, _ = x_sample.shape
    depth = w_in.shape[0]
    yp = x_prompt.reshape(bp * sp, d)
    ys = x_sample.reshape(bs * ss, d)
    new_p = [[] for _ in range(7)]
    new_s = [[] for _ in range(7)]
    w_t = jnp.swapaxes(w_in, 1, 2)
    zeros = lambda *s: jnp.zeros(s, F32)
    states_p = _pad_states(zeros(1, bp, CONV_W - 1, 3 * W_BR), zeros(1, bp, H_A, DK_A, DK_A),
                           zeros(1, bp, H_C, DK_C, DK_C), zeros(1, bp, H_C, DK_C), zeros(1, bp, H_C))
    states_s = _pad_states(state_conv_a, state_delta_a, state_mlstm_c, state_mlstm_n, state_mlstm_m)
    for l in range(depth):
        lw = (conv_a[l], a_log[l], dt_bias[l], norm_a[l], qnorm_b[l],
              knorm_b[l], b_if[l], norm_c[l], w_br_a[l].astype(BF16), w_br_b[l].astype(BF16),
              w_br_c[l].astype(BF16), w_out[l].astype(BF16))

        def attend_prompt(p, qg, kg):
            qn, kn, knb, vt, kbar = qk_norm(p, qg, kg, with_kv=True)
            ob = moba_prompt(qn, knb, vt, kbar.reshape(bp, sp // MOBA_BLOCK, W_BR), bp, sp)
            return ob, kn, p[:, COL_VB * W_BR:(COL_VB + 1) * W_BR]

        def attend_sample(p, qg, kg, l=l):
            qn, kn = qk_norm(p, qg, kg, with_kv=False)
            vb = p[:, COL_VB * W_BR:(COL_VB + 1) * W_BR]
            return moba_decode(qn, kn, vb, cache_k, cache_v, page_table, l, bs, ss), kn, vb

        hp, narrow_p = rms_cast(yp, ln_g[l], w_t, l)
        hs, narrow_s = rms_cast(ys, ln_g[l], w_t, l)
        wide_p, wide_s = in_proj(hp, hs, w_t, l)
        yp, st_p = _mixer(yp, wide_p, narrow_p, lw, states_p, 0, bp, sp, attend_prompt)
        ys, st_s = _mixer(ys, wide_s, narrow_s, lw, states_s, l, bs, ss, attend_sample)
        for i in range(7):
            new_p[i].append(st_p[i])
            new_s[i].append(st_s[i])
    outs_p = [jnp.stack(t) for t in new_p]
    outs_s = [jnp.stack(t) for t in new_s]
    return (yp.reshape(bp, sp, d), ys.reshape(bs, ss, d), *outs_p, *outs_s)
```

```python
import functools

import jax
import jax.numpy as jnp
from jax import lax
from jax.experimental import pallas as pl
from jax.experimental.pallas import tpu as pltpu

F32 = jnp.float32
BF16 = jnp.bfloat16
HI = lax.Precision.HIGHEST
EPS = 1e-6
NEG_INF = float("-inf")

H_A, DK_A = 8, 128
H_B, HD_B = 8, 128
H_C, DK_C = 4, 256
CONV_W = 4
MOBA_BLOCK = 256
MOBA_TOPK = 3
PAGE_SIZE = 128
CHUNK = 64
W_BR = 1024
D_MODEL = 2048

LANES = 128
SUBLANES = 8
BF16_ROWS = 16
LOG2E = 1.4426950408889634
VMEM_LIMIT = 48 * 1024 * 1024

COL_QA, COL_KA, COL_VA, COL_ZA = 0, 1, 2, 3
COL_QB, COL_KB, COL_VB, COL_ZB = 4, 5, 6, 7
COL_QC, COL_KC, COL_VC, COL_ZC, COL_OC = 8, 9, 10, 11, 12
COL_GATE = 13
N_BIG = 19 * W_BR
OFF_SMALL_A = 4 * W_BR
OFF_SMALL_C = OFF_SMALL_A + 2 * H_A + 9 * W_BR
OFF_GATE = OFF_SMALL_C + 2 * H_C
WIN_A, WIN_C = OFF_SMALL_A // LANES, OFF_SMALL_C // LANES
LANE_BETA, LANE_ALPHA = 0, H_A
LANE_I = OFF_SMALL_C % LANES
LANE_F = LANE_I + H_C
SHIFT_A, SHIFT_BC, SHIFT_G = 0, 2 * H_A, 2 * H_A + 2 * H_C
FIRST_BC_BLOCK, FIRST_G_BLOCK = COL_QB, COL_GATE


def _params(*sem):
    return pltpu.CompilerParams(dimension_semantics=sem, vmem_limit_bytes=VMEM_LIMIT)


def _sigmoid(x):
    return 1.0 / (1.0 + jnp.exp(-x))


def _silu(x):
    return x * _sigmoid(x)


def _softplus(x):
    return jnp.maximum(x, 0.0) + jnp.log(1.0 + jnp.exp(-jnp.abs(x)))


def _dot(a, b, precision=None):
    return jnp.dot(a, b, precision=precision, preferred_element_type=F32)


def _dot_nt(a, b, precision=None):
    return lax.dot_general(a, b, (((1,), (1,)), ((), ())), precision=precision,
                           preferred_element_type=F32)


def _dot_tn(a, b, precision=None):
    return lax.dot_general(a, b, (((0,), (0,)), ((), ())), precision=precision,
                           preferred_element_type=F32)


def _cumsum_rows(x):
    rid = lax.broadcasted_iota(jnp.int32, x.shape, 0)
    d = 1
    while d < x.shape[0]:
        x = x + jnp.where(rid >= d, pltpu.roll(x, d, 0), 0.0)
        d *= 2
    return x


def _chunk_len(seq):
    return min(CHUNK, -(-seq // BF16_ROWS) * BF16_ROWS)


def _branch_dtype(block_rows):
    return BF16 if block_rows % BF16_ROWS == 0 else F32


def _pick_tile(n, candidates):
    for c in candidates:
        if n % c == 0:
            return c
    return n


def _rms_kernel(x_ref, g_ref, wa_ref, wc_ref, o_ref, ps_ref):
    x = x_ref[...]
    y = x * lax.rsqrt(jnp.mean(x * x, axis=-1, keepdims=True) + EPS)
    h = (y * g_ref[...]).astype(o_ref.dtype)
    o_ref[...] = h
    ps_ref[:, 0:LANES] = _dot_nt(h, wa_ref[0].astype(BF16))
    ps_ref[:, LANES:2 * LANES] = _dot_nt(h, wc_ref[0].astype(BF16))


def rms_cast(x, g, w_t, layer):
    t, d = x.shape
    tm = _pick_tile(t, (512, 256, 128))
    win = lambda c: pl.BlockSpec((1, LANES, d), lambda i, c=c: (layer, c, 0))
    return pl.pallas_call(
        _rms_kernel, grid=(t // tm,),
        in_specs=[pl.BlockSpec((tm, d), lambda i: (i, 0)), pl.BlockSpec((1, d), lambda i: (0, 0)),
                  win(WIN_A), win(WIN_C)],
        out_specs=[pl.BlockSpec((tm, d), lambda i: (i, 0)), pl.BlockSpec((tm, 2 * LANES), lambda i: (i, 0))],
        out_shape=[jax.ShapeDtypeStruct((t, d), BF16), jax.ShapeDtypeStruct((t, 2 * LANES), F32)],
        compiler_params=_params("arbitrary"), name="rms_cast")(x, g.reshape(1, d), w_t, w_t)


def _in_proj_kernel(a_ref, a2_ref, w_hbm, o_ref, o2_ref, wf_s, wb_s, sem, *, layer, tn):
    j = pl.program_id(0)
    n_j = pl.num_programs(0)

    def w_copy(jj, slot):
        shift = jnp.where(jj < FIRST_BC_BLOCK, SHIFT_A // SUBLANES,
                          jnp.where(jj < FIRST_G_BLOCK, SHIFT_BC // SUBLANES, SHIFT_G // SUBLANES))
        row0 = pl.multiple_of((jj * (tn // SUBLANES) + shift) * SUBLANES, SUBLANES)
        return pltpu.make_async_copy(w_hbm.at[layer, pl.ds(row0, tn), :], wf_s.at[slot], sem.at[slot])

    @pl.when(pl.program_id(1) == 0)
    def _first():
        slot = j % 2

        @pl.when(j == 0)
        def _prologue():
            w_copy(0, 0).start()

        @pl.when(j + 1 < n_j)
        def _prefetch():
            w_copy(j + 1, 1 - slot).start()

        w_copy(j, slot).wait()
        wb_s[...] = wf_s[slot].astype(BF16)
        o2_ref[...] = _dot_nt(a2_ref[...], wb_s[...])

    o_ref[...] = _dot_nt(a_ref[...], wb_s[...])


def in_proj(h, h2, w_t, layer):
    t, k = h.shape
    t2 = h2.shape[0]
    tm = _pick_tile(t, (1024, 768, 512, 256, 128))
    tn = W_BR
    n_i = t // tm
    row = lambda j, i: jnp.where(j % 2 == 0, i, n_i - 1 - i)
    return pl.pallas_call(
        functools.partial(_in_proj_kernel, layer=layer, tn=tn), grid=(N_BIG // tn, n_i),
        in_specs=[pl.BlockSpec((tm, k), lambda j, i: (row(j, i), 0)),
                  pl.BlockSpec((t2, k), lambda j, i: (0, 0)),
                  pl.BlockSpec(memory_space=pl.ANY)],
        out_specs=[pl.BlockSpec((tm, tn), lambda j, i: (row(j, i), j)),
                   pl.BlockSpec((t2, tn), lambda j, i: (0, j))],
        out_shape=[jax.ShapeDtypeStruct((t, N_BIG), F32), jax.ShapeDtypeStruct((t2, N_BIG), F32)],
        scratch_shapes=[pltpu.VMEM((2, tn, k), F32), pltpu.VMEM((tn, k), BF16), pltpu.SemaphoreType.DMA((2,))],
        compiler_params=_params("arbitrary", "arbitrary"), name="in_proj")(h, h2, w_t)


def _merge_kernel(oa_ref, ob_ref, oc_ref, zb_ref, ga_ref, gb_ref, gc_ref, wa_ref, wb_ref, wc_ref, o_ref):
    ob = ob_ref[...] * _silu(zb_ref[...])
    m = _sigmoid(ga_ref[...]) * _dot(oa_ref[...].astype(BF16), wa_ref[...])
    m += _sigmoid(gb_ref[...]) * _dot(ob.astype(BF16), wb_ref[...])
    m += _sigmoid(gc_ref[...]) * _dot(oc_ref[...].astype(BF16), wc_ref[...])
    o_ref[...] = m.astype(o_ref.dtype)


def merge(oa, ob, oc, p, wa, wb, wc):
    t = oa.shape[0]
    tm = _pick_tile(t, (512, 256, 128))
    tn = W_BR
    row = lambda j, i: (i, 0)
    return pl.pallas_call(
        _merge_kernel, grid=(D_MODEL // tn, t // tm),
        in_specs=[pl.BlockSpec((tm, W_BR), row), pl.BlockSpec((tm, W_BR), row), pl.BlockSpec((tm, W_BR), row),
                  pl.BlockSpec((tm, W_BR), lambda j, i: (i, COL_ZB)),
                  pl.BlockSpec((tm, tn), lambda j, i: (i, COL_GATE + j)),
                  pl.BlockSpec((tm, tn), lambda j, i: (i, COL_GATE + 2 + j)),
                  pl.BlockSpec((tm, tn), lambda j, i: (i, COL_GATE + 4 + j)),
                  pl.BlockSpec((W_BR, tn), lambda j, i: (0, j)),
                  pl.BlockSpec((W_BR, tn), lambda j, i: (0, j)),
                  pl.BlockSpec((W_BR, tn), lambda j, i: (0, j))],
        out_specs=pl.BlockSpec((tm, tn), lambda j, i: (i, j)),
        out_shape=jax.ShapeDtypeStruct((t, D_MODEL), BF16),
        compiler_params=_params("arbitrary", "arbitrary"), name="merge")(oa, ob, oc, p, p, p, p, wa, wb, wc)


def _outproj_kernel(x_ref, m_ref, w_ref, o_ref):
    o_ref[...] = x_ref[...] + _dot(m_ref[...], w_ref[...])


def out_proj(x, m, w):
    t, d = x.shape
    tm = _pick_tile(t, (512, 256, 128))
    tn = 1024
    return pl.pallas_call(
        _outproj_kernel, grid=(d // tn, t // tm),
        in_specs=[pl.BlockSpec((tm, tn), lambda j, i: (i, j)), pl.BlockSpec((tm, d), lambda j, i: (i, 0)),
                  pl.BlockSpec((d, tn), lambda j, i: (0, j))],
        out_specs=pl.BlockSpec((tm, tn), lambda j, i: (i, j)),
        out_shape=jax.ShapeDtypeStruct((t, d), F32),
        compiler_params=_params("arbitrary", "arbitrary"), name="out_proj")(x, m, w)


def _bdot(a, b):
    return _dot(a.astype(BF16), b.astype(BF16))


DELTA_CHUNKS_PER_STEP = 4


def _delta_kernel(q_ref, k_ref, v_ref, z_ref, sm_ref, cw_ref, cb_ref, al_ref, dtb_ref, s0_ref, ng_ref,
                  o_ref, cn_ref, sn_ref, xp_ref, s_ref, *, c_in, chunk):
    c = pl.program_id(1)
    C = chunk
    hd = DK_A

    @pl.when(c == 0)
    def _init():
        xp_ref[...] = jnp.zeros(xp_ref.shape, F32)
        xp_ref[0:SUBLANES, :] = cb_ref[0]
        s_ref[...] = s0_ref[0]

    xp_ref[SUBLANES:SUBLANES + c_in, 0:W_BR] = q_ref[...]
    xp_ref[SUBLANES:SUBLANES + c_in, W_BR:2 * W_BR] = k_ref[...]
    xp_ref[SUBLANES:SUBLANES + c_in, 2 * W_BR:3 * W_BR] = v_ref[...]

    nch = max(c_in // C, 1)
    padded = c_in < C
    valid = lax.broadcasted_iota(jnp.int32, (C, 1), 0) < c_in
    mask = (lambda x, fill: jnp.where(valid, x, fill)) if padded else (lambda x, fill: x)
    ri = lax.broadcasted_iota(jnp.int32, (C, C), 0)
    ci = lax.broadcasted_iota(jnp.int32, (C, C), 1)
    causal = ci <= ri
    strict = ci < ri

    beta_c, gcum_c, gcum_t_c, e_g_c, e_end_c, e_last_c = [], [], [], [], [], []
    for kc in range(nch):
        if padded:
            sm = jnp.concatenate([sm_ref[...], jnp.zeros((C - c_in, LANES), F32)], axis=0)
        else:
            sm = sm_ref[kc * C:(kc + 1) * C, :]
        g = mask(-jnp.exp(al_ref[...]) * _softplus(sm + dtb_ref[...]), 0.0)
        gcum = _cumsum_rows(g)
        g_last = gcum[C - 1:C, :]
        beta_c.append(_sigmoid(sm))
        gcum_c.append(gcum)
        gcum_t_c.append(gcum.T)
        e_g_c.append(jnp.exp(gcum))
        e_end_c.append(jnp.exp(g_last - gcum))
        e_last_c.append(jnp.exp(g_last))

    def conv(kc, col):
        sl = slice(col, col + hd)
        r0 = 5 + kc * C
        y = xp_ref[r0:r0 + C, sl] * cw_ref[0:1, sl]
        for j in range(1, CONV_W):
            y = y + xp_ref[r0 + j:r0 + j + C, sl] * cw_ref[j:j + 1, sl]
        return mask(_silu(y), 0.0)

    heads = range(H_A)
    pairs = [(kc, h) for kc in range(nch) for h in heads]
    npair = range(len(pairs))
    col = lambda vec, kc, lane: vec[kc][:, lane:lane + 1]
    q_l = [conv(kc, h * hd) for kc, h in pairs]
    k_l = [conv(kc, W_BR + h * hd) for kc, h in pairs]
    v_l = [conv(kc, 2 * W_BR + h * hd) for kc, h in pairs]
    q_l = [q * lax.rsqrt(jnp.sum(q * q, axis=-1, keepdims=True) + EPS) * (DK_A ** -0.5) for q in q_l]
    k_l = [k * lax.rsqrt(jnp.sum(k * k, axis=-1, keepdims=True) + EPS) for k in k_l]
    decay_l = [jnp.exp(jnp.where(causal, col(gcum_c, kc, LANE_ALPHA + h)
                                 - gcum_t_c[kc][LANE_ALPHA + h:LANE_ALPHA + h + 1, :], NEG_INF))
               for kc, h in pairs]
    b_l = [col(beta_c, kc, LANE_BETA + h) for kc, h in pairs]
    eg_l = [col(e_g_c, kc, LANE_ALPHA + h) for kc, h in pairs]
    kb_l = [k.astype(BF16) for k in k_l]
    qk_kk = [_dot_nt(jnp.concatenate([q_l[i].astype(BF16), kb_l[i]], axis=0), kb_l[i]) for i in npair]
    a_l = [jnp.where(strict, b_l[i] * decay_l[i] * qk_kk[i][C:2 * C], 0.0) for i in npair]
    r_l = [-a for a in a_l]
    pw_l = a_l
    n = 2
    while n < C:
        pw_l = [_bdot(pw, pw) for pw in pw_l]
        r_l = [r_l[i] + pw_l[i] + _bdot(r_l[i], pw_l[i]) for i in npair]
        n *= 2
    rhs_l = [jnp.concatenate([b_l[i] * v_l[i], (b_l[i] * eg_l[i]) * k_l[i]], axis=1) for i in npair]
    sol_l = [rhs_l[i] + _bdot(r_l[i], rhs_l[i]) for i in npair]
    kq_l = [jnp.concatenate([sol_l[i][:, hd:2 * hd], q_l[i] * eg_l[i]], axis=0).astype(BF16) for i in npair]
    p_l = [(decay_l[i] * qk_kk[i][0:C]).astype(BF16) for i in npair]
    kend_l = [(k_l[i] * col(e_end_c, kc, LANE_ALPHA + h)).astype(BF16) for i, (kc, h) in enumerate(pairs)]

    s_l = [s_ref[h] for h in heads]
    o_l = []
    for kc in range(nch):
        idx = [kc * H_A + h for h in heads]
        kq_s = [_dot(kq_l[i], s_l[h].astype(BF16)) for h, i in enumerate(idx)]
        wb_l = [(sol_l[i][:, 0:hd] - kq_s[h][0:C]).astype(BF16) for h, i in enumerate(idx)]
        o_l += [kq_s[h][C:2 * C] + _dot(p_l[i], wb_l[h]) for h, i in enumerate(idx)]
        s_l = [col(e_last_c, kc, LANE_ALPHA + h) * s_l[h] + _dot_tn(kend_l[i], wb_l[h])
               for h, i in enumerate(idx)]
    for h in heads:
        s_ref[h] = s_l[h]
    for i, (kc, h) in enumerate(pairs):
        o = o_l[i]
        o = o * lax.rsqrt(jnp.mean(o * o, axis=-1, keepdims=True) + EPS) * ng_ref[...]
        rows = slice(0, c_in) if padded else slice(kc * C, (kc + 1) * C)
        o = o[0:min(c_in, C)] * _silu(z_ref[rows, h * hd:(h + 1) * hd])
        o_ref[rows, h * hd:(h + 1) * hd] = o.astype(o_ref.dtype)

    tail = xp_ref[c_in:c_in + SUBLANES, :]
    xp_ref[0:SUBLANES, :] = tail

    @pl.when(c == pl.num_programs(1) - 1)
    def _fin():
        cn_ref[0] = tail
        sn_ref[0] = s_ref[...]


def delta_branch(p, ps, conv_w, conv_buf, a_log, dt_bias, s0, st, norm_g, bsz, seq):
    chunk = _chunk_len(seq)
    c_in = min(seq, DELTA_CHUNKS_PER_STEP * chunk)
    assert seq % c_in == 0 and (c_in % chunk == 0 or c_in < chunk)
    n_chunks = seq // c_in
    lane_vec = lambda v: jnp.zeros((1, LANES), F32).at[0, LANE_ALPHA:LANE_ALPHA + H_A].set(v)
    rb = lambda col: pl.BlockSpec((c_in, W_BR), lambda b, c, col=col: (b * n_chunks + c, col))
    const2 = lambda b, c: (0, 0)
    o, cn, sn = pl.pallas_call(
        functools.partial(_delta_kernel, c_in=c_in, chunk=chunk), grid=(bsz, n_chunks),
        in_specs=[rb(COL_QA), rb(COL_KA), rb(COL_VA), rb(COL_ZA),
                  pl.BlockSpec((c_in, LANES), lambda b, c: (b * n_chunks + c, 0)),
                  pl.BlockSpec((CONV_W, 3 * W_BR), const2),
                  pl.BlockSpec((None, 1, SUBLANES, 3 * W_BR), lambda b, c: (st, b, 0, 0)),
                  pl.BlockSpec((1, LANES), const2), pl.BlockSpec((1, LANES), const2),
                  pl.BlockSpec((None, 1, H_A, DK_A, DK_A), lambda b, c: (st, b, 0, 0, 0)),
                  pl.BlockSpec((1, DK_A), const2)],
        out_specs=[pl.BlockSpec((c_in, W_BR), lambda b, c: (b * n_chunks + c, 0)),
                   pl.BlockSpec((1, SUBLANES, 3 * W_BR), lambda b, c: (b, 0, 0)),
                   pl.BlockSpec((1, H_A, DK_A, DK_A), lambda b, c: (b, 0, 0, 0))],
        out_shape=[jax.ShapeDtypeStruct((bsz * seq, W_BR), _branch_dtype(c_in)),
                   jax.ShapeDtypeStruct((bsz, SUBLANES, 3 * W_BR), F32),
                   jax.ShapeDtypeStruct((bsz, H_A, DK_A, DK_A), F32)],
        scratch_shapes=[pltpu.VMEM((SUBLANES + max(c_in, chunk), 3 * W_BR), F32),
                        pltpu.VMEM((H_A, DK_A, DK_A), F32)],
        compiler_params=_params("arbitrary", "arbitrary"), name="delta")(
            p, p, p, p, ps, conv_w, conv_buf, lane_vec(a_log), lane_vec(dt_bias), s0, norm_g.reshape(1, DK_A))
    return o, cn[:, SUBLANES - (CONV_W - 1):], sn


MLSTM_CHUNKS_PER_STEP = 4


def _mlstm_kernel(q_ref, k_ref, v_ref, z_ref, og_ref, sm_ref, bif_ref, c0_ref, n0_ref, m0_ref, ng_ref,
                  o_ref, cn_ref, nn_ref, mn_ref, c_s, n_s, m_s, *, c_in, chunk):
    c = pl.program_id(1)
    C = chunk
    hd = DK_C

    @pl.when(c == 0)
    def _init():
        c_s[...] = c0_ref[0]
        n_s[...] = n0_ref[0]
        m_s[...] = m0_ref[0]

    nch = max(c_in // C, 1)
    padded = c_in < C

    def rows_of(ref, kc, sl):
        if padded:
            return jnp.concatenate([ref[:, sl], jnp.zeros((C - c_in, sl.stop - sl.start), F32)], axis=0)
        return ref[kc * C:(kc + 1) * C, sl]

    valid = lax.broadcasted_iota(jnp.int32, (C, 1), 0) < c_in
    ri = lax.broadcasted_iota(jnp.int32, (C, C), 0)
    ci = lax.broadcasted_iota(jnp.int32, (C, C), 1)
    causal = ci <= ri
    m_old = m_s[...]
    lane = lax.broadcasted_iota(jnp.int32, m_old.shape, 1)
    heads = range(H_C)
    hsl = [slice(h * hd, (h + 1) * hd) for h in heads]
    pairs = [(kc, h) for kc in range(nch) for h in heads]

    i_c, i_t_c, b_c, b_t_c = [], [], [], []
    for kc in range(nch):
        pre = rows_of(sm_ref, kc, slice(0, LANES)) + bif_ref[...]
        i_pre = jnp.where(valid, pre, NEG_INF) if padded else pre
        log_f = -_softplus(-pre)
        if padded:
            log_f = jnp.where(valid, log_f, 0.0)
        bcum = _dot(causal.astype(F32), log_f, HI)
        i_c.append(i_pre)
        i_t_c.append(i_pre.T)
        b_c.append(bcum)
        b_t_c.append(bcum.T)

    q_l = [rows_of(q_ref, kc, hsl[h]) for kc, h in pairs]
    ks_l = [rows_of(k_ref, kc, hsl[h]) * (DK_C ** -0.5) for kc, h in pairs]
    vb_l = [rows_of(v_ref, kc, hsl[h]).astype(BF16) for kc, h in pairs]
    qb_l = [q.astype(BF16) for q in q_l]
    qk_l = [_dot_nt(qb_l[i], ks_l[i].astype(BF16)) for i in range(len(pairs))]
    bcol_l = [b_c[kc][:, LANE_F + h:LANE_F + h + 1] for kc, h in pairs]
    icol_l = [i_c[kc][:, LANE_I + h:LANE_I + h + 1] for kc, h in pairs]
    intra_l = [jnp.where(causal, bcol_l[i] - b_t_c[kc][LANE_F + h:LANE_F + h + 1, :]
                         + i_t_c[kc][LANE_I + h:LANE_I + h + 1, :], NEG_INF) for i, (kc, h) in enumerate(pairs)]
    imax_l = [jnp.max(x, axis=-1, keepdims=True) for x in intra_l]
    m_prev = [m_old[:, h:h + 1] for h in heads]
    mt_l, winter_l, wc_l, wj_l = [], [], [], []
    for i, (kc, h) in enumerate(pairs):
        inter = bcol_l[i] + m_prev[h]
        m_t = jnp.maximum(inter, imax_l[i])
        m_new = m_t[C - 1:C, :]
        b_last = bcol_l[i][C - 1:C, :]
        mt_l.append(m_t)
        winter_l.append(jnp.exp(inter - m_t))
        wc_l.append(jnp.exp(b_last + m_prev[h] - m_new))
        wj_l.append(jnp.exp(b_last - bcol_l[i] + icol_l[i] - m_new))
        m_prev[h] = m_new
    m_new_vec = m_old
    for h in heads:
        m_new_vec = jnp.where(lane == h, m_prev[h], m_new_vec)
    m_s[...] = m_new_vec

    s_l = [qk_l[i] * jnp.exp(intra_l[i] - mt_l[i]) for i in range(len(pairs))]
    sv_l = [_dot(s_l[i].astype(BF16), vb_l[i]) for i in range(len(pairs))]
    ssum_l = [jnp.sum(s, axis=-1, keepdims=True) for s in s_l]
    kw_l = [wj_l[i] * ks_l[i] for i in range(len(pairs))]
    upd_l = [_dot_tn(kw_l[i].astype(BF16), vb_l[i]) for i in range(len(pairs))]
    nsum_l = [jnp.sum(kw, axis=0, keepdims=True) for kw in kw_l]

    cm_l = [c_s[h] for h in heads]
    n_l = [n_s[h:h + 1, :] for h in heads]
    for i, (kc, h) in enumerate(pairs):
        num = winter_l[i] * _dot(qb_l[i], cm_l[h].astype(BF16)) + sv_l[i]
        den = winter_l[i] * jnp.sum(q_l[i] * n_l[h], axis=-1, keepdims=True) + ssum_l[i]
        hh = num / jnp.maximum(jnp.abs(den), jnp.exp(-mt_l[i]))
        cm_l[h] = wc_l[i] * cm_l[h] + upd_l[i]
        n_l[h] = wc_l[i] * n_l[h] + nsum_l[i]
        rows = slice(0, c_in) if padded else slice(kc * C, (kc + 1) * C)
        hc = _sigmoid(og_ref[rows, hsl[h]]) * hh[0:min(c_in, C)]
        hc = hc * lax.rsqrt(jnp.mean(hc * hc, axis=-1, keepdims=True) + EPS) * ng_ref[...]
        o_ref[rows, hsl[h]] = (hc * _silu(z_ref[rows, hsl[h]])).astype(o_ref.dtype)
    for h in heads:
        c_s[h] = cm_l[h]
        n_s[h:h + 1, :] = n_l[h]

    @pl.when(c == pl.num_programs(1) - 1)
    def _fin():
        cn_ref[0] = c_s[...]
        nn_ref[0] = n_s[...]
        mn_ref[0] = m_new_vec


def mlstm_branch(p, ps, b_if, c0, n0, m0p, st, norm_g, bsz, seq):
    chunk = _chunk_len(seq)
    c_in = min(seq, MLSTM_CHUNKS_PER_STEP * chunk)
    assert seq % c_in == 0 and (c_in % chunk == 0 or c_in < chunk)
    n_chunks = seq // c_in
    bif = jnp.zeros((1, LANES), F32).at[0, LANE_I:LANE_I + 2 * H_C].set(b_if)
    rb = lambda col: pl.BlockSpec((c_in, W_BR), lambda b, c, col=col: (b * n_chunks + c, col))
    const2 = lambda b, c: (0, 0)
    st4 = pl.BlockSpec((1, H_C, DK_C, DK_C), lambda b, c: (b, 0, 0, 0))
    st3 = pl.BlockSpec((1, H_C, DK_C), lambda b, c: (b, 0, 0))
    stm = pl.BlockSpec((1, 1, LANES), lambda b, c: (b, 0, 0))
    in4 = pl.BlockSpec((None, 1, H_C, DK_C, DK_C), lambda b, c: (st, b, 0, 0, 0))
    in3 = pl.BlockSpec((None, 1, H_C, DK_C), lambda b, c: (st, b, 0, 0))
    inm = pl.BlockSpec((None, 1, 1, LANES), lambda b, c: (st, b, 0, 0))
    o, cn, nn, mn = pl.pallas_call(
        functools.partial(_mlstm_kernel, c_in=c_in, chunk=chunk), grid=(bsz, n_chunks),
        in_specs=[rb(COL_QC), rb(COL_KC), rb(COL_VC), rb(COL_ZC), rb(COL_OC),
                  pl.BlockSpec((c_in, LANES), lambda b, c: (b * n_chunks + c, 1)),
                  pl.BlockSpec((1, LANES), const2), in4, in3, inm, pl.BlockSpec((1, DK_C), const2)],
        out_specs=[pl.BlockSpec((c_in, W_BR), lambda b, c: (b * n_chunks + c, 0)), st4, st3, stm],
        out_shape=[jax.ShapeDtypeStruct((bsz * seq, W_BR), _branch_dtype(c_in)),
                   jax.ShapeDtypeStruct((bsz, H_C, DK_C, DK_C), F32),
                   jax.ShapeDtypeStruct((bsz, H_C, DK_C), F32),
                   jax.ShapeDtypeStruct((bsz, 1, LANES), F32)],
        scratch_shapes=[pltpu.VMEM((H_C, DK_C, DK_C), F32), pltpu.VMEM((H_C, DK_C), F32),
                        pltpu.VMEM((1, LANES), F32)],
        compiler_params=_params("arbitrary", "arbitrary"), name="mlstm")(
            p, p, p, p, p, ps, bif, c0, n0, m0p, norm_g.reshape(1, DK_C))
    return o, cn, nn, mn[:, 0, :H_C]


def _head_rms(x, g):
    return x * lax.rsqrt(jnp.mean(x * x, axis=-1, keepdims=True) + EPS) * g


def _qknorm_kernel(q_ref, k_ref, qg_ref, kg_ref, qn_ref, kn_ref):
    for h in range(H_B):
        sl = slice(h * HD_B, (h + 1) * HD_B)
        qn_ref[:, sl] = _head_rms(q_ref[:, sl], qg_ref[...]) * (HD_B ** -0.5)
        kn_ref[:, sl] = _head_rms(k_ref[:, sl], kg_ref[...])


def _qknorm_kv_kernel(q_ref, k_ref, v_ref, qg_ref, kg_ref, qn_ref, kn_ref, knb_ref, vt_ref, kbar_ref):
    for h in range(H_B):
        sl = slice(h * HD_B, (h + 1) * HD_B)
        qn_ref[:, sl] = _head_rms(q_ref[:, sl], qg_ref[...]) * (HD_B ** -0.5)
        kn = _head_rms(k_ref[:, sl], kg_ref[...])
        kn_ref[:, sl] = kn
        knb_ref[0, :, sl] = kn.astype(BF16)
        kbar_ref[0, :, sl] = jnp.mean(kn, axis=0, keepdims=True)
    eye = (lax.broadcasted_iota(jnp.int32, (HD_B, HD_B), 0)
           == lax.broadcasted_iota(jnp.int32, (HD_B, HD_B), 1)).astype(BF16)
    for h in range(H_B):
        sl = slice(h * HD_B, (h + 1) * HD_B)
        vt_ref[0, sl, :] = _dot_nt(eye, v_ref[:, sl].astype(BF16)).astype(BF16)


def qk_norm(p, qg, kg, with_kv):
    t = p.shape[0]
    qg = qg.reshape(1, HD_B)
    kg = kg.reshape(1, HD_B)
    gs = pl.BlockSpec((1, HD_B), lambda i: (0, 0))
    if not with_kv:
        tm = _pick_tile(t, (256, 128))
        return pl.pallas_call(
            _qknorm_kernel, grid=(t // tm,),
            in_specs=[pl.BlockSpec((tm, W_BR), lambda i: (i, COL_QB)),
                      pl.BlockSpec((tm, W_BR), lambda i: (i, COL_KB)), gs, gs],
            out_specs=[pl.BlockSpec((tm, W_BR), lambda i: (i, 0))] * 2,
            out_shape=[jax.ShapeDtypeStruct((t, W_BR), F32)] * 2,
            compiler_params=_params("arbitrary"), name="qk_norm")(p, p, qg, kg)
    tm = MOBA_BLOCK
    nb = t // tm
    return pl.pallas_call(
        _qknorm_kv_kernel, grid=(nb,),
        in_specs=[pl.BlockSpec((tm, W_BR), lambda i: (i, COL_QB)),
                  pl.BlockSpec((tm, W_BR), lambda i: (i, COL_KB)),
                  pl.BlockSpec((tm, W_BR), lambda i: (i, COL_VB)), gs, gs],
        out_specs=[pl.BlockSpec((tm, W_BR), lambda i: (i, 0)), pl.BlockSpec((tm, W_BR), lambda i: (i, 0)),
                   pl.BlockSpec((1, tm, W_BR), lambda i: (i, 0, 0)),
                   pl.BlockSpec((1, W_BR, tm), lambda i: (i, 0, 0)),
                   pl.BlockSpec((1, 1, W_BR), lambda i: (i, 0, 0))],
        out_shape=[jax.ShapeDtypeStruct((t, W_BR), F32), jax.ShapeDtypeStruct((t, W_BR), F32),
                   jax.ShapeDtypeStruct((nb, tm, W_BR), BF16), jax.ShapeDtypeStruct((nb, W_BR, tm), BF16),
                   jax.ShapeDtypeStruct((nb, 1, W_BR), F32)],
        compiler_params=_params("arbitrary"), name="qk_norm_kv")(p, p, p, qg, kg)


def _select_topk(sc, valid, axis):
    nb = sc.shape[axis]
    idx = lax.broadcasted_iota(jnp.int32, sc.shape, axis).astype(F32)
    if valid is not None:
        sc = jnp.where(valid, sc, NEG_INF)
    sel = jnp.zeros(sc.shape, F32)
    for _ in range(MOBA_TOPK):
        mx = jnp.max(sc, axis=axis, keepdims=True)
        first = jnp.min(jnp.where(sc == mx, idx, float(nb)), axis=axis, keepdims=True)
        hit = idx == first
        sel = jnp.where(hit & (mx > NEG_INF), 1.0, sel)
        sc = jnp.where(hit, NEG_INF, sc)
    return sel


MOBA_HEADS_PER_STEP = 8


def _moba_prompt_kernel(q_ref, k_ref, vt_ref, kbar_ref, o_ref, sel_ref, qb_ref, acc_ref):
    i = pl.program_id(2)
    blk = MOBA_BLOCK
    hp = MOBA_HEADS_PER_STEP
    nb = kbar_ref.shape[1]
    past = lax.broadcasted_iota(jnp.int32, (nb, blk), 0) < i
    kpos = lax.broadcasted_iota(jnp.int32, (blk, blk), 0)
    qpos = lax.broadcasted_iota(jnp.int32, (blk, blk), 1)
    heads = [slice(hh * HD_B, (hh + 1) * HD_B) for hh in range(hp)]

    ones = jnp.ones((BF16_ROWS, blk), BF16)

    def pv(n, sl, pr):
        return _dot(jnp.concatenate([vt_ref[n, sl, :], ones], axis=0), pr)

    for hh, sl in enumerate(heads):
        qb_ref[hh] = (q_ref[:, sl] * LOG2E).astype(BF16)
    s_l = [_dot_nt(k_ref[i, :, sl], qb_ref[hh]) for hh, sl in enumerate(heads)]
    sc_l = [_dot_nt(kbar_ref[0, :, sl], q_ref[:, sl], HI) for sl in heads]
    ms, pr_l = [], []
    for hh in range(hp):
        s = jnp.where(kpos <= qpos, s_l[hh], NEG_INF)
        m = jnp.max(s, axis=0, keepdims=True)
        ms.append(m)
        pr_l.append(jnp.exp2(s - m).astype(BF16))
    for hh, sl in enumerate(heads):
        acc_ref[hh] = pv(i, sl, pr_l[hh])
        sel_ref[hh] = _select_topk(sc_l[hh], past, 0)

    def past_blocks(ns, ms):
        s_l = [[_dot_nt(k_ref[n, :, sl], qb_ref[hh]) for n in ns] for hh, sl in enumerate(heads)]
        ones_n = jnp.ones((BF16_ROWS, len(ns) * blk), BF16)
        ms_new, pr_l, alpha_l = [], [], []
        for hh in range(hp):
            ss = [jnp.where(sel_ref[hh, pl.ds(n, 1), :] > 0.0, s, NEG_INF) for n, s in zip(ns, s_l[hh])]
            m_new = ms[hh]
            for s in ss:
                m_new = jnp.maximum(m_new, jnp.max(s, axis=0, keepdims=True))
            alpha_l.append(jnp.exp2(ms[hh] - m_new))
            ms_new.append(m_new)
            pr_l.append(jnp.concatenate([jnp.exp2(s - m_new).astype(BF16) for s in ss], axis=0))
        for hh, sl in enumerate(heads):
            lhs = jnp.concatenate([jnp.concatenate([vt_ref[n, sl, :] for n in ns], axis=1), ones_n], axis=0)
            acc_ref[hh] = alpha_l[hh] * acc_ref[hh] + _dot(lhs, pr_l[hh])
        return tuple(ms_new)

    odd = i % 2
    ms = lax.cond(odd == 1, lambda ms: past_blocks([0], ms), lambda ms: ms, tuple(ms))
    lax.fori_loop(0, i // 2, lambda t, ms: past_blocks([2 * t + odd, 2 * t + odd + 1], ms), ms)
    for hh, sl in enumerate(heads):
        o_ref[:, sl] = (acc_ref[hh, 0:HD_B, :] / acc_ref[hh, HD_B:HD_B + 1, :]).T.astype(o_ref.dtype)


def moba_prompt(qn, knb, vt, kbar, bsz, seq):
    nb = seq // MOBA_BLOCK
    blk = MOBA_BLOCK
    hp = MOBA_HEADS_PER_STEP
    wid = hp * HD_B
    return pl.pallas_call(
        _moba_prompt_kernel, grid=(bsz, H_B // hp, nb),
        in_specs=[pl.BlockSpec((blk, wid), lambda b, h, i: (b * nb + i, h)),
                  pl.BlockSpec((nb, blk, wid), lambda b, h, i: (b, 0, h)),
                  pl.BlockSpec((nb, wid, blk), lambda b, h, i: (b, h, 0)),
                  pl.BlockSpec((1, nb, wid), lambda b, h, i: (b, 0, h))],
        out_specs=pl.BlockSpec((blk, wid), lambda b, h, i: (b * nb + i, h)),
        out_shape=jax.ShapeDtypeStruct((bsz * seq, W_BR), _branch_dtype(blk)),
        scratch_shapes=[pltpu.VMEM((hp, nb, blk), F32), pltpu.VMEM((hp, blk, HD_B), BF16),
                        pltpu.VMEM((hp, HD_B + BF16_ROWS, blk), F32)],
        compiler_params=_params("arbitrary", "arbitrary", "arbitrary"), name="moba_prompt")(qn, knb, vt, kbar)


DECODE_PAGES_PER_STEP = 16


def _moba_decode_kernel(pt_ref, q_ref, qf_ref, bias_ref, obias_ref, kown_ref, vown_ref, *rest, pps, seq):
    del pt_ref
    k_refs = rest[:pps]
    v_refs = rest[pps:2 * pps]
    o_ref = rest[2 * pps]
    m_s, l_s, o_s, ks_s = rest[2 * pps + 1:]
    n = pl.program_id(1)
    q = q_ref[0]
    ppb = MOBA_BLOCK // PAGE_SIZE
    npg = ks_s.shape[0] // H_B
    lane = lax.broadcasted_iota(jnp.int32, m_s.shape, 1)

    def partial_softmax(s):
        m = jnp.max(s, axis=-1, keepdims=True)
        pr = jnp.exp(s - m)
        return m, jnp.sum(pr, axis=-1, keepdims=True), pr

    @pl.when(n == 0)
    def _init():
        m_s[...] = jnp.zeros(m_s.shape, F32)
        l_s[...] = jnp.zeros(l_s.shape, F32)

    m_all = m_s[...]
    l_all = l_s[...]
    s_l = [_dot_nt(q, k_refs[j][0, 0].astype(BF16)) for j in range(pps)]
    pr_l = []
    for j in range(pps):
        m, l, pr = partial_softmax(s_l[j] + bias_ref[...])
        pg = n * pps + j
        m_all = jnp.where(lane == pg, m, m_all)
        l_all = jnp.where(lane == pg, l, l_all)
        pr_l.append(pr.astype(BF16))
    for j in range(pps):
        pg = n * pps + j
        o_s[pg] = _dot(pr_l[j], v_refs[j][0, 0].astype(BF16))
        ks_s[pl.ds(pg * H_B, H_B), :] = jnp.sum(k_refs[j][0, 0].reshape(PAGE_SIZE, H_B, HD_B), axis=0)
    m_s[...] = m_all
    l_s[...] = l_all

    @pl.when(n == pl.num_programs(1) - 1)
    def _combine():
        nb = npg // ppb
        m_o, l_o, pr_o = partial_softmax(_dot_nt(q, kown_ref[0].astype(BF16)) + obias_ref[...])
        o_s[npg] = _dot(pr_o.astype(BF16), vown_ref[0].astype(BF16))
        m_sl = jnp.where(lane == npg, m_o, m_all)
        l_sl = jnp.where(lane == npg, l_o, l_all)
        scp = _dot_nt(qf_ref[0], ks_s[...], HI)
        ch = lax.broadcasted_iota(jnp.int32, scp.shape, 1) % H_B
        rh = lax.broadcasted_iota(jnp.int32, scp.shape, 0) // seq
        scp = jnp.where(ch == rh, scp, 0.0)
        cols_per_blk = ppb * H_B
        gather = (lax.broadcasted_iota(jnp.int32, (npg * H_B, nb), 0) // cols_per_blk
                  == lax.broadcasted_iota(jnp.int32, (npg * H_B, nb), 1)).astype(F32)
        scb = _dot(scp, gather, HI) * (1.0 / MOBA_BLOCK)
        sel = _select_topk(scb, None, 1)
        slot = lax.broadcasted_iota(jnp.int32, (nb, m_s.shape[1]), 1)
        expand = (slot // ppb == lax.broadcasted_iota(jnp.int32, (nb, m_s.shape[1]), 0)) & (slot < npg)
        sel_slot = jnp.where(lane == npg, 1.0, _dot(sel.astype(BF16), expand.astype(BF16)))
        m_sel = jnp.where(sel_slot > 0.0, m_sl, NEG_INF)
        w = sel_slot * jnp.exp(m_sel - jnp.max(m_sel, axis=-1, keepdims=True))
        w = w / jnp.sum(w * l_sl, axis=-1, keepdims=True)
        acc = w[:, 0:1] * o_s[0]
        for pg in range(1, npg + 1):
            acc = acc + w[:, pg:pg + 1] * o_s[pg]
        o_ref[0] = acc


def moba_decode(qn, kn, v, cache_k, cache_v, page_table, layer, bsz, seq):
    ppb = MOBA_BLOCK // PAGE_SIZE
    pps = DECODE_PAGES_PER_STEP
    n_pages = page_table.shape[1]
    nrow = H_B * seq
    assert nrow % SUBLANES == 0 and n_pages // ppb >= MOBA_TOPK and n_pages % ppb == 0
    assert n_pages % pps == 0 and n_pages < LANES
    rows = PAGE_SIZE * H_B
    depth, n_phys = cache_k.shape[:2]
    ck = cache_k.reshape(depth, n_phys, rows, HD_B)
    cv = cache_v.reshape(depth, n_phys, rows, HD_B)
    qc = jnp.transpose(qn.reshape(bsz, seq, H_B, HD_B), (0, 2, 1, 3)).reshape(bsz, nrow, HD_B)
    c_h, c_q = jnp.arange(nrow) // seq, jnp.arange(nrow) % seq
    key_h = jnp.arange(rows) % H_B
    bias = jnp.where(c_h[:, None] == key_h[None, :], 0.0, NEG_INF).astype(F32)
    own = jnp.arange(seq * H_B)
    obias = jnp.where((c_h[:, None] == own[None, :] % H_B) & (own[None, :] // H_B <= c_q[:, None]),
                      0.0, NEG_INF).astype(F32)
    kown = kn.reshape(bsz, seq * H_B, HD_B)
    vown = v.reshape(bsz, seq * H_B, HD_B)

    page = lambda j: pl.BlockSpec((1, 1, rows, HD_B), lambda b, n, pt, j=j: (layer, pt[b, n * pps + j], 0, 0))
    seq3 = lambda shape: pl.BlockSpec((1,) + shape, lambda b, n, pt: (b, 0, 0))
    const2 = lambda shape: pl.BlockSpec(shape, lambda b, n, pt: (0, 0))
    grid_spec = pltpu.PrefetchScalarGridSpec(
        num_scalar_prefetch=1, grid=(bsz, n_pages // pps),
        in_specs=[seq3((nrow, HD_B)), seq3((nrow, HD_B)), const2((nrow, rows)), const2((nrow, seq * H_B)),
                  seq3((seq * H_B, HD_B)), seq3((seq * H_B, HD_B))]
                 + [page(j) for j in range(pps)] + [page(j) for j in range(pps)],
        out_specs=seq3((nrow, HD_B)),
        scratch_shapes=[pltpu.VMEM((nrow, LANES), F32), pltpu.VMEM((nrow, LANES), F32),
                        pltpu.VMEM((n_pages + 1, nrow, HD_B), F32), pltpu.VMEM((n_pages * H_B, HD_B), F32)])
    out = pl.pallas_call(
        functools.partial(_moba_decode_kernel, pps=pps, seq=seq), grid_spec=grid_spec,
        out_shape=jax.ShapeDtypeStruct((bsz, nrow, HD_B), F32),
        compiler_params=_params("arbitrary", "arbitrary"), name="moba_decode")(
            page_table, qc.astype(BF16), qc, bias, obias, kown, vown, *([ck] * pps), *([cv] * pps))
    return jnp.transpose(out.reshape(bsz, H_B, seq, HD_B), (0, 2, 1, 3)).reshape(bsz * seq, W_BR)


def _mixer(x, p, ps, lw, states, st, bsz, seq, attend):
    (conv_a, a_log, dt_bias, norm_a, qnorm_b, knorm_b, b_if, norm_c, wa, wb, wc, wo) = lw
    conv_buf, s_a, c_c, n_c, m_c = states
    oa, conv_new, s_new = delta_branch(p, ps, conv_a, conv_buf, a_log, dt_bias, s_a, st, norm_a, bsz, seq)
    ob, kn, vb = attend(p, qnorm_b, knorm_b)
    oc, c_new, n_new, m_new = mlstm_branch(p, ps, b_if, c_c, n_c, m_c, st, norm_c, bsz, seq)
    y = out_proj(x, merge(oa, ob, oc, p, wa, wb, wc), wo)
    kv_shape = (bsz, seq, H_B, HD_B)
    return y, (kn.reshape(kv_shape), vb.reshape(kv_shape), conv_new, s_new, c_new, n_new, m_new)


def _pad_states(conv, delta, c, n, m):
    conv = jnp.pad(conv, ((0, 0), (0, 0), (SUBLANES - (CONV_W - 1), 0), (0, 0)))
    m = jnp.pad(m, ((0, 0), (0, 0), (0, LANES - H_C)))[:, :, None, :]
    return conv, delta, c, n, m


def kernel(x_prompt, x_sample, cache_k, cache_v, page_table, state_conv_a, state_delta_a, state_mlstm_c,
           state_mlstm_n, state_mlstm_m, ln_g, w_in, conv_a, a_log, dt_bias, norm_a, qnorm_b, knorm_b,
           b_if, norm_c, w_br_a, w_br_b, w_br_c, w_out):
    bp, sp, d = x_prompt.shape
    bs, ss, _ = x_sample.shape
    depth = w_in.shape[0]
    yp = x_prompt.reshape(bp * sp, d)
    ys = x_sample.reshape(bs * ss, d)
    new_p = [[] for _ in range(7)]
    new_s = [[] for _ in range(7)]
    w_t = jnp.swapaxes(w_in, 1, 2)
    zeros = lambda *s: jnp.zeros(s, F32)
    states_p = _pad_states(zeros(1, bp, CONV_W - 1, 3 * W_BR), zeros(1, bp, H_A, DK_A, DK_A),
                           zeros(1, bp, H_C, DK_C, DK_C), zeros(1, bp, H_C, DK_C), zeros(1, bp, H_C))
    states_s = _pad_states(state_conv_a, state_delta_a, state_mlstm_c, state_mlstm_n, state_mlstm_m)
    for l in range(depth):
        lw = (conv_a[l], a_log[l], dt_bias[l], norm_a[l], qnorm_b[l],
              knorm_b[l], b_if[l], norm_c[l], w_br_a[l].astype(BF16), w_br_b[l].astype(BF16),
              w_br_c[l].astype(BF16), w_out[l].astype(BF16))

        def attend_prompt(p, qg, kg):
            qn, kn, knb, vt, kbar = qk_norm(p, qg, kg, with_kv=True)
            ob = moba_prompt(qn, knb, vt, kbar.reshape(bp, sp // MOBA_BLOCK, W_BR), bp, sp)
            return ob, kn, p[:, COL_VB * W_BR:(COL_VB + 1) * W_BR]

        def attend_sample(p, qg, kg, l=l):
            qn, kn = qk_norm(p, qg, kg, with_kv=False)
            vb = p[:, COL_VB * W_BR:(COL_VB + 1) * W_BR]
            return moba_decode(qn, kn, vb, cache_k, cache_v, page_table, l, bs, ss), kn, vb

        hp, narrow_p = rms_cast(yp, ln_g[l], w_t, l)
        hs, narrow_s = rms_cast(ys, ln_g[l], w_t, l)
        wide_p, wide_s = in_proj(hp, hs, w_t, l)
        yp, st_p = _mixer(yp, wide_p, narrow_p, lw, states_p, 0, bp, sp, attend_prompt)
        ys, st_s = _mixer(ys, wide_s, narrow_s, lw, states_s, l, bs, ss, attend_sample)
        for i in range(7):
            new_p[i].append(st_p[i])
            new_s[i].append(st_s[i])
    outs_p = [jnp.stack(t) for t in new_p]
    outs_s = [jnp.stack(t) for t in new_s]
    return (yp.reshape(bp, sp, d), ys.reshape(bs, ss, d), *outs_p, *outs_s)
```

```python
import functools

import jax
import jax.numpy as jnp
from jax import lax
from jax.experimental import pallas as pl
from jax.experimental.pallas import tpu as pltpu

F32 = jnp.float32
BF16 = jnp.bfloat16
HI = lax.Precision.HIGHEST
EPS = 1e-6
NEG_INF = float("-inf")

H_A, DK_A = 8, 128
H_B, HD_B = 8, 128
H_C, DK_C = 4, 256
CONV_W = 4
MOBA_BLOCK = 256
MOBA_TOPK = 3
PAGE_SIZE = 128
CHUNK = 64
W_BR = 1024
D_MODEL = 2048

LANES = 128
SUBLANES = 8
BF16_ROWS = 16
LOG2E = 1.4426950408889634
VMEM_LIMIT = 48 * 1024 * 1024

COL_QA, COL_KA, COL_VA, COL_ZA = 0, 1, 2, 3
COL_QB, COL_KB, COL_VB, COL_ZB = 4, 5, 6, 7
COL_QC, COL_KC, COL_VC, COL_ZC, COL_OC = 8, 9, 10, 11, 12
COL_GATE = 13
N_BIG = 19 * W_BR
OFF_SMALL_A = 4 * W_BR
OFF_SMALL_C = OFF_SMALL_A + 2 * H_A + 9 * W_BR
OFF_GATE = OFF_SMALL_C + 2 * H_C
WIN_A, WIN_C = OFF_SMALL_A // LANES, OFF_SMALL_C // LANES
LANE_BETA, LANE_ALPHA = 0, H_A
LANE_I = OFF_SMALL_C % LANES
LANE_F = LANE_I + H_C
SHIFT_A, SHIFT_BC, SHIFT_G = 0, 2 * H_A, 2 * H_A + 2 * H_C
FIRST_BC_BLOCK, FIRST_G_BLOCK = COL_QB, COL_GATE


def _params(*sem):
    return pltpu.CompilerParams(dimension_semantics=sem, vmem_limit_bytes=VMEM_LIMIT)


def _sigmoid(x):
    return 1.0 / (1.0 + jnp.exp(-x))


def _silu(x):
    return x * _sigmoid(x)


def _softplus(x):
    return jnp.maximum(x, 0.0) + jnp.log(1.0 + jnp.exp(-jnp.abs(x)))


def _dot(a, b, precision=None):
    return jnp.dot(a, b, precision=precision, preferred_element_type=F32)


def _dot_nt(a, b, precision=None):
    return lax.dot_general(a, b, (((1,), (1,)), ((), ())), precision=precision,
                           preferred_element_type=F32)


def _dot_tn(a, b, precision=None):
    return lax.dot_general(a, b, (((0,), (0,)), ((), ())), precision=precision,
                           preferred_element_type=F32)


def _cumsum_rows(x):
    rid = lax.broadcasted_iota(jnp.int32, x.shape, 0)
    d = 1
    while d < x.shape[0]:
        x = x + jnp.where(rid >= d, pltpu.roll(x, d, 0), 0.0)
        d *= 2
    return x


def _chunk_len(seq):
    return min(CHUNK, -(-seq // BF16_ROWS) * BF16_ROWS)


def _branch_dtype(block_rows):
    return BF16 if block_rows % BF16_ROWS == 0 else F32


def _pick_tile(n, candidates):
    for c in candidates:
        if n % c == 0:
            return c
    return n


def _rms_kernel(x_ref, g_ref, wa_ref, wc_ref, o_ref, ps_ref):
    x = x_ref[...]
    y = x * lax.rsqrt(jnp.mean(x * x, axis=-1, keepdims=True) + EPS)
    h = (y * g_ref[...]).astype(o_ref.dtype)
    o_ref[...] = h
    ps_ref[:, 0:LANES] = _dot_nt(h, wa_ref[0].astype(BF16))
    ps_ref[:, LANES:2 * LANES] = _dot_nt(h, wc_ref[0].astype(BF16))


def rms_cast(x, g, w_t, layer):
    t, d = x.shape
    tm = _pick_tile(t, (512, 256, 128))
    win = lambda c: pl.BlockSpec((1, LANES, d), lambda i, c=c: (layer, c, 0))
    return pl.pallas_call(
        _rms_kernel, grid=(t // tm,),
        in_specs=[pl.BlockSpec((tm, d), lambda i: (i, 0)), pl.BlockSpec((1, d), lambda i: (0, 0)),
                  win(WIN_A), win(WIN_C)],
        out_specs=[pl.BlockSpec((tm, d), lambda i: (i, 0)), pl.BlockSpec((tm, 2 * LANES), lambda i: (i, 0))],
        out_shape=[jax.ShapeDtypeStruct((t, d), BF16), jax.ShapeDtypeStruct((t, 2 * LANES), F32)],
        compiler_params=_params("arbitrary"), name="rms_cast")(x, g.reshape(1, d), w_t, w_t)


def _in_proj_kernel(a_ref, a2_ref, w_hbm, o_ref, o2_ref, wf_s, wb_s, sem, *, layer, tn):
    j = pl.program_id(0)
    n_j = pl.num_programs(0)

    def w_copy(jj, slot):
        shift = jnp.where(jj < FIRST_BC_BLOCK, SHIFT_A // SUBLANES,
                          jnp.where(jj < FIRST_G_BLOCK, SHIFT_BC // SUBLANES, SHIFT_G // SUBLANES))
        row0 = pl.multiple_of((jj * (tn // SUBLANES) + shift) * SUBLANES, SUBLANES)
        return pltpu.make_async_copy(w_hbm.at[layer, pl.ds(row0, tn), :], wf_s.at[slot], sem.at[slot])

    @pl.when(pl.program_id(1) == 0)
    def _first():
        slot = j % 2

        @pl.when(j == 0)
        def _prologue():
            w_copy(0, 0).start()

        @pl.when(j + 1 < n_j)
        def _prefetch():
            w_copy(j + 1, 1 - slot).start()

        w_copy(j, slot).wait()
        wb_s[...] = wf_s[slot].astype(BF16)
        o2_ref[...] = _dot_nt(a2_ref[...], wb_s[...])

    o_ref[...] = _dot_nt(a_ref[...], wb_s[...])


def in_proj(h, h2, w_t, layer):
    t, k = h.shape
    t2 = h2.shape[0]
    tm = _pick_tile(t, (1024, 768, 512, 256, 128))
    tn = W_BR
    n_i = t // tm
    row = lambda j, i: jnp.where(j % 2 == 0, i, n_i - 1 - i)
    return pl.pallas_call(
        functools.partial(_in_proj_kernel, layer=layer, tn=tn), grid=(N_BIG // tn, n_i),
        in_specs=[pl.BlockSpec((tm, k), lambda j, i: (row(j, i), 0)),
                  pl.BlockSpec((t2, k), lambda j, i: (0, 0)),
                  pl.BlockSpec(memory_space=pl.ANY)],
        out_specs=[pl.BlockSpec((tm, tn), lambda j, i: (row(j, i), j)),
                   pl.BlockSpec((t2, tn), lambda j, i: (0, j))],
        out_shape=[jax.ShapeDtypeStruct((t, N_BIG), F32), jax.ShapeDtypeStruct((t2, N_BIG), F32)],
        scratch_shapes=[pltpu.VMEM((2, tn, k), F32), pltpu.VMEM((tn, k), BF16), pltpu.SemaphoreType.DMA((2,))],
        compiler_params=_params("arbitrary", "arbitrary"), name="in_proj")(h, h2, w_t)


def _merge_kernel(oa_ref, ob_ref, oc_ref, zb_ref, ga_ref, gb_ref, gc_ref, wa_ref, wb_ref, wc_ref, o_ref):
    ob = ob_ref[...] * _silu(zb_ref[...])
    m = _sigmoid(ga_ref[...]) * _dot(oa_ref[...].astype(BF16), wa_ref[...])
    m += _sigmoid(gb_ref[...]) * _dot(ob.astype(BF16), wb_ref[...])
    m += _sigmoid(gc_ref[...]) * _dot(oc_ref[...].astype(BF16), wc_ref[...])
    o_ref[...] = m.astype(o_ref.dtype)


def merge(oa, ob, oc, p, wa, wb, wc):
    t = oa.shape[0]
    tm = _pick_tile(t, (512, 256, 128))
    tn = W_BR
    row = lambda j, i: (i, 0)
    return pl.pallas_call(
        _merge_kernel, grid=(D_MODEL // tn, t // tm),
        in_specs=[pl.BlockSpec((tm, W_BR), row), pl.BlockSpec((tm, W_BR), row), pl.BlockSpec((tm, W_BR), row),
                  pl.BlockSpec((tm, W_BR), lambda j, i: (i, COL_ZB)),
                  pl.BlockSpec((tm, tn), lambda j, i: (i, COL_GATE + j)),
                  pl.BlockSpec((tm, tn), lambda j, i: (i, COL_GATE + 2 + j)),
                  pl.BlockSpec((tm, tn), lambda j, i: (i, COL_GATE + 4 + j)),
                  pl.BlockSpec((W_BR, tn), lambda j, i: (0, j)),
                  pl.BlockSpec((W_BR, tn), lambda j, i: (0, j)),
                  pl.BlockSpec((W_BR, tn), lambda j, i: (0, j))],
        out_specs=pl.BlockSpec((tm, tn), lambda j, i: (i, j)),
        out_shape=jax.ShapeDtypeStruct((t, D_MODEL), BF16),
        compiler_params=_params("arbitrary", "arbitrary"), name="merge")(oa, ob, oc, p, p, p, p, wa, wb, wc)


def _outproj_kernel(x_ref, m_ref, w_ref, o_ref):
    o_ref[...] = x_ref[...] + _dot(m_ref[...], w_ref[...])


def out_proj(x, m, w):
    t, d = x.shape
    tm = _pick_tile(t, (512, 256, 128))
    tn = 1024
    return pl.pallas_call(
        _outproj_kernel, grid=(d // tn, t // tm),
        in_specs=[pl.BlockSpec((tm, tn), lambda j, i: (i, j)), pl.BlockSpec((tm, d), lambda j, i: (i, 0)),
                  pl.BlockSpec((d, tn), lambda j, i: (0, j))],
        out_specs=pl.BlockSpec((tm, tn), lambda j, i: (i, j)),
        out_shape=jax.ShapeDtypeStruct((t, d), F32),
        compiler_params=_params("arbitrary", "arbitrary"), name="out_proj")(x, m, w)


def _bdot(a, b):
    return _dot(a.astype(BF16), b.astype(BF16))


DELTA_CHUNKS_PER_STEP = 4


def _delta_kernel(q_ref, k_ref, v_ref, z_ref, sm_ref, cw_ref, cb_ref, al_ref, dtb_ref, s0_ref, ng_ref,
                  o_ref, cn_ref, sn_ref, xp_ref, s_ref, *, c_in, chunk):
    c = pl.program_id(1)
    C = chunk
    hd = DK_A

    @pl.when(c == 0)
    def _init():
        xp_ref[...] = jnp.zeros(xp_ref.shape, F32)
        xp_ref[0:SUBLANES, :] = cb_ref[0]
        s_ref[...] = s0_ref[0]

    xp_ref[SUBLANES:SUBLANES + c_in, 0:W_BR] = q_ref[...]
    xp_ref[SUBLANES:SUBLANES + c_in, W_BR:2 * W_BR] = k_ref[...]
    xp_ref[SUBLANES:SUBLANES + c_in, 2 * W_BR:3 * W_BR] = v_ref[...]

    nch = max(c_in // C, 1)
    padded = c_in < C
    valid = lax.broadcasted_iota(jnp.int32, (C, 1), 0) < c_in
    mask = (lambda x, fill: jnp.where(valid, x, fill)) if padded else (lambda x, fill: x)
    ri = lax.broadcasted_iota(jnp.int32, (C, C), 0)
    ci = lax.broadcasted_iota(jnp.int32, (C, C), 1)
    causal = ci <= ri
    strict = ci < ri

    beta_c, gcum_c, gcum_t_c, e_g_c, e_end_c, e_last_c = [], [], [], [], [], []
    for kc in range(nch):
        if padded:
            sm = jnp.concatenate([sm_ref[...], jnp.zeros((C - c_in, LANES), F32)], axis=0)
        else:
            sm = sm_ref[kc * C:(kc + 1) * C, :]
        g = mask(-jnp.exp(al_ref[...]) * _softplus(sm + dtb_ref[...]), 0.0)
        gcum = _cumsum_rows(g)
        g_last = gcum[C - 1:C, :]
        beta_c.append(_sigmoid(sm))
        gcum_c.append(gcum)
        gcum_t_c.append(gcum.T)
        e_g_c.append(jnp.exp(gcum))
        e_end_c.append(jnp.exp(g_last - gcum))
        e_last_c.append(jnp.exp(g_last))

    def conv(kc, col):
        sl = slice(col, col + hd)
        r0 = 5 + kc * C
        y = xp_ref[r0:r0 + C, sl] * cw_ref[0:1, sl]
        for j in range(1, CONV_W):
            y = y + xp_ref[r0 + j:r0 + j + C, sl] * cw_ref[j:j + 1, sl]
        return mask(_silu(y), 0.0)

    heads = range(H_A)
    pairs = [(kc, h) for kc in range(nch) for h in heads]
    npair = range(len(pairs))
    col = lambda vec, kc, lane: vec[kc][:, lane:lane + 1]
    q_l = [conv(kc, h * hd) for kc, h in pairs]
    k_l = [conv(kc, W_BR + h * hd) for kc, h in pairs]
    v_l = [conv(kc, 2 * W_BR + h * hd) for kc, h in pairs]
    q_l = [q * lax.rsqrt(jnp.sum(q * q, axis=-1, keepdims=True) + EPS) * (DK_A ** -0.5) for q in q_l]
    k_l = [k * lax.rsqrt(jnp.sum(k * k, axis=-1, keepdims=True) + EPS) for k in k_l]
    decay_l = [jnp.exp(jnp.where(causal, col(gcum_c, kc, LANE_ALPHA + h)
                                 - gcum_t_c[kc][LANE_ALPHA + h:LANE_ALPHA + h + 1, :], NEG_INF))
               for kc, h in pairs]
    b_l = [col(beta_c, kc, LANE_BETA + h) for kc, h in pairs]
    eg_l = [col(e_g_c, kc, LANE_ALPHA + h) for kc, h in pairs]
    kb_l = [k.astype(BF16) for k in k_l]
    qk_kk = [_dot_nt(jnp.concatenate([q_l[i].astype(BF16), kb_l[i]], axis=0), kb_l[i]) for i in npair]
    a_l = [jnp.where(strict, b_l[i] * decay_l[i] * qk_kk[i][C:2 * C], 0.0) for i in npair]
    r_l = [-a for a in a_l]
    pw_l = a_l
    n = 2
    while n < C:
        pw_l = [_bdot(pw, pw) for pw in pw_l]
        r_l = [r_l[i] + pw_l[i] + _bdot(r_l[i], pw_l[i]) for i in npair]
        n *= 2
    rhs_l = [jnp.concatenate([b_l[i] * v_l[i], (b_l[i] * eg_l[i]) * k_l[i]], axis=1) for i in npair]
    sol_l = [rhs_l[i] + _bdot(r_l[i], rhs_l[i]) for i in npair]
    kq_l = [jnp.concatenate([sol_l[i][:, hd:2 * hd], q_l[i] * eg_l[i]], axis=0).astype(BF16) for i in npair]
    p_l = [(decay_l[i] * qk_kk[i][0:C]).astype(BF16) for i in npair]
    kend_l = [(k_l[i] * col(e_end_c, kc, LANE_ALPHA + h)).astype(BF16) for i, (kc, h) in enumerate(pairs)]

    s_l = [s_ref[h] for h in heads]
    o_l = []
    for kc in range(nch):
        idx = [kc * H_A + h for h in heads]
        kq_s = [_dot(kq_l[i], s_l[h].astype(BF16)) for h, i in enumerate(idx)]
        wb_l = [(sol_l[i][:, 0:hd] - kq_s[h][0:C]).astype(BF16) for h, i in enumerate(idx)]
        o_l += [kq_s[h][C:2 * C] + _dot(p_l[i], wb_l[h]) for h, i in enumerate(idx)]
        s_l = [col(e_last_c, kc, LANE_ALPHA + h) * s_l[h] + _dot_tn(kend_l[i], wb_l[h])
               for h, i in enumerate(idx)]
    for h in heads:
        s_ref[h] = s_l[h]
    for i, (kc, h) in enumerate(pairs):
        o = o_l[i]
        o = o * lax.rsqrt(jnp.mean(o * o, axis=-1, keepdims=True) + EPS) * ng_ref[...]
        rows = slice(0, c_in) if padded else slice(kc * C, (kc + 1) * C)
        o = o[0:min(c_in, C)] * _silu(z_ref[rows, h * hd:(h + 1) * hd])
        o_ref[rows, h * hd:(h + 1) * hd] = o.astype(o_ref.dtype)

    tail = xp_ref[c_in:c_in + SUBLANES, :]
    xp_ref[0:SUBLANES, :] = tail

    @pl.when(c == pl.num_programs(1) - 1)
    def _fin():
        cn_ref[0] = tail
        sn_ref[0] = s_ref[...]


def delta_branch(p, ps, conv_w, conv_buf, a_log, dt_bias, s0, st, norm_g, bsz, seq):
    chunk = _chunk_len(seq)
    c_in = min(seq, DELTA_CHUNKS_PER_STEP * chunk)
    assert seq % c_in == 0 and (c_in % chunk == 0 or c_in < chunk)
    n_chunks = seq // c_in
    lane_vec = lambda v: jnp.zeros((1, LANES), F32).at[0, LANE_ALPHA:LANE_ALPHA + H_A].set(v)
    rb = lambda col: pl.BlockSpec((c_in, W_BR), lambda b, c, col=col: (b * n_chunks + c, col))
    const2 = lambda b, c: (0, 0)
    o, cn, sn = pl.pallas_call(
        functools.partial(_delta_kernel, c_in=c_in, chunk=chunk), grid=(bsz, n_chunks),
        in_specs=[rb(COL_QA), rb(COL_KA), rb(COL_VA), rb(COL_ZA),
                  pl.BlockSpec((c_in, LANES), lambda b, c: (b * n_chunks + c, 0)),
                  pl.BlockSpec((CONV_W, 3 * W_BR), const2),
                  pl.BlockSpec((None, 1, SUBLANES, 3 * W_BR), lambda b, c: (st, b, 0, 0)),
                  pl.BlockSpec((1, LANES), const2), pl.BlockSpec((1, LANES), const2),
                  pl.BlockSpec((None, 1, H_A, DK_A, DK_A), lambda b, c: (st, b, 0, 0, 0)),
                  pl.BlockSpec((1, DK_A), const2)],
        out_specs=[pl.BlockSpec((c_in, W_BR), lambda b, c: (b * n_chunks + c, 0)),
                   pl.BlockSpec((1, SUBLANES, 3 * W_BR), lambda b, c: (b, 0, 0)),
                   pl.BlockSpec((1, H_A, DK_A, DK_A), lambda b, c: (b, 0, 0, 0))],
        out_shape=[jax.ShapeDtypeStruct((bsz * seq, W_BR), _branch_dtype(c_in)),
                   jax.ShapeDtypeStruct((bsz, SUBLANES, 3 * W_BR), F32),
                   jax.ShapeDtypeStruct((bsz, H_A, DK_A, DK_A), F32)],
        scratch_shapes=[pltpu.VMEM((SUBLANES + max(c_in, chunk), 3 * W_BR), F32),
                        pltpu.VMEM((H_A, DK_A, DK_A), F32)],
        compiler_params=_params("arbitrary", "arbitrary"), name="delta")(
            p, p, p, p, ps, conv_w, conv_buf, lane_vec(a_log), lane_vec(dt_bias), s0, norm_g.reshape(1, DK_A))
    return o, cn[:, SUBLANES - (CONV_W - 1):], sn


MLSTM_CHUNKS_PER_STEP = 4


def _mlstm_kernel(q_ref, k_ref, v_ref, z_ref, og_ref, sm_ref, bif_ref, c0_ref, n0_ref, m0_ref, ng_ref,
                  o_ref, cn_ref, nn_ref, mn_ref, c_s, n_s, m_s, *, c_in, chunk):
    c = pl.program_id(1)
    C = chunk
    hd = DK_C

    @pl.when(c == 0)
    def _init():
        c_s[...] = c0_ref[0]
        n_s[...] = n0_ref[0]
        m_s[...] = m0_ref[0]

    nch = max(c_in // C, 1)
    padded = c_in < C

    def rows_of(ref, kc, sl):
        if padded:
            return jnp.concatenate([ref[:, sl], jnp.zeros((C - c_in, sl.stop - sl.start), F32)], axis=0)
        return ref[kc * C:(kc + 1) * C, sl]

    valid = lax.broadcasted_iota(jnp.int32, (C, 1), 0) < c_in
    ri = lax.broadcasted_iota(jnp.int32, (C, C), 0)
    ci = lax.broadcasted_iota(jnp.int32, (C, C), 1)
    causal = ci <= ri
    m_old = m_s[...]
    lane = lax.broadcasted_iota(jnp.int32, m_old.shape, 1)
    heads = range(H_C)
    hsl = [slice(h * hd, (h + 1) * hd) for h in heads]
    pairs = [(kc, h) for kc in range(nch) for h in heads]

    i_c, i_t_c, b_c, b_t_c = [], [], [], []
    for kc in range(nch):
        pre = rows_of(sm_ref, kc, slice(0, LANES)) + bif_ref[...]
        i_pre = jnp.where(valid, pre, NEG_INF) if padded else pre
        log_f = -_softplus(-pre)
        if padded:
            log_f = jnp.where(valid, log_f, 0.0)
        bcum = _cumsum_rows(log_f)
        i_c.append(i_pre)
        i_t_c.append(i_pre.T)
        b_c.append(bcum)
        b_t_c.append(bcum.T)

    q_l = [rows_of(q_ref, kc, hsl[h]) for kc, h in pairs]
    ks_l = [rows_of(k_ref, kc, hsl[h]) * (DK_C ** -0.5) for kc, h in pairs]
    vb_l = [rows_of(v_ref, kc, hsl[h]).astype(BF16) for kc, h in pairs]
    qb_l = [q.astype(BF16) for q in q_l]
    qk_l = [_dot_nt(qb_l[i], ks_l[i].astype(BF16)) for i in range(len(pairs))]
    bcol_l = [b_c[kc][:, LANE_F + h:LANE_F + h + 1] for kc, h in pairs]
    icol_l = [i_c[kc][:, LANE_I + h:LANE_I + h + 1] for kc, h in pairs]
    intra_l = [jnp.where(causal, bcol_l[i] - b_t_c[kc][LANE_F + h:LANE_F + h + 1, :]
                         + i_t_c[kc][LANE_I + h:LANE_I + h + 1, :], NEG_INF) for i, (kc, h) in enumerate(pairs)]
    imax_l = [jnp.max(x, axis=-1, keepdims=True) for x in intra_l]
    m_prev = [m_old[:, h:h + 1] for h in heads]
    mt_l, winter_l, wc_l, wj_l = [], [], [], []
    for i, (kc, h) in enumerate(pairs):
        inter = bcol_l[i] + m_prev[h]
        m_t = jnp.maximum(inter, imax_l[i])
        m_new = m_t[C - 1:C, :]
        b_last = bcol_l[i][C - 1:C, :]
        mt_l.append(m_t)
        winter_l.append(jnp.exp(inter - m_t))
        wc_l.append(jnp.exp(b_last + m_prev[h] - m_new))
        wj_l.append(jnp.exp(b_last - bcol_l[i] + icol_l[i] - m_new))
        m_prev[h] = m_new
    m_new_vec = m_old
    for h in heads:
        m_new_vec = jnp.where(lane == h, m_prev[h], m_new_vec)
    m_s[...] = m_new_vec

    s_l = [qk_l[i] * jnp.exp(intra_l[i] - mt_l[i]) for i in range(len(pairs))]
    sv_l = [_dot(s_l[i].astype(BF16), vb_l[i]) for i in range(len(pairs))]
    ssum_l = [jnp.sum(s, axis=-1, keepdims=True) for s in s_l]
    kw_l = [wj_l[i] * ks_l[i] for i in range(len(pairs))]
    upd_l = [_dot_tn(kw_l[i].astype(BF16), vb_l[i]) for i in range(len(pairs))]
    nsum_l = [jnp.sum(kw, axis=0, keepdims=True) for kw in kw_l]

    cm_l = [c_s[h] for h in heads]
    n_l = [n_s[h:h + 1, :] for h in heads]
    for i, (kc, h) in enumerate(pairs):
        num = winter_l[i] * _dot(qb_l[i], cm_l[h].astype(BF16)) + sv_l[i]
        den = winter_l[i] * jnp.sum(q_l[i] * n_l[h], axis=-1, keepdims=True) + ssum_l[i]
        hh = num / jnp.maximum(jnp.abs(den), jnp.exp(-mt_l[i]))
        cm_l[h] = wc_l[i] * cm_l[h] + upd_l[i]
        n_l[h] = wc_l[i] * n_l[h] + nsum_l[i]
        rows = slice(0, c_in) if padded else slice(kc * C, (kc + 1) * C)
        hc = _sigmoid(og_ref[rows, hsl[h]]) * hh[0:min(c_in, C)]
        hc = hc * lax.rsqrt(jnp.mean(hc * hc, axis=-1, keepdims=True) + EPS) * ng_ref[...]
        o_ref[rows, hsl[h]] = (hc * _silu(z_ref[rows, hsl[h]])).astype(o_ref.dtype)
    for h in heads:
        c_s[h] = cm_l[h]
        n_s[h:h + 1, :] = n_l[h]

    @pl.when(c == pl.num_programs(1) - 1)
    def _fin():
        cn_ref[0] = c_s[...]
        nn_ref[0] = n_s[...]
        mn_ref[0] = m_new_vec


def mlstm_branch(p, ps, b_if, c0, n0, m0p, st, norm_g, bsz, seq):
    chunk = _chunk_len(seq)
    c_in = min(seq, MLSTM_CHUNKS_PER_STEP * chunk)
    assert seq % c_in == 0 and (c_in % chunk == 0 or c_in < chunk)
    n_chunks = seq // c_in
    bif = jnp.zeros((1, LANES), F32).at[0, LANE_I:LANE_I + 2 * H_C].set(b_if)
    rb = lambda col: pl.BlockSpec((c_in, W_BR), lambda b, c, col=col: (b * n_chunks + c, col))
    const2 = lambda b, c: (0, 0)
    st4 = pl.BlockSpec((1, H_C, DK_C, DK_C), lambda b, c: (b, 0, 0, 0))
    st3 = pl.BlockSpec((1, H_C, DK_C), lambda b, c: (b, 0, 0))
    stm = pl.BlockSpec((1, 1, LANES), lambda b, c: (b, 0, 0))
    in4 = pl.BlockSpec((None, 1, H_C, DK_C, DK_C), lambda b, c: (st, b, 0, 0, 0))
    in3 = pl.BlockSpec((None, 1, H_C, DK_C), lambda b, c: (st, b, 0, 0))
    inm = pl.BlockSpec((None, 1, 1, LANES), lambda b, c: (st, b, 0, 0))
    o, cn, nn, mn = pl.pallas_call(
        functools.partial(_mlstm_kernel, c_in=c_in, chunk=chunk), grid=(bsz, n_chunks),
        in_specs=[rb(COL_QC), rb(COL_KC), rb(COL_VC), rb(COL_ZC), rb(COL_OC),
                  pl.BlockSpec((c_in, LANES), lambda b, c: (b * n_chunks + c, 1)),
                  pl.BlockSpec((1, LANES), const2), in4, in3, inm, pl.BlockSpec((1, DK_C), const2)],
        out_specs=[pl.BlockSpec((c_in, W_BR), lambda b, c: (b * n_chunks + c, 0)), st4, st3, stm],
        out_shape=[jax.ShapeDtypeStruct((bsz * seq, W_BR), _branch_dtype(c_in)),
                   jax.ShapeDtypeStruct((bsz, H_C, DK_C, DK_C), F32),
                   jax.ShapeDtypeStruct((bsz, H_C, DK_C), F32),
                   jax.ShapeDtypeStruct((bsz, 1, LANES), F32)],
        scratch_shapes=[pltpu.VMEM((H_C, DK_C, DK_C), F32), pltpu.VMEM((H_C, DK_C), F32),
                        pltpu.VMEM((1, LANES), F32)],
        compiler_params=_params("arbitrary", "arbitrary"), name="mlstm")(
            p, p, p, p, p, ps, bif, c0, n0, m0p, norm_g.reshape(1, DK_C))
    return o, cn, nn, mn[:, 0, :H_C]


def _head_rms(x, g):
    return x * lax.rsqrt(jnp.mean(x * x, axis=-1, keepdims=True) + EPS) * g


def _qknorm_kernel(q_ref, k_ref, qg_ref, kg_ref, qn_ref, kn_ref):
    for h in range(H_B):
        sl = slice(h * HD_B, (h + 1) * HD_B)
        qn_ref[:, sl] = _head_rms(q_ref[:, sl], qg_ref[...]) * (HD_B ** -0.5)
        kn_ref[:, sl] = _head_rms(k_ref[:, sl], kg_ref[...])


def _qknorm_kv_kernel(q_ref, k_ref, v_ref, qg_ref, kg_ref, qn_ref, kn_ref, vo_ref, knb_ref, vt_ref, kbar_ref):
    vo_ref[...] = v_ref[...]
    for h in range(H_B):
        sl = slice(h * HD_B, (h + 1) * HD_B)
        qn_ref[:, sl] = _head_rms(q_ref[:, sl], qg_ref[...]) * (HD_B ** -0.5)
        kn = _head_rms(k_ref[:, sl], kg_ref[...])
        kn_ref[:, sl] = kn
        knb_ref[0, :, sl] = kn.astype(BF16)
        kbar_ref[0, :, sl] = jnp.mean(kn, axis=0, keepdims=True)
    eye = (lax.broadcasted_iota(jnp.int32, (HD_B, HD_B), 0)
           == lax.broadcasted_iota(jnp.int32, (HD_B, HD_B), 1)).astype(BF16)
    for h in range(H_B):
        sl = slice(h * HD_B, (h + 1) * HD_B)
        vt_ref[0, sl, :] = _dot_nt(eye, v_ref[:, sl].astype(BF16)).astype(BF16)


def qk_norm(p, qg, kg, with_kv):
    t = p.shape[0]
    qg = qg.reshape(1, HD_B)
    kg = kg.reshape(1, HD_B)
    gs = pl.BlockSpec((1, HD_B), lambda i: (0, 0))
    if not with_kv:
        tm = _pick_tile(t, (256, 128))
        return pl.pallas_call(
            _qknorm_kernel, grid=(t // tm,),
            in_specs=[pl.BlockSpec((tm, W_BR), lambda i: (i, COL_QB)),
                      pl.BlockSpec((tm, W_BR), lambda i: (i, COL_KB)), gs, gs],
            out_specs=[pl.BlockSpec((tm, W_BR), lambda i: (i, 0))] * 2,
            out_shape=[jax.ShapeDtypeStruct((t, W_BR), F32)] * 2,
            compiler_params=_params("arbitrary"), name="qk_norm")(p, p, qg, kg)
    tm = MOBA_BLOCK
    nb = t // tm
    return pl.pallas_call(
        _qknorm_kv_kernel, grid=(nb,),
        in_specs=[pl.BlockSpec((tm, W_BR), lambda i: (i, COL_QB)),
                  pl.BlockSpec((tm, W_BR), lambda i: (i, COL_KB)),
                  pl.BlockSpec((tm, W_BR), lambda i: (i, COL_VB)), gs, gs],
        out_specs=[pl.BlockSpec((tm, W_BR), lambda i: (i, 0))] * 3
                  + [pl.BlockSpec((1, tm, W_BR), lambda i: (i, 0, 0)),
                     pl.BlockSpec((1, W_BR, tm), lambda i: (i, 0, 0)),
                     pl.BlockSpec((1, 1, W_BR), lambda i: (i, 0, 0))],
        out_shape=[jax.ShapeDtypeStruct((t, W_BR), F32)] * 3
                  + [jax.ShapeDtypeStruct((nb, tm, W_BR), BF16), jax.ShapeDtypeStruct((nb, W_BR, tm), BF16),
                   jax.ShapeDtypeStruct((nb, 1, W_BR), F32)],
        compiler_params=_params("arbitrary"), name="qk_norm_kv")(p, p, p, qg, kg)


def _select_topk(sc, valid, axis):
    nb = sc.shape[axis]
    idx = lax.broadcasted_iota(jnp.int32, sc.shape, axis).astype(F32)
    if valid is not None:
        sc = jnp.where(valid, sc, NEG_INF)
    sel = jnp.zeros(sc.shape, F32)
    for _ in range(MOBA_TOPK):
        mx = jnp.max(sc, axis=axis, keepdims=True)
        first = jnp.min(jnp.where(sc == mx, idx, float(nb)), axis=axis, keepdims=True)
        hit = idx == first
        sel = jnp.where(hit & (mx > NEG_INF), 1.0, sel)
        sc = jnp.where(hit, NEG_INF, sc)
    return sel


MOBA_HEADS_PER_STEP = 8


def _moba_prompt_kernel(q_ref, k_ref, vt_ref, kbar_ref, o_ref, sel_ref, qb_ref, acc_ref):
    i = pl.program_id(2)
    blk = MOBA_BLOCK
    hp = MOBA_HEADS_PER_STEP
    nb = kbar_ref.shape[1]
    past = lax.broadcasted_iota(jnp.int32, (nb, blk), 0) < i
    kpos = lax.broadcasted_iota(jnp.int32, (blk, blk), 0)
    qpos = lax.broadcasted_iota(jnp.int32, (blk, blk), 1)
    heads = [slice(hh * HD_B, (hh + 1) * HD_B) for hh in range(hp)]

    ones = jnp.ones((BF16_ROWS, blk), BF16)

    def pv(n, sl, pr):
        return _dot(jnp.concatenate([vt_ref[n, sl, :], ones], axis=0), pr)

    for hh, sl in enumerate(heads):
        qb_ref[hh] = (q_ref[:, sl] * LOG2E).astype(BF16)
    s_l = [_dot_nt(k_ref[i, :, sl], qb_ref[hh]) for hh, sl in enumerate(heads)]
    sc_l = [_dot_nt(kbar_ref[0, :, sl], q_ref[:, sl], HI) for sl in heads]
    ms, pr_l = [], []
    for hh in range(hp):
        s = jnp.where(kpos <= qpos, s_l[hh], NEG_INF)
        m = jnp.max(s, axis=0, keepdims=True)
        ms.append(m)
        pr_l.append(jnp.exp2(s - m).astype(BF16))
    for hh, sl in enumerate(heads):
        acc_ref[hh] = pv(i, sl, pr_l[hh])
        sel_ref[hh] = _select_topk(sc_l[hh], past, 0)

    def past_blocks(ns, ms):
        s_l = [[_dot_nt(k_ref[n, :, sl], qb_ref[hh]) for n in ns] for hh, sl in enumerate(heads)]
        ones_n = jnp.ones((BF16_ROWS, len(ns) * blk), BF16)
        ms_new, pr_l, alpha_l = [], [], []
        for hh in range(hp):
            ss = [jnp.where(sel_ref[hh, pl.ds(n, 1), :] > 0.0, s, NEG_INF) for n, s in zip(ns, s_l[hh])]
            m_new = ms[hh]
            for s in ss:
                m_new = jnp.maximum(m_new, jnp.max(s, axis=0, keepdims=True))
            alpha_l.append(jnp.exp2(ms[hh] - m_new))
            ms_new.append(m_new)
            pr_l.append(jnp.concatenate([jnp.exp2(s - m_new).astype(BF16) for s in ss], axis=0))
        for hh, sl in enumerate(heads):
            lhs = jnp.concatenate([jnp.concatenate([vt_ref[n, sl, :] for n in ns], axis=1), ones_n], axis=0)
            acc_ref[hh] = alpha_l[hh] * acc_ref[hh] + _dot(lhs, pr_l[hh])
        return tuple(ms_new)

    odd = i % 2
    ms = lax.cond(odd == 1, lambda ms: past_blocks([0], ms), lambda ms: ms, tuple(ms))
    lax.fori_loop(0, i // 2, lambda t, ms: past_blocks([2 * t + odd, 2 * t + odd + 1], ms), ms)
    for hh, sl in enumerate(heads):
        o_ref[:, sl] = (acc_ref[hh, 0:HD_B, :] / acc_ref[hh, HD_B:HD_B + 1, :]).T.astype(o_ref.dtype)


def moba_prompt(qn, knb, vt, kbar, bsz, seq):
    nb = seq // MOBA_BLOCK
    blk = MOBA_BLOCK
    hp = MOBA_HEADS_PER_STEP
    wid = hp * HD_B
    return pl.pallas_call(
        _moba_prompt_kernel, grid=(bsz, H_B // hp, nb),
        in_specs=[pl.BlockSpec((blk, wid), lambda b, h, i: (b * nb + i, h)),
                  pl.BlockSpec((nb, blk, wid), lambda b, h, i: (b, 0, h)),
                  pl.BlockSpec((nb, wid, blk), lambda b, h, i: (b, h, 0)),
                  pl.BlockSpec((1, nb, wid), lambda b, h, i: (b, 0, h))],
        out_specs=pl.BlockSpec((blk, wid), lambda b, h, i: (b * nb + i, h)),
        out_shape=jax.ShapeDtypeStruct((bsz * seq, W_BR), _branch_dtype(blk)),
        scratch_shapes=[pltpu.VMEM((hp, nb, blk), F32), pltpu.VMEM((hp, blk, HD_B), BF16),
                        pltpu.VMEM((hp, HD_B + BF16_ROWS, blk), F32)],
        compiler_params=_params("arbitrary", "arbitrary", "arbitrary"), name="moba_prompt")(qn, knb, vt, kbar)


DECODE_PAGES_PER_STEP = 16


def _moba_decode_kernel(pt_ref, q_ref, qf_ref, bias_ref, obias_ref, kown_ref, vown_ref, *rest, pps, seq):
    del pt_ref
    k_refs = rest[:pps]
    v_refs = rest[pps:2 * pps]
    o_ref = rest[2 * pps]
    m_s, l_s, o_s, ks_s = rest[2 * pps + 1:]
    n = pl.program_id(1)
    q = q_ref[0]
    ppb = MOBA_BLOCK // PAGE_SIZE
    npg = ks_s.shape[0] // H_B
    lane = lax.broadcasted_iota(jnp.int32, m_s.shape, 1)

    def partial_softmax(s):
        m = jnp.max(s, axis=-1, keepdims=True)
        pr = jnp.exp(s - m)
        return m, jnp.sum(pr, axis=-1, keepdims=True), pr

    @pl.when(n == 0)
    def _init():
        m_s[...] = jnp.zeros(m_s.shape, F32)
        l_s[...] = jnp.zeros(l_s.shape, F32)

    m_all = m_s[...]
    l_all = l_s[...]
    s_l = [_dot_nt(q, k_refs[j][0, 0].astype(BF16)) for j in range(pps)]
    pr_l = []
    for j in range(pps):
        m, l, pr = partial_softmax(s_l[j] + bias_ref[...])
        pg = n * pps + j
        m_all = jnp.where(lane == pg, m, m_all)
        l_all = jnp.where(lane == pg, l, l_all)
        pr_l.append(pr.astype(BF16))
    for j in range(pps):
        pg = n * pps + j
        o_s[pg] = _dot(pr_l[j], v_refs[j][0, 0].astype(BF16))
        ks_s[pl.ds(pg * H_B, H_B), :] = jnp.sum(k_refs[j][0, 0].reshape(PAGE_SIZE, H_B, HD_B), axis=0)
    m_s[...] = m_all
    l_s[...] = l_all

    @pl.when(n == pl.num_programs(1) - 1)
    def _combine():
        nb = npg // ppb
        m_o, l_o, pr_o = partial_softmax(_dot_nt(q, kown_ref[0].astype(BF16)) + obias_ref[...])
        o_s[npg] = _dot(pr_o.astype(BF16), vown_ref[0].astype(BF16))
        m_sl = jnp.where(lane == npg, m_o, m_all)
        l_sl = jnp.where(lane == npg, l_o, l_all)
        scp = _dot_nt(qf_ref[0], ks_s[...], HI)
        ch = lax.broadcasted_iota(jnp.int32, scp.shape, 1) % H_B
        rh = lax.broadcasted_iota(jnp.int32, scp.shape, 0) // seq
        scp = jnp.where(ch == rh, scp, 0.0)
        cols_per_blk = ppb * H_B
        gather = (lax.broadcasted_iota(jnp.int32, (npg * H_B, nb), 0) // cols_per_blk
                  == lax.broadcasted_iota(jnp.int32, (npg * H_B, nb), 1)).astype(F32)
        scb = _dot(scp, gather, HI) * (1.0 / MOBA_BLOCK)
        sel = _select_topk(scb, None, 1)
        slot = lax.broadcasted_iota(jnp.int32, (nb, m_s.shape[1]), 1)
        expand = (slot // ppb == lax.broadcasted_iota(jnp.int32, (nb, m_s.shape[1]), 0)) & (slot < npg)
        sel_slot = jnp.where(lane == npg, 1.0, _dot(sel.astype(BF16), expand.astype(BF16)))
        m_sel = jnp.where(sel_slot > 0.0, m_sl, NEG_INF)
        w = sel_slot * jnp.exp(m_sel - jnp.max(m_sel, axis=-1, keepdims=True))
        w = w / jnp.sum(w * l_sl, axis=-1, keepdims=True)
        acc = w[:, 0:1] * o_s[0]
        for pg in range(1, npg + 1):
            acc = acc + w[:, pg:pg + 1] * o_s[pg]
        o_ref[0] = acc


def moba_decode(qn, kn, v, cache_k, cache_v, page_table, layer, bsz, seq):
    ppb = MOBA_BLOCK // PAGE_SIZE
    pps = DECODE_PAGES_PER_STEP
    n_pages = page_table.shape[1]
    nrow = H_B * seq
    assert nrow % SUBLANES == 0 and n_pages // ppb >= MOBA_TOPK and n_pages % ppb == 0
    assert n_pages % pps == 0 and n_pages < LANES
    rows = PAGE_SIZE * H_B
    depth, n_phys = cache_k.shape[:2]
    ck = cache_k.reshape(depth, n_phys, rows, HD_B)
    cv = cache_v.reshape(depth, n_phys, rows, HD_B)
    qc = jnp.transpose(qn.reshape(bsz, seq, H_B, HD_B), (0, 2, 1, 3)).reshape(bsz, nrow, HD_B)
    c_h, c_q = jnp.arange(nrow) // seq, jnp.arange(nrow) % seq
    key_h = jnp.arange(rows) % H_B
    bias = jnp.where(c_h[:, None] == key_h[None, :], 0.0, NEG_INF).astype(F32)
    own = jnp.arange(seq * H_B)
    obias = jnp.where((c_h[:, None] == own[None, :] % H_B) & (own[None, :] // H_B <= c_q[:, None]),
                      0.0, NEG_INF).astype(F32)
    kown = kn.reshape(bsz, seq * H_B, HD_B)
    vown = v.reshape(bsz, seq * H_B, HD_B)

    page = lambda j: pl.BlockSpec((1, 1, rows, HD_B), lambda b, n, pt, j=j: (layer, pt[b, n * pps + j], 0, 0))
    seq3 = lambda shape: pl.BlockSpec((1,) + shape, lambda b, n, pt: (b, 0, 0))
    const2 = lambda shape: pl.BlockSpec(shape, lambda b, n, pt: (0, 0))
    grid_spec = pltpu.PrefetchScalarGridSpec(
        num_scalar_prefetch=1, grid=(bsz, n_pages // pps),
        in_specs=[seq3((nrow, HD_B)), seq3((nrow, HD_B)), const2((nrow, rows)), const2((nrow, seq * H_B)),
                  seq3((seq * H_B, HD_B)), seq3((seq * H_B, HD_B))]
                 + [page(j) for j in range(pps)] + [page(j) for j in range(pps)],
        out_specs=seq3((nrow, HD_B)),
        scratch_shapes=[pltpu.VMEM((nrow, LANES), F32), pltpu.VMEM((nrow, LANES), F32),
                        pltpu.VMEM((n_pages + 1, nrow, HD_B), F32), pltpu.VMEM((n_pages * H_B, HD_B), F32)])
    out = pl.pallas_call(
        functools.partial(_moba_decode_kernel, pps=pps, seq=seq), grid_spec=grid_spec,
        out_shape=jax.ShapeDtypeStruct((bsz, nrow, HD_B), F32),
        compiler_params=_params("arbitrary", "arbitrary"), name="moba_decode")(
            page_table, qc.astype(BF16), qc, bias, obias, kown, vown, *([ck] * pps), *([cv] * pps))
    return jnp.transpose(out.reshape(bsz, H_B, seq, HD_B), (0, 2, 1, 3)).reshape(bsz * seq, W_BR)


def _mixer(x, p, ps, lw, states, st, bsz, seq, attend):
    (conv_a, a_log, dt_bias, norm_a, qnorm_b, knorm_b, b_if, norm_c, wa, wb, wc, wo) = lw
    conv_buf, s_a, c_c, n_c, m_c = states
    oa, conv_new, s_new = delta_branch(p, ps, conv_a, conv_buf, a_log, dt_bias, s_a, st, norm_a, bsz, seq)
    ob, kn, vb = attend(p, qnorm_b, knorm_b)
    oc, c_new, n_new, m_new = mlstm_branch(p, ps, b_if, c_c, n_c, m_c, st, norm_c, bsz, seq)
    y = out_proj(x, merge(oa, ob, oc, p, wa, wb, wc), wo)
    kv_shape = (bsz, seq, H_B, HD_B)
    return y, (kn.reshape(kv_shape), vb.reshape(kv_shape), conv_new, s_new, c_new, n_new, m_new)


def _pad_states(conv, delta, c, n, m):
    conv = jnp.pad(conv, ((0, 0), (0, 0), (SUBLANES - (CONV_W - 1), 0), (0, 0)))
    m = jnp.pad(m, ((0, 0), (0, 0), (0, LANES - H_C)))[:, :, None, :]
    return conv, delta, c, n, m


def kernel(x_prompt, x_sample, cache_k, cache_v, page_table, state_conv_a, state_delta_a, state_mlstm_c,
           state_mlstm_n, state_mlstm_m, ln_g, w_in, conv_a, a_log, dt_bias, norm_a, qnorm_b, knorm_b,
           b_if, norm_c, w_br_a, w_br_b, w_br_c, w_out):
    bp, sp, d = x_prompt.shape
    bs, ss, _ = x_sample.shape
    depth = w_in.shape[0]
    yp = x_prompt.reshape(bp * sp, d)
    ys = x_sample.reshape(bs * ss, d)
    new_p = [[] for _ in range(7)]
    new_s = [[] for _ in range(7)]
    w_t = jnp.swapaxes(w_in, 1, 2)
    zeros = lambda *s: jnp.zeros(s, F32)
    states_p = _pad_states(zeros(1, bp, CONV_W - 1, 3 * W_BR), zeros(1, bp, H_A, DK_A, DK_A),
                           zeros(1, bp, H_C, DK_C, DK_C), zeros(1, bp, H_C, DK_C), zeros(1, bp, H_C))
    states_s = _pad_states(state_conv_a, state_delta_a, state_mlstm_c, state_mlstm_n, state_mlstm_m)
    for l in range(depth):
        lw = (conv_a[l], a_log[l], dt_bias[l], norm_a[l], qnorm_b[l],
              knorm_b[l], b_if[l], norm_c[l], w_br_a[l].astype(BF16), w_br_b[l].astype(BF16),
              w_br_c[l].astype(BF16), w_out[l].astype(BF16))

        def attend_prompt(p, qg, kg):
            qn, kn, vb, knb, vt, kbar = qk_norm(p, qg, kg, with_kv=True)
            ob = moba_prompt(qn, knb, vt, kbar.reshape(bp, sp // MOBA_BLOCK, W_BR), bp, sp)
            return ob, kn, vb

        def attend_sample(p, qg, kg, l=l):
            qn, kn = qk_norm(p, qg, kg, with_kv=False)
            vb = p[:, COL_VB * W_BR:(COL_VB + 1) * W_BR]
            return moba_decode(qn, kn, vb, cache_k, cache_v, page_table, l, bs, ss), kn, vb

        hp, narrow_p = rms_cast(yp, ln_g[l], w_t, l)
        hs, narrow_s = rms_cast(ys, ln_g[l], w_t, l)
        wide_p, wide_s = in_proj(hp, hs, w_t, l)
        yp, st_p = _mixer(yp, wide_p, narrow_p, lw, states_p, 0, bp, sp, attend_prompt)
        ys, st_s = _mixer(ys, wide_s, narrow_s, lw, states_s, l, bs, ss, attend_sample)
        for i in range(7):
            new_p[i].append(st_p[i])
            new_s[i].append(st_s[i])
    outs_p = [jnp.stack(t) for t in new_p]
    outs_s = [jnp.stack(t) for t in new_s]
    return (yp.reshape(bp, sp, d), ys.reshape(bs, ss, d), *outs_p, *outs_s)
```

```python
import functools

import jax
import jax.numpy as jnp
from jax import lax
from jax.experimental import pallas as pl
from jax.experimental.pallas import tpu as pltpu

F32 = jnp.float32
BF16 = jnp.bfloat16
HI = lax.Precision.HIGHEST
EPS = 1e-6
NEG_INF = float("-inf")

H_A, DK_A = 8, 128
H_B, HD_B = 8, 128
H_C, DK_C = 4, 256
CONV_W = 4
MOBA_BLOCK = 256
MOBA_TOPK = 3
PAGE_SIZE = 128
CHUNK = 64
W_BR = 1024
D_MODEL = 2048

LANES = 128
SUBLANES = 8
BF16_ROWS = 16
LOG2E = 1.4426950408889634
VMEM_LIMIT = 48 * 1024 * 1024

COL_QA, COL_KA, COL_VA, COL_ZA = 0, 1, 2, 3
COL_QB, COL_KB, COL_VB, COL_ZB = 4, 5, 6, 7
COL_QC, COL_KC, COL_VC, COL_ZC, COL_OC = 8, 9, 10, 11, 12
COL_GATE = 13
N_BIG = 19 * W_BR
OFF_SMALL_A = 4 * W_BR
OFF_SMALL_C = OFF_SMALL_A + 2 * H_A + 9 * W_BR
OFF_GATE = OFF_SMALL_C + 2 * H_C
WIN_A, WIN_C = OFF_SMALL_A // LANES, OFF_SMALL_C // LANES
LANE_BETA, LANE_ALPHA = 0, H_A
LANE_I = OFF_SMALL_C % LANES
LANE_F = LANE_I + H_C
SHIFT_A, SHIFT_BC, SHIFT_G = 0, 2 * H_A, 2 * H_A + 2 * H_C
FIRST_BC_BLOCK, FIRST_G_BLOCK = COL_QB, COL_GATE


def _params(*sem):
    return pltpu.CompilerParams(dimension_semantics=sem, vmem_limit_bytes=VMEM_LIMIT)


def _sigmoid(x):
    return 1.0 / (1.0 + jnp.exp(-x))


def _silu(x):
    return x * _sigmoid(x)


def _softplus(x):
    return jnp.maximum(x, 0.0) + jnp.log(1.0 + jnp.exp(-jnp.abs(x)))


def _dot(a, b, precision=None):
    return jnp.dot(a, b, precision=precision, preferred_element_type=F32)


def _dot_nt(a, b, precision=None):
    return lax.dot_general(a, b, (((1,), (1,)), ((), ())), precision=precision,
                           preferred_element_type=F32)


def _dot_tn(a, b, precision=None):
    return lax.dot_general(a, b, (((0,), (0,)), ((), ())), precision=precision,
                           preferred_element_type=F32)


def _cumsum_rows(x):
    rid = lax.broadcasted_iota(jnp.int32, x.shape, 0)
    d = 1
    while d < x.shape[0]:
        x = x + jnp.where(rid >= d, pltpu.roll(x, d, 0), 0.0)
        d *= 2
    return x


def _chunk_len(seq):
    return min(CHUNK, -(-seq // BF16_ROWS) * BF16_ROWS)


def _branch_dtype(block_rows):
    return BF16 if block_rows % BF16_ROWS == 0 else F32


def _pick_tile(n, candidates):
    for c in candidates:
        if n % c == 0:
            return c
    return n


def _rms_kernel(x_ref, g_ref, wa_ref, wc_ref, o_ref, ps_ref):
    x = x_ref[...]
    y = x * lax.rsqrt(jnp.mean(x * x, axis=-1, keepdims=True) + EPS)
    h = (y * g_ref[...]).astype(o_ref.dtype)
    o_ref[...] = h
    ps_ref[:, 0:LANES] = _dot_nt(h, wa_ref[0].astype(BF16))
    ps_ref[:, LANES:2 * LANES] = _dot_nt(h, wc_ref[0].astype(BF16))


def rms_cast(x, g, w_t, layer):
    t, d = x.shape
    tm = _pick_tile(t, (512, 256, 128))
    win = lambda c: pl.BlockSpec((1, LANES, d), lambda i, c=c: (layer, c, 0))
    return pl.pallas_call(
        _rms_kernel, grid=(t // tm,),
        in_specs=[pl.BlockSpec((tm, d), lambda i: (i, 0)), pl.BlockSpec((1, d), lambda i: (0, 0)),
                  win(WIN_A), win(WIN_C)],
        out_specs=[pl.BlockSpec((tm, d), lambda i: (i, 0)), pl.BlockSpec((tm, 2 * LANES), lambda i: (i, 0))],
        out_shape=[jax.ShapeDtypeStruct((t, d), BF16), jax.ShapeDtypeStruct((t, 2 * LANES), F32)],
        compiler_params=_params("arbitrary"), name="rms_cast")(x, g.reshape(1, d), w_t, w_t)


def _in_proj_kernel(a_ref, a2_ref, w_hbm, o_ref, o2_ref, wf_s, wb_s, sem, *, layer, tn):
    j = pl.program_id(0)
    n_j = pl.num_programs(0)

    def w_copy(jj, slot):
        shift = jnp.where(jj < FIRST_BC_BLOCK, SHIFT_A // SUBLANES,
                          jnp.where(jj < FIRST_G_BLOCK, SHIFT_BC // SUBLANES, SHIFT_G // SUBLANES))
        row0 = pl.multiple_of((jj * (tn // SUBLANES) + shift) * SUBLANES, SUBLANES)
        return pltpu.make_async_copy(w_hbm.at[layer, pl.ds(row0, tn), :], wf_s.at[slot], sem.at[slot])

    @pl.when(pl.program_id(1) == 0)
    def _first():
        slot = j % 2

        @pl.when(j == 0)
        def _prologue():
            w_copy(0, 0).start()

        @pl.when(j + 1 < n_j)
        def _prefetch():
            w_copy(j + 1, 1 - slot).start()

        w_copy(j, slot).wait()
        wb_s[...] = wf_s[slot].astype(BF16)
        o2_ref[...] = _dot_nt(a2_ref[...], wb_s[...])

    o_ref[...] = _dot_nt(a_ref[...], wb_s[...])


def in_proj(h, h2, w_t, layer):
    t, k = h.shape
    t2 = h2.shape[0]
    tm = _pick_tile(t, (1024, 768, 512, 256, 128))
    tn = W_BR
    n_i = t // tm
    row = lambda j, i: jnp.where(j % 2 == 0, i, n_i - 1 - i)
    return pl.pallas_call(
        functools.partial(_in_proj_kernel, layer=layer, tn=tn), grid=(N_BIG // tn, n_i),
        in_specs=[pl.BlockSpec((tm, k), lambda j, i: (row(j, i), 0)),
                  pl.BlockSpec((t2, k), lambda j, i: (0, 0)),
                  pl.BlockSpec(memory_space=pl.ANY)],
        out_specs=[pl.BlockSpec((tm, tn), lambda j, i: (row(j, i), j)),
                   pl.BlockSpec((t2, tn), lambda j, i: (0, j))],
        out_shape=[jax.ShapeDtypeStruct((t, N_BIG), F32), jax.ShapeDtypeStruct((t2, N_BIG), F32)],
        scratch_shapes=[pltpu.VMEM((2, tn, k), F32), pltpu.VMEM((tn, k), BF16), pltpu.SemaphoreType.DMA((2,))],
        compiler_params=_params("arbitrary", "arbitrary"), name="in_proj")(h, h2, w_t)


def _merge_kernel(oa_ref, ob_ref, oc_ref, zb_ref, ga_ref, gb_ref, gc_ref, wa_ref, wb_ref, wc_ref, o_ref,
                  wa_s, wb_s, wc_s):
    @pl.when(pl.program_id(1) == 0)
    def _cast():
        wa_s[...] = wa_ref[...].astype(BF16)
        wb_s[...] = wb_ref[...].astype(BF16)
        wc_s[...] = wc_ref[...].astype(BF16)

    ob = ob_ref[...] * _silu(zb_ref[...])
    m = _sigmoid(ga_ref[...]) * _dot(oa_ref[...].astype(BF16), wa_s[...])
    m += _sigmoid(gb_ref[...]) * _dot(ob.astype(BF16), wb_s[...])
    m += _sigmoid(gc_ref[...]) * _dot(oc_ref[...].astype(BF16), wc_s[...])
    o_ref[...] = m.astype(o_ref.dtype)


def merge(oa, ob, oc, p, wa, wb, wc, layer):
    t = oa.shape[0]
    tm = _pick_tile(t, (512, 256, 128))
    tn = W_BR
    row = lambda j, i: (i, 0)
    wspec = pl.BlockSpec((None, W_BR, tn), lambda j, i: (layer, 0, j), pipeline_mode=pl.Buffered(1))
    return pl.pallas_call(
        _merge_kernel, grid=(D_MODEL // tn, t // tm),
        in_specs=[pl.BlockSpec((tm, W_BR), row), pl.BlockSpec((tm, W_BR), row), pl.BlockSpec((tm, W_BR), row),
                  pl.BlockSpec((tm, W_BR), lambda j, i: (i, COL_ZB)),
                  pl.BlockSpec((tm, tn), lambda j, i: (i, COL_GATE + j)),
                  pl.BlockSpec((tm, tn), lambda j, i: (i, COL_GATE + 2 + j)),
                  pl.BlockSpec((tm, tn), lambda j, i: (i, COL_GATE + 4 + j)),
                  wspec, wspec, wspec],
        out_specs=pl.BlockSpec((tm, tn), lambda j, i: (i, j)),
        out_shape=jax.ShapeDtypeStruct((t, D_MODEL), BF16),
        scratch_shapes=[pltpu.VMEM((W_BR, tn), BF16)] * 3,
        compiler_params=_params("arbitrary", "arbitrary"), name="merge")(oa, ob, oc, p, p, p, p, wa, wb, wc)


def _outproj_kernel(x_ref, m_ref, w_ref, o_ref, w_s):
    @pl.when(pl.program_id(1) == 0)
    def _cast():
        w_s[...] = w_ref[...].astype(BF16)

    o_ref[...] = x_ref[...] + _dot(m_ref[...], w_s[...])


def out_proj(x, m, w, layer):
    t, d = x.shape
    tm = _pick_tile(t, (512, 256, 128))
    tn = 1024
    return pl.pallas_call(
        _outproj_kernel, grid=(d // tn, t // tm),
        in_specs=[pl.BlockSpec((tm, tn), lambda j, i: (i, j)), pl.BlockSpec((tm, d), lambda j, i: (i, 0)),
                  pl.BlockSpec((None, d, tn), lambda j, i: (layer, 0, j), pipeline_mode=pl.Buffered(1))],
        out_specs=pl.BlockSpec((tm, tn), lambda j, i: (i, j)),
        out_shape=jax.ShapeDtypeStruct((t, d), F32),
        scratch_shapes=[pltpu.VMEM((d, tn), BF16)],
        compiler_params=_params("arbitrary", "arbitrary"), name="out_proj")(x, m, w)


def _bdot(a, b):
    return _dot(a.astype(BF16), b.astype(BF16))


DELTA_CHUNKS_PER_STEP = 4


def _delta_kernel(q_ref, k_ref, v_ref, z_ref, sm_ref, cw_ref, cb_ref, al_ref, dtb_ref, s0_ref, ng_ref,
                  o_ref, cn_ref, sn_ref, xp_ref, s_ref, *, c_in, chunk):
    c = pl.program_id(1)
    C = chunk
    hd = DK_A

    @pl.when(c == 0)
    def _init():
        xp_ref[...] = jnp.zeros(xp_ref.shape, F32)
        xp_ref[0:SUBLANES, :] = cb_ref[0]
        s_ref[...] = s0_ref[0]

    xp_ref[SUBLANES:SUBLANES + c_in, 0:W_BR] = q_ref[...]
    xp_ref[SUBLANES:SUBLANES + c_in, W_BR:2 * W_BR] = k_ref[...]
    xp_ref[SUBLANES:SUBLANES + c_in, 2 * W_BR:3 * W_BR] = v_ref[...]

    nch = max(c_in // C, 1)
    padded = c_in < C
    valid = lax.broadcasted_iota(jnp.int32, (C, 1), 0) < c_in
    mask = (lambda x, fill: jnp.where(valid, x, fill)) if padded else (lambda x, fill: x)
    ri = lax.broadcasted_iota(jnp.int32, (C, C), 0)
    ci = lax.broadcasted_iota(jnp.int32, (C, C), 1)
    causal = ci <= ri
    strict = ci < ri

    beta_c, gcum_c, gcum_t_c, e_g_c, e_end_c, e_last_c = [], [], [], [], [], []
    for kc in range(nch):
        if padded:
            sm = jnp.concatenate([sm_ref[...], jnp.zeros((C - c_in, LANES), F32)], axis=0)
        else:
            sm = sm_ref[kc * C:(kc + 1) * C, :]
        g = mask(-jnp.exp(al_ref[...]) * _softplus(sm + dtb_ref[...]), 0.0)
        gcum = _cumsum_rows(g)
        g_last = gcum[C - 1:C, :]
        beta_c.append(_sigmoid(sm))
        gcum_c.append(gcum)
        gcum_t_c.append(gcum.T)
        e_g_c.append(jnp.exp(gcum))
        e_end_c.append(jnp.exp(g_last - gcum))
        e_last_c.append(jnp.exp(g_last))

    def conv(kc, col):
        sl = slice(col, col + hd)
        r0 = 5 + kc * C
        y = xp_ref[r0:r0 + C, sl] * cw_ref[0:1, sl]
        for j in range(1, CONV_W):
            y = y + xp_ref[r0 + j:r0 + j + C, sl] * cw_ref[j:j + 1, sl]
        return mask(_silu(y), 0.0)

    heads = range(H_A)
    pairs = [(kc, h) for kc in range(nch) for h in heads]
    npair = range(len(pairs))
    col = lambda vec, kc, lane: vec[kc][:, lane:lane + 1]
    q_l = [conv(kc, h * hd) for kc, h in pairs]
    k_l = [conv(kc, W_BR + h * hd) for kc, h in pairs]
    v_l = [conv(kc, 2 * W_BR + h * hd) for kc, h in pairs]
    q_l = [q * lax.rsqrt(jnp.sum(q * q, axis=-1, keepdims=True) + EPS) * (DK_A ** -0.5) for q in q_l]
    k_l = [k * lax.rsqrt(jnp.sum(k * k, axis=-1, keepdims=True) + EPS) for k in k_l]
    decay_l = [jnp.exp(jnp.where(causal, col(gcum_c, kc, LANE_ALPHA + h)
                                 - gcum_t_c[kc][LANE_ALPHA + h:LANE_ALPHA + h + 1, :], NEG_INF))
               for kc, h in pairs]
    b_l = [col(beta_c, kc, LANE_BETA + h) for kc, h in pairs]
    eg_l = [col(e_g_c, kc, LANE_ALPHA + h) for kc, h in pairs]
    kb_l = [k.astype(BF16) for k in k_l]
    qk_kk = [_dot_nt(jnp.concatenate([q_l[i].astype(BF16), kb_l[i]], axis=0), kb_l[i]) for i in npair]
    a_l = [jnp.where(strict, b_l[i] * decay_l[i] * qk_kk[i][C:2 * C], 0.0) for i in npair]
    r_l = [-a for a in a_l]
    pw_l = a_l
    n = 2
    while n < C:
        pw_l = [_bdot(pw, pw) for pw in pw_l]
        r_l = [r_l[i] + pw_l[i] + _bdot(r_l[i], pw_l[i]) for i in npair]
        n *= 2
    rhs_l = [jnp.concatenate([b_l[i] * v_l[i], (b_l[i] * eg_l[i]) * k_l[i]], axis=1) for i in npair]
    sol_l = [rhs_l[i] + _bdot(r_l[i], rhs_l[i]) for i in npair]
    kq_l = [jnp.concatenate([sol_l[i][:, hd:2 * hd], q_l[i] * eg_l[i]], axis=0).astype(BF16) for i in npair]
    p_l = [(decay_l[i] * qk_kk[i][0:C]).astype(BF16) for i in npair]
    kend_l = [(k_l[i] * col(e_end_c, kc, LANE_ALPHA + h)).astype(BF16) for i, (kc, h) in enumerate(pairs)]

    s_l = [s_ref[h] for h in heads]
    o_l = []
    for kc in range(nch):
        idx = [kc * H_A + h for h in heads]
        kq_s = [_dot(kq_l[i], s_l[h].astype(BF16)) for h, i in enumerate(idx)]
        wb_l = [(sol_l[i][:, 0:hd] - kq_s[h][0:C]).astype(BF16) for h, i in enumerate(idx)]
        o_l += [kq_s[h][C:2 * C] + _dot(p_l[i], wb_l[h]) for h, i in enumerate(idx)]
        s_l = [col(e_last_c, kc, LANE_ALPHA + h) * s_l[h] + _dot_tn(kend_l[i], wb_l[h])
               for h, i in enumerate(idx)]
    for h in heads:
        s_ref[h] = s_l[h]
    for i, (kc, h) in enumerate(pairs):
        o = o_l[i]
        o = o * lax.rsqrt(jnp.mean(o * o, axis=-1, keepdims=True) + EPS) * ng_ref[...]
        rows = slice(0, c_in) if padded else slice(kc * C, (kc + 1) * C)
        o = o[0:min(c_in, C)] * _silu(z_ref[rows, h * hd:(h + 1) * hd])
        o_ref[rows, h * hd:(h + 1) * hd] = o.astype(o_ref.dtype)

    tail = xp_ref[c_in:c_in + SUBLANES, :]
    xp_ref[0:SUBLANES, :] = tail

    @pl.when(c == pl.num_programs(1) - 1)
    def _fin():
        cn_ref[0] = tail
        sn_ref[0] = s_ref[...]


def delta_branch(p, ps, conv_w, conv_buf, a_log, dt_bias, s0, st, norm_g, bsz, seq):
    chunk = _chunk_len(seq)
    c_in = min(seq, DELTA_CHUNKS_PER_STEP * chunk)
    assert seq % c_in == 0 and (c_in % chunk == 0 or c_in < chunk)
    n_chunks = seq // c_in
    lane_vec = lambda v: jnp.zeros((1, LANES), F32).at[0, LANE_ALPHA:LANE_ALPHA + H_A].set(v)
    rb = lambda col: pl.BlockSpec((c_in, W_BR), lambda b, c, col=col: (b * n_chunks + c, col))
    const2 = lambda b, c: (0, 0)
    o, cn, sn = pl.pallas_call(
        functools.partial(_delta_kernel, c_in=c_in, chunk=chunk), grid=(bsz, n_chunks),
        in_specs=[rb(COL_QA), rb(COL_KA), rb(COL_VA), rb(COL_ZA),
                  pl.BlockSpec((c_in, LANES), lambda b, c: (b * n_chunks + c, 0)),
                  pl.BlockSpec((CONV_W, 3 * W_BR), const2),
                  pl.BlockSpec((None, 1, SUBLANES, 3 * W_BR), lambda b, c: (st, b, 0, 0)),
                  pl.BlockSpec((1, LANES), const2), pl.BlockSpec((1, LANES), const2),
                  pl.BlockSpec((None, 1, H_A, DK_A, DK_A), lambda b, c: (st, b, 0, 0, 0)),
                  pl.BlockSpec((1, DK_A), const2)],
        out_specs=[pl.BlockSpec((c_in, W_BR), lambda b, c: (b * n_chunks + c, 0)),
                   pl.BlockSpec((1, SUBLANES, 3 * W_BR), lambda b, c: (b, 0, 0)),
                   pl.BlockSpec((1, H_A, DK_A, DK_A), lambda b, c: (b, 0, 0, 0))],
        out_shape=[jax.ShapeDtypeStruct((bsz * seq, W_BR), _branch_dtype(c_in)),
                   jax.ShapeDtypeStruct((bsz, SUBLANES, 3 * W_BR), F32),
                   jax.ShapeDtypeStruct((bsz, H_A, DK_A, DK_A), F32)],
        scratch_shapes=[pltpu.VMEM((SUBLANES + max(c_in, chunk), 3 * W_BR), F32),
                        pltpu.VMEM((H_A, DK_A, DK_A), F32)],
        compiler_params=_params("arbitrary", "arbitrary"), name="delta")(
            p, p, p, p, ps, conv_w, conv_buf, lane_vec(a_log), lane_vec(dt_bias), s0, norm_g.reshape(1, DK_A))
    return o, cn[:, SUBLANES - (CONV_W - 1):], sn


MLSTM_CHUNKS_PER_STEP = 4


def _mlstm_kernel(q_ref, k_ref, v_ref, z_ref, og_ref, sm_ref, bif_ref, c0_ref, n0_ref, m0_ref, ng_ref,
                  o_ref, cn_ref, nn_ref, mn_ref, c_s, n_s, m_s, *, c_in, chunk):
    c = pl.program_id(1)
    C = chunk
    hd = DK_C

    @pl.when(c == 0)
    def _init():
        c_s[...] = c0_ref[0]
        n_s[...] = n0_ref[0]
        m_s[...] = m0_ref[0]

    nch = max(c_in // C, 1)
    padded = c_in < C

    def rows_of(ref, kc, sl):
        if padded:
            return jnp.concatenate([ref[:, sl], jnp.zeros((C - c_in, sl.stop - sl.start), F32)], axis=0)
        return ref[kc * C:(kc + 1) * C, sl]

    valid = lax.broadcasted_iota(jnp.int32, (C, 1), 0) < c_in
    ri = lax.broadcasted_iota(jnp.int32, (C, C), 0)
    ci = lax.broadcasted_iota(jnp.int32, (C, C), 1)
    causal = ci <= ri
    m_old = m_s[...]
    lane = lax.broadcasted_iota(jnp.int32, m_old.shape, 1)
    heads = range(H_C)
    hsl = [slice(h * hd, (h + 1) * hd) for h in heads]
    pairs = [(kc, h) for kc in range(nch) for h in heads]

    i_c, i_t_c, b_c, b_t_c = [], [], [], []
    for kc in range(nch):
        pre = rows_of(sm_ref, kc, slice(0, LANES)) + bif_ref[...]
        i_pre = jnp.where(valid, pre, NEG_INF) if padded else pre
        log_f = -_softplus(-pre)
        if padded:
            log_f = jnp.where(valid, log_f, 0.0)
        bcum = _cumsum_rows(log_f)
        i_c.append(i_pre)
        i_t_c.append(i_pre.T)
        b_c.append(bcum)
        b_t_c.append(bcum.T)

    q_l = [rows_of(q_ref, kc, hsl[h]) for kc, h in pairs]
    ks_l = [rows_of(k_ref, kc, hsl[h]) * (DK_C ** -0.5) for kc, h in pairs]
    vb_l = [rows_of(v_ref, kc, hsl[h]).astype(BF16) for kc, h in pairs]
    qb_l = [q.astype(BF16) for q in q_l]
    qk_l = [_dot_nt(qb_l[i], ks_l[i].astype(BF16)) for i in range(len(pairs))]
    bcol_l = [b_c[kc][:, LANE_F + h:LANE_F + h + 1] for kc, h in pairs]
    icol_l = [i_c[kc][:, LANE_I + h:LANE_I + h + 1] for kc, h in pairs]
    intra_l = [jnp.where(causal, bcol_l[i] - b_t_c[kc][LANE_F + h:LANE_F + h + 1, :]
                         + i_t_c[kc][LANE_I + h:LANE_I + h + 1, :], NEG_INF) for i, (kc, h) in enumerate(pairs)]
    imax_l = [jnp.max(x, axis=-1, keepdims=True) for x in intra_l]
    m_prev = [m_old[:, h:h + 1] for h in heads]
    mt_l, winter_l, wc_l, wj_l = [], [], [], []
    for i, (kc, h) in enumerate(pairs):
        inter = bcol_l[i] + m_prev[h]
        m_t = jnp.maximum(inter, imax_l[i])
        m_new = m_t[C - 1:C, :]
        b_last = bcol_l[i][C - 1:C, :]
        mt_l.append(m_t)
        winter_l.append(jnp.exp(inter - m_t))
        wc_l.append(jnp.exp(b_last + m_prev[h] - m_new))
        wj_l.append(jnp.exp(b_last - bcol_l[i] + icol_l[i] - m_new))
        m_prev[h] = m_new
    m_new_vec = m_old
    for h in heads:
        m_new_vec = jnp.where(lane == h, m_prev[h], m_new_vec)
    m_s[...] = m_new_vec

    s_l = [qk_l[i] * jnp.exp(intra_l[i] - mt_l[i]) for i in range(len(pairs))]
    sv_l = [_dot(s_l[i].astype(BF16), vb_l[i]) for i in range(len(pairs))]
    ssum_l = [jnp.sum(s, axis=-1, keepdims=True) for s in s_l]
    kw_l = [wj_l[i] * ks_l[i] for i in range(len(pairs))]
    upd_l = [_dot_tn(kw_l[i].astype(BF16), vb_l[i]) for i in range(len(pairs))]
    nsum_l = [jnp.sum(kw, axis=0, keepdims=True) for kw in kw_l]

    cm_l = [c_s[h] for h in heads]
    n_l = [n_s[h:h + 1, :] for h in heads]
    for i, (kc, h) in enumerate(pairs):
        num = winter_l[i] * _dot(qb_l[i], cm_l[h].astype(BF16)) + sv_l[i]
        den = winter_l[i] * jnp.sum(q_l[i] * n_l[h], axis=-1, keepdims=True) + ssum_l[i]
        hh = num / jnp.maximum(jnp.abs(den), jnp.exp(-mt_l[i]))
        cm_l[h] = wc_l[i] * cm_l[h] + upd_l[i]
        n_l[h] = wc_l[i] * n_l[h] + nsum_l[i]
        rows = slice(0, c_in) if padded else slice(kc * C, (kc + 1) * C)
        hc = _sigmoid(og_ref[rows, hsl[h]]) * hh[0:min(c_in, C)]
        hc = hc * lax.rsqrt(jnp.mean(hc * hc, axis=-1, keepdims=True) + EPS) * ng_ref[...]
        o_ref[rows, hsl[h]] = (hc * _silu(z_ref[rows, hsl[h]])).astype(o_ref.dtype)
    for h in heads:
        c_s[h] = cm_l[h]
        n_s[h:h + 1, :] = n_l[h]

    @pl.when(c == pl.num_programs(1) - 1)
    def _fin():
        cn_ref[0] = c_s[...]
        nn_ref[0] = n_s[...]
        mn_ref[0] = m_new_vec


def mlstm_branch(p, ps, b_if, c0, n0, m0p, st, norm_g, bsz, seq):
    chunk = _chunk_len(seq)
    c_in = min(seq, MLSTM_CHUNKS_PER_STEP * chunk)
    assert seq % c_in == 0 and (c_in % chunk == 0 or c_in < chunk)
    n_chunks = seq // c_in
    bif = jnp.zeros((1, LANES), F32).at[0, LANE_I:LANE_I + 2 * H_C].set(b_if)
    rb = lambda col: pl.BlockSpec((c_in, W_BR), lambda b, c, col=col: (b * n_chunks + c, col))
    const2 = lambda b, c: (0, 0)
    st4 = pl.BlockSpec((1, H_C, DK_C, DK_C), lambda b, c: (b, 0, 0, 0))
    st3 = pl.BlockSpec((1, H_C, DK_C), lambda b, c: (b, 0, 0))
    stm = pl.BlockSpec((1, 1, LANES), lambda b, c: (b, 0, 0))
    in4 = pl.BlockSpec((None, 1, H_C, DK_C, DK_C), lambda b, c: (st, b, 0, 0, 0))
    in3 = pl.BlockSpec((None, 1, H_C, DK_C), lambda b, c: (st, b, 0, 0))
    inm = pl.BlockSpec((None, 1, 1, LANES), lambda b, c: (st, b, 0, 0))
    o, cn, nn, mn = pl.pallas_call(
        functools.partial(_mlstm_kernel, c_in=c_in, chunk=chunk), grid=(bsz, n_chunks),
        in_specs=[rb(COL_QC), rb(COL_KC), rb(COL_VC), rb(COL_ZC), rb(COL_OC),
                  pl.BlockSpec((c_in, LANES), lambda b, c: (b * n_chunks + c, 1)),
                  pl.BlockSpec((1, LANES), const2), in4, in3, inm, pl.BlockSpec((1, DK_C), const2)],
        out_specs=[pl.BlockSpec((c_in, W_BR), lambda b, c: (b * n_chunks + c, 0)), st4, st3, stm],
        out_shape=[jax.ShapeDtypeStruct((bsz * seq, W_BR), _branch_dtype(c_in)),
                   jax.ShapeDtypeStruct((bsz, H_C, DK_C, DK_C), F32),
                   jax.ShapeDtypeStruct((bsz, H_C, DK_C), F32),
                   jax.ShapeDtypeStruct((bsz, 1, LANES), F32)],
        scratch_shapes=[pltpu.VMEM((H_C, DK_C, DK_C), F32), pltpu.VMEM((H_C, DK_C), F32),
                        pltpu.VMEM((1, LANES), F32)],
        compiler_params=_params("arbitrary", "arbitrary"), name="mlstm")(
            p, p, p, p, p, ps, bif, c0, n0, m0p, norm_g.reshape(1, DK_C))
    return o, cn, nn, mn[:, 0, :H_C]


def _head_rms(x, g):
    return x * lax.rsqrt(jnp.mean(x * x, axis=-1, keepdims=True) + EPS) * g


def _qknorm_kernel(q_ref, k_ref, qg_ref, kg_ref, qn_ref, kn_ref):
    for h in range(H_B):
        sl = slice(h * HD_B, (h + 1) * HD_B)
        qn_ref[:, sl] = _head_rms(q_ref[:, sl], qg_ref[...]) * (HD_B ** -0.5)
        kn_ref[:, sl] = _head_rms(k_ref[:, sl], kg_ref[...])


def _qknorm_kv_kernel(q_ref, k_ref, v_ref, qg_ref, kg_ref, qn_ref, kn_ref, vo_ref, knb_ref, vt_ref, kbar_ref):
    vo_ref[...] = v_ref[...]
    for h in range(H_B):
        sl = slice(h * HD_B, (h + 1) * HD_B)
        qn_ref[:, sl] = _head_rms(q_ref[:, sl], qg_ref[...]) * (HD_B ** -0.5)
        kn = _head_rms(k_ref[:, sl], kg_ref[...])
        kn_ref[:, sl] = kn
        knb_ref[0, :, sl] = kn.astype(BF16)
        kbar_ref[0, :, sl] = jnp.mean(kn, axis=0, keepdims=True)
    eye = (lax.broadcasted_iota(jnp.int32, (HD_B, HD_B), 0)
           == lax.broadcasted_iota(jnp.int32, (HD_B, HD_B), 1)).astype(BF16)
    for h in range(H_B):
        sl = slice(h * HD_B, (h + 1) * HD_B)
        vt_ref[0, sl, :] = _dot_nt(eye, v_ref[:, sl].astype(BF16)).astype(BF16)


def qk_norm(p, qg, kg, with_kv):
    t = p.shape[0]
    qg = qg.reshape(1, HD_B)
    kg = kg.reshape(1, HD_B)
    gs = pl.BlockSpec((1, HD_B), lambda i: (0, 0))
    if not with_kv:
        tm = _pick_tile(t, (256, 128))
        return pl.pallas_call(
            _qknorm_kernel, grid=(t // tm,),
            in_specs=[pl.BlockSpec((tm, W_BR), lambda i: (i, COL_QB)),
                      pl.BlockSpec((tm, W_BR), lambda i: (i, COL_KB)), gs, gs],
            out_specs=[pl.BlockSpec((tm, W_BR), lambda i: (i, 0))] * 2,
            out_shape=[jax.ShapeDtypeStruct((t, W_BR), F32)] * 2,
            compiler_params=_params("arbitrary"), name="qk_norm")(p, p, qg, kg)
    tm = MOBA_BLOCK
    nb = t // tm
    return pl.pallas_call(
        _qknorm_kv_kernel, grid=(nb,),
        in_specs=[pl.BlockSpec((tm, W_BR), lambda i: (i, COL_QB)),
                  pl.BlockSpec((tm, W_BR), lambda i: (i, COL_KB)),
                  pl.BlockSpec((tm, W_BR), lambda i: (i, COL_VB)), gs, gs],
        out_specs=[pl.BlockSpec((tm, W_BR), lambda i: (i, 0))] * 3
                  + [pl.BlockSpec((1, tm, W_BR), lambda i: (i, 0, 0)),
                     pl.BlockSpec((1, W_BR, tm), lambda i: (i, 0, 0)),
                     pl.BlockSpec((1, 1, W_BR), lambda i: (i, 0, 0))],
        out_shape=[jax.ShapeDtypeStruct((t, W_BR), F32)] * 3
                  + [jax.ShapeDtypeStruct((nb, tm, W_BR), BF16), jax.ShapeDtypeStruct((nb, W_BR, tm), BF16),
                   jax.ShapeDtypeStruct((nb, 1, W_BR), F32)],
        compiler_params=_params("arbitrary"), name="qk_norm_kv")(p, p, p, qg, kg)


def _select_topk(sc, valid, axis):
    nb = sc.shape[axis]
    idx = lax.broadcasted_iota(jnp.int32, sc.shape, axis).astype(F32)
    if valid is not None:
        sc = jnp.where(valid, sc, NEG_INF)
    sel = jnp.zeros(sc.shape, F32)
    for _ in range(MOBA_TOPK):
        mx = jnp.max(sc, axis=axis, keepdims=True)
        first = jnp.min(jnp.where(sc == mx, idx, float(nb)), axis=axis, keepdims=True)
        hit = idx == first
        sel = jnp.where(hit & (mx > NEG_INF), 1.0, sel)
        sc = jnp.where(hit, NEG_INF, sc)
    return sel


MOBA_HEADS_PER_STEP = 8


def _moba_prompt_kernel(q_ref, k_ref, vt_ref, kbar_ref, o_ref, sel_ref, qb_ref, acc_ref):
    i = pl.program_id(2)
    blk = MOBA_BLOCK
    hp = MOBA_HEADS_PER_STEP
    nb = kbar_ref.shape[1]
    past = lax.broadcasted_iota(jnp.int32, (nb, blk), 0) < i
    kpos = lax.broadcasted_iota(jnp.int32, (blk, blk), 0)
    qpos = lax.broadcasted_iota(jnp.int32, (blk, blk), 1)
    heads = [slice(hh * HD_B, (hh + 1) * HD_B) for hh in range(hp)]

    ones = jnp.ones((BF16_ROWS, blk), BF16)

    def pv(n, sl, pr):
        return _dot(jnp.concatenate([vt_ref[n, sl, :], ones], axis=0), pr)

    for hh, sl in enumerate(heads):
        qb_ref[hh] = (q_ref[:, sl] * LOG2E).astype(BF16)
    s_l = [_dot_nt(k_ref[i, :, sl], qb_ref[hh]) for hh, sl in enumerate(heads)]
    sc_l = [_dot_nt(kbar_ref[0, :, sl], q_ref[:, sl], HI) for sl in heads]
    ms, pr_l = [], []
    for hh in range(hp):
        s = jnp.where(kpos <= qpos, s_l[hh], NEG_INF)
        m = jnp.max(s, axis=0, keepdims=True)
        ms.append(m)
        pr_l.append(jnp.exp2(s - m).astype(BF16))
    for hh, sl in enumerate(heads):
        acc_ref[hh] = pv(i, sl, pr_l[hh])
        sel_ref[hh] = _select_topk(sc_l[hh], past, 0)

    def past_blocks(ns, ms):
        s_l = [[_dot_nt(k_ref[n, :, sl], qb_ref[hh]) for n in ns] for hh, sl in enumerate(heads)]
        ones_n = jnp.ones((BF16_ROWS, len(ns) * blk), BF16)
        ms_new, pr_l, alpha_l = [], [], []
        for hh in range(hp):
            ss = [jnp.where(sel_ref[hh, pl.ds(n, 1), :] > 0.0, s, NEG_INF) for n, s in zip(ns, s_l[hh])]
            m_new = ms[hh]
            for s in ss:
                m_new = jnp.maximum(m_new, jnp.max(s, axis=0, keepdims=True))
            alpha_l.append(jnp.exp2(ms[hh] - m_new))
            ms_new.append(m_new)
            pr_l.append(jnp.concatenate([jnp.exp2(s - m_new).astype(BF16) for s in ss], axis=0))
        for hh, sl in enumerate(heads):
            lhs = jnp.concatenate([jnp.concatenate([vt_ref[n, sl, :] for n in ns], axis=1), ones_n], axis=0)
            acc_ref[hh] = alpha_l[hh] * acc_ref[hh] + _dot(lhs, pr_l[hh])
        return tuple(ms_new)

    odd = i % 2
    ms = lax.cond(odd == 1, lambda ms: past_blocks([0], ms), lambda ms: ms, tuple(ms))
    lax.fori_loop(0, i // 2, lambda t, ms: past_blocks([2 * t + odd, 2 * t + odd + 1], ms), ms)
    for hh, sl in enumerate(heads):
        o_ref[:, sl] = (acc_ref[hh, 0:HD_B, :] / acc_ref[hh, HD_B:HD_B + 1, :]).T.astype(o_ref.dtype)


def moba_prompt(qn, knb, vt, kbar, bsz, seq):
    nb = seq // MOBA_BLOCK
    blk = MOBA_BLOCK
    hp = MOBA_HEADS_PER_STEP
    wid = hp * HD_B
    return pl.pallas_call(
        _moba_prompt_kernel, grid=(bsz, H_B // hp, nb),
        in_specs=[pl.BlockSpec((blk, wid), lambda b, h, i: (b * nb + i, h)),
                  pl.BlockSpec((nb, blk, wid), lambda b, h, i: (b, 0, h)),
                  pl.BlockSpec((nb, wid, blk), lambda b, h, i: (b, h, 0)),
                  pl.BlockSpec((1, nb, wid), lambda b, h, i: (b, 0, h))],
        out_specs=pl.BlockSpec((blk, wid), lambda b, h, i: (b * nb + i, h)),
        out_shape=jax.ShapeDtypeStruct((bsz * seq, W_BR), _branch_dtype(blk)),
        scratch_shapes=[pltpu.VMEM((hp, nb, blk), F32), pltpu.VMEM((hp, blk, HD_B), BF16),
                        pltpu.VMEM((hp, HD_B + BF16_ROWS, blk), F32)],
        compiler_params=_params("arbitrary", "arbitrary", "arbitrary"), name="moba_prompt")(qn, knb, vt, kbar)


DECODE_PAGES_PER_STEP = 16


def _moba_decode_kernel(pt_ref, q_ref, qf_ref, bias_ref, obias_ref, kown_ref, vown_ref, *rest, pps, seq):
    del pt_ref
    k_refs = rest[:pps]
    v_refs = rest[pps:2 * pps]
    o_ref = rest[2 * pps]
    m_s, l_s, o_s, ks_s = rest[2 * pps + 1:]
    n = pl.program_id(1)
    q = q_ref[0]
    ppb = MOBA_BLOCK // PAGE_SIZE
    npg = ks_s.shape[0] // H_B
    lane = lax.broadcasted_iota(jnp.int32, m_s.shape, 1)

    def partial_softmax(s):
        m = jnp.max(s, axis=-1, keepdims=True)
        pr = jnp.exp(s - m)
        return m, jnp.sum(pr, axis=-1, keepdims=True), pr

    @pl.when(n == 0)
    def _init():
        m_s[...] = jnp.zeros(m_s.shape, F32)
        l_s[...] = jnp.zeros(l_s.shape, F32)

    m_all = m_s[...]
    l_all = l_s[...]
    s_l = [_dot_nt(q, k_refs[j][0, 0].astype(BF16)) for j in range(pps)]
    pr_l = []
    for j in range(pps):
        m, l, pr = partial_softmax(s_l[j] + bias_ref[...])
        pg = n * pps + j
        m_all = jnp.where(lane == pg, m, m_all)
        l_all = jnp.where(lane == pg, l, l_all)
        pr_l.append(pr.astype(BF16))
    for j in range(pps):
        pg = n * pps + j
        o_s[pg] = _dot(pr_l[j], v_refs[j][0, 0].astype(BF16))
        ks_s[pl.ds(pg * H_B, H_B), :] = jnp.sum(k_refs[j][0, 0].reshape(PAGE_SIZE, H_B, HD_B), axis=0)
    m_s[...] = m_all
    l_s[...] = l_all

    @pl.when(n == pl.num_programs(1) - 1)
    def _combine():
        nb = npg // ppb
        m_o, l_o, pr_o = partial_softmax(_dot_nt(q, kown_ref[0].astype(BF16)) + obias_ref[...])
        o_s[npg] = _dot(pr_o.astype(BF16), vown_ref[0].astype(BF16))
        m_sl = jnp.where(lane == npg, m_o, m_all)
        l_sl = jnp.where(lane == npg, l_o, l_all)
        scp = _dot_nt(qf_ref[0], ks_s[...], HI)
        ch = lax.broadcasted_iota(jnp.int32, scp.shape, 1) % H_B
        rh = lax.broadcasted_iota(jnp.int32, scp.shape, 0) // seq
        scp = jnp.where(ch == rh, scp, 0.0)
        cols_per_blk = ppb * H_B
        gather = (lax.broadcasted_iota(jnp.int32, (npg * H_B, nb), 0) // cols_per_blk
                  == lax.broadcasted_iota(jnp.int32, (npg * H_B, nb), 1)).astype(F32)
        scb = _dot(scp, gather, HI) * (1.0 / MOBA_BLOCK)
        sel = _select_topk(scb, None, 1)
        slot = lax.broadcasted_iota(jnp.int32, (nb, m_s.shape[1]), 1)
        expand = (slot // ppb == lax.broadcasted_iota(jnp.int32, (nb, m_s.shape[1]), 0)) & (slot < npg)
        sel_slot = jnp.where(lane == npg, 1.0, _dot(sel.astype(BF16), expand.astype(BF16)))
        m_sel = jnp.where(sel_slot > 0.0, m_sl, NEG_INF)
        w = sel_slot * jnp.exp(m_sel - jnp.max(m_sel, axis=-1, keepdims=True))
        w = w / jnp.sum(w * l_sl, axis=-1, keepdims=True)
        acc = w[:, 0:1] * o_s[0]
        for pg in range(1, npg + 1):
            acc = acc + w[:, pg:pg + 1] * o_s[pg]
        o_ref[0] = acc


def moba_decode(qn, kn, v, cache_k, cache_v, page_table, layer, bsz, seq):
    ppb = MOBA_BLOCK // PAGE_SIZE
    pps = DECODE_PAGES_PER_STEP
    n_pages = page_table.shape[1]
    nrow = H_B * seq
    assert nrow % SUBLANES == 0 and n_pages // ppb >= MOBA_TOPK and n_pages % ppb == 0
    assert n_pages % pps == 0 and n_pages < LANES
    rows = PAGE_SIZE * H_B
    depth, n_phys = cache_k.shape[:2]
    ck = cache_k.reshape(depth, n_phys, rows, HD_B)
    cv = cache_v.reshape(depth, n_phys, rows, HD_B)
    qc = jnp.transpose(qn.reshape(bsz, seq, H_B, HD_B), (0, 2, 1, 3)).reshape(bsz, nrow, HD_B)
    c_h, c_q = jnp.arange(nrow) // seq, jnp.arange(nrow) % seq
    key_h = jnp.arange(rows) % H_B
    bias = jnp.where(c_h[:, None] == key_h[None, :], 0.0, NEG_INF).astype(F32)
    own = jnp.arange(seq * H_B)
    obias = jnp.where((c_h[:, None] == own[None, :] % H_B) & (own[None, :] // H_B <= c_q[:, None]),
                      0.0, NEG_INF).astype(F32)
    kown = kn.reshape(bsz, seq * H_B, HD_B)
    vown = v.reshape(bsz, seq * H_B, HD_B)

    page = lambda j: pl.BlockSpec((1, 1, rows, HD_B), lambda b, n, pt, j=j: (layer, pt[b, n * pps + j], 0, 0))
    seq3 = lambda shape: pl.BlockSpec((1,) + shape, lambda b, n, pt: (b, 0, 0))
    const2 = lambda shape: pl.BlockSpec(shape, lambda b, n, pt: (0, 0))
    grid_spec = pltpu.PrefetchScalarGridSpec(
        num_scalar_prefetch=1, grid=(bsz, n_pages // pps),
        in_specs=[seq3((nrow, HD_B)), seq3((nrow, HD_B)), const2((nrow, rows)), const2((nrow, seq * H_B)),
                  seq3((seq * H_B, HD_B)), seq3((seq * H_B, HD_B))]
                 + [page(j) for j in range(pps)] + [page(j) for j in range(pps)],
        out_specs=seq3((nrow, HD_B)),
        scratch_shapes=[pltpu.VMEM((nrow, LANES), F32), pltpu.VMEM((nrow, LANES), F32),
                        pltpu.VMEM((n_pages + 1, nrow, HD_B), F32), pltpu.VMEM((n_pages * H_B, HD_B), F32)])
    out = pl.pallas_call(
        functools.partial(_moba_decode_kernel, pps=pps, seq=seq), grid_spec=grid_spec,
        out_shape=jax.ShapeDtypeStruct((bsz, nrow, HD_B), F32),
        compiler_params=_params("arbitrary", "arbitrary"), name="moba_decode")(
            page_table, qc.astype(BF16), qc, bias, obias, kown, vown, *([ck] * pps), *([cv] * pps))
    return jnp.transpose(out.reshape(bsz, H_B, seq, HD_B), (0, 2, 1, 3)).reshape(bsz * seq, W_BR)


def _mixer(x, p, ps, lw, states, st, bsz, seq, attend):
    (conv_a, a_log, dt_bias, norm_a, qnorm_b, knorm_b, b_if, norm_c, wa, wb, wc, wo, layer) = lw
    conv_buf, s_a, c_c, n_c, m_c = states
    oa, conv_new, s_new = delta_branch(p, ps, conv_a, conv_buf, a_log, dt_bias, s_a, st, norm_a, bsz, seq)
    ob, kn, vb = attend(p, qnorm_b, knorm_b)
    oc, c_new, n_new, m_new = mlstm_branch(p, ps, b_if, c_c, n_c, m_c, st, norm_c, bsz, seq)
    y = out_proj(x, merge(oa, ob, oc, p, wa, wb, wc, layer), wo, layer)
    kv_shape = (bsz, seq, H_B, HD_B)
    return y, (kn.reshape(kv_shape), vb.reshape(kv_shape), conv_new, s_new, c_new, n_new, m_new)


def _pad_states(conv, delta, c, n, m):
    conv = jnp.pad(conv, ((0, 0), (0, 0), (SUBLANES - (CONV_W - 1), 0), (0, 0)))
    m = jnp.pad(m, ((0, 0), (0, 0), (0, LANES - H_C)))[:, :, None, :]
    return conv, delta, c, n, m


def kernel(x_prompt, x_sample, cache_k, cache_v, page_table, state_conv_a, state_delta_a, state_mlstm_c,
           state_mlstm_n, state_mlstm_m, ln_g, w_in, conv_a, a_log, dt_bias, norm_a, qnorm_b, knorm_b,
           b_if, norm_c, w_br_a, w_br_b, w_br_c, w_out):
    bp, sp, d = x_prompt.shape
    bs, ss, _ = x_sample.shape
    depth = w_in.shape[0]
    yp = x_prompt.reshape(bp * sp, d)
    ys = x_sample.reshape(bs * ss, d)
    new_p = [[] for _ in range(7)]
    new_s = [[] for _ in range(7)]
    w_t = jnp.swapaxes(w_in, 1, 2)
    zeros = lambda *s: jnp.zeros(s, F32)
    states_p = _pad_states(zeros(1, bp, CONV_W - 1, 3 * W_BR), zeros(1, bp, H_A, DK_A, DK_A),
                           zeros(1, bp, H_C, DK_C, DK_C), zeros(1, bp, H_C, DK_C), zeros(1, bp, H_C))
    states_s = _pad_states(state_conv_a, state_delta_a, state_mlstm_c, state_mlstm_n, state_mlstm_m)
    for l in range(depth):
        lw = (conv_a[l], a_log[l], dt_bias[l], norm_a[l], qnorm_b[l],
              knorm_b[l], b_if[l], norm_c[l], w_br_a, w_br_b, w_br_c, w_out, l)

        def attend_prompt(p, qg, kg):
            qn, kn, vb, knb, vt, kbar = qk_norm(p, qg, kg, with_kv=True)
            ob = moba_prompt(qn, knb, vt, kbar.reshape(bp, sp // MOBA_BLOCK, W_BR), bp, sp)
            return ob, kn, vb

        def attend_sample(p, qg, kg, l=l):
            qn, kn = qk_norm(p, qg, kg, with_kv=False)
            vb = p[:, COL_VB * W_BR:(COL_VB + 1) * W_BR]
            return moba_decode(qn, kn, vb, cache_k, cache_v, page_table, l, bs, ss), kn, vb

        hp, narrow_p = rms_cast(yp, ln_g[l], w_t, l)
        hs, narrow_s = rms_cast(ys, ln_g[l], w_t, l)
        wide_p, wide_s = in_proj(hp, hs, w_t, l)
        yp, st_p = _mixer(yp, wide_p, narrow_p, lw, states_p, 0, bp, sp, attend_prompt)
        ys, st_s = _mixer(ys, wide_s, narrow_s, lw, states_s, l, bs, ss, attend_sample)
        for i in range(7):
            new_p[i].append(st_p[i])
            new_s[i].append(st_s[i])
    outs_p = [jnp.stack(t) for t in new_p]
    outs_s = [jnp.stack(t) for t in new_s]
    return (yp.reshape(bp, sp, d), ys.reshape(bs, ss, d), *outs_p, *outs_s)
```
